```python
import math
import jax
import jax.numpy as jnp
from jax import lax
import numpy as np

D_MODEL = 1024
BATCH = 8
SEQ = 4096
DEPTH = 2

CTX_LEN = 256
GRID_W = 64
N_MOD = 6
EPS = 1e-6
NEG_INF = -1e30

HEAD_DIM = 64
ATTN_WIDTH = D_MODEL // 2
N_HEADS = ATTN_WIDTH // HEAD_DIM
N_KV_HEADS = N_HEADS // 4
KV_WIDTH = N_KV_HEADS * HEAD_DIM
WINDOW = 128
ATT_BLOCK = 128
ROPE_BASE = 10000.0

SSM_INNER = D_MODEL - ATTN_WIDTH
SSM_HEAD_DIM = 64
SSM_HEADS = SSM_INNER // SSM_HEAD_DIM
SSM_GROUPS = 2
SSM_STATE = 64
CONV_W = 5
CONV_CH = SSM_INNER + 2 * SSM_GROUPS * SSM_STATE
CHUNK = 128
N_DIRS = 2

MIX_WIDTH = ATTN_WIDTH + SSM_INNER
IN_WIDTH = ATTN_WIDTH + 2 * KV_WIDTH + SSM_INNER + CONV_CH + N_DIRS * SSM_HEADS

N_EXPERTS = 32
TOP_K = 4
D_FF = D_MODEL
SWIGLU_LIMIT = 7.0
SWIGLU_ALPHA = 1.702
MOE_BLOCK = 128

kernel_name = 'hybrid_attn_ssd_moe_diffusion_block'


def rms_norm(x, w):
    xf = x.astype(jnp.float32)
    y = xf * lax.rsqrt(jnp.mean(xf * xf, axis=-1, keepdims=True) + EPS)
    return (y * w.astype(jnp.float32)).astype(x.dtype)


def modulate(h, shift, scale):
    return h * (1.0 + scale) + shift


def axial_rope_tables(row_pos, col_pos):
    n_freq = HEAD_DIM // 4
    inv_freq = ROPE_BASE ** (-jnp.arange(n_freq, dtype=jnp.float32) / n_freq)
    ang = jnp.stack([row_pos.astype(jnp.float32)[:, None] * inv_freq,
                     col_pos.astype(jnp.float32)[:, None] * inv_freq], axis=1)
    return jnp.cos(ang)[None, :, None], jnp.sin(ang)[None, :, None]


def apply_rope(t, cos, sin):
    tr = t.astype(jnp.float32).reshape(*t.shape[:-1], 2, 2, HEAD_DIM // 4)
    t1, t2 = tr[..., 0, :], tr[..., 1, :]
    out = jnp.stack([t1 * cos - t2 * sin, t2 * cos + t1 * sin], axis=-2)
    return out.reshape(t.shape).astype(t.dtype)


def split_in_proj(p):
    b, L = p.shape[0], p.shape[1]
    s1 = ATTN_WIDTH
    s2 = s1 + KV_WIDTH
    s3 = s2 + KV_WIDTH
    s4 = s3 + SSM_INNER
    s5 = s4 + CONV_CH
    q, k, v, z, xbc, dt = jnp.split(p, [s1, s2, s3, s4, s5], axis=-1)
    return (q.reshape(b, L, N_HEADS, HEAD_DIM), k.reshape(b, L, N_KV_HEADS, HEAD_DIM),
            v.reshape(b, L, N_KV_HEADS, HEAD_DIM), z, xbc, dt.reshape(b, L, N_DIRS, SSM_HEADS))


def latent_window_attention(q, k, v, k_c, v_c, sinks):
    b, S = q.shape[0], q.shape[1]
    nb = S // ATT_BLOCK
    rep = N_HEADS // N_KV_HEADS
    scale = HEAD_DIM ** -0.5
    qb = q.reshape(b, nb, ATT_BLOCK, N_KV_HEADS, rep, HEAD_DIM)

    def band(t):
        tp = jnp.pad(t, ((0, 0), (ATT_BLOCK, ATT_BLOCK), (0, 0), (0, 0)))
        tp = tp.reshape(b, nb + 2, ATT_BLOCK, N_KV_HEADS, HEAD_DIM)
        return jnp.concatenate([tp[:, :-2], tp[:, 1:-1], tp[:, 2:]], axis=2)

    kb, vb = band(k), band(v)
    s_win = jnp.einsum('bnqgrd,bnkgd->bngrqk', qb, kb).astype(jnp.float32) * scale
    qpos = jnp.arange(nb)[:, None] * ATT_BLOCK + jnp.arange(ATT_BLOCK)[None, :]
    kpos = jnp.arange(nb)[:, None] * ATT_BLOCK - ATT_BLOCK + jnp.arange(3 * ATT_BLOCK)[None, :]
    valid = ((jnp.abs(qpos[:, :, None] - kpos[:, None, :]) <= WINDOW)
             & (kpos[:, None, :] >= 0) & (kpos[:, None, :] < S))
    s_win = jnp.where(valid[None, :, None, None], s_win, NEG_INF)
    s_ctx = jnp.einsum('bnqgrd,bcgd->bngrqc', qb, k_c).astype(jnp.float32) * scale
    s_sink = jnp.broadcast_to(sinks.astype(jnp.float32).reshape(N_KV_HEADS, rep, 1, 1),
                              s_ctx.shape[:-1] + (1,))
    probs = jax.nn.softmax(jnp.concatenate([s_ctx, s_win, s_sink], axis=-1), axis=-1).astype(v.dtype)
    n_ctx = k_c.shape[1]
    out = (jnp.einsum('bngrqc,bcgd->bnqgrd', probs[..., :n_ctx], v_c)
           + jnp.einsum('bngrqk,bnkgd->bnqgrd', probs[..., n_ctx:n_ctx + 3 * ATT_BLOCK], vb))
    return out.reshape(b, S, ATTN_WIDTH)


def context_attention(q_c, k_c, v_c, sinks):
    b, n_ctx = q_c.shape[0], q_c.shape[1]
    rep = N_HEADS // N_KV_HEADS
    qcb = q_c.reshape(b, n_ctx, N_KV_HEADS, rep, HEAD_DIM)
    s = jnp.einsum('bqgrd,bkgd->bgrqk', qcb, k_c).astype(jnp.float32) * HEAD_DIM ** -0.5
    s_sink = jnp.broadcast_to(sinks.astype(jnp.float32).reshape(N_KV_HEADS, rep, 1, 1), s.shape[:-1] + (1,))
    probs = jax.nn.softmax(jnp.concatenate([s, s_sink], axis=-1), axis=-1).astype(v_c.dtype)
    out = jnp.einsum('bgrqk,bkgd->bqgrd', probs[..., :n_ctx], v_c)
    return out.reshape(b, n_ctx, ATTN_WIDTH)


def centred_dwconv(u, w, bias):
    out = lax.conv_general_dilated(
        u, w[:, None, :].astype(u.dtype), window_strides=(1,),
        padding=[(CONV_W // 2, CONV_W // 2)], dimension_numbers=('NWC', 'WIO', 'NWC'),
        feature_group_count=u.shape[-1])
    return out + bias.astype(u.dtype)


def conv_split(xbc, conv_w, conv_b):
    u = jax.nn.silu(centred_dwconv(xbc, conv_w, conv_b)).astype(jnp.float32)
    b, L = u.shape[0], u.shape[1]
    gs = SSM_GROUPS * SSM_STATE
    xs = u[..., :SSM_INNER].reshape(b, L, SSM_HEADS, SSM_HEAD_DIM)
    bm = u[..., SSM_INNER:SSM_INNER + gs].reshape(b, L, SSM_GROUPS, SSM_STATE)
    cm = u[..., SSM_INNER + gs:].reshape(b, L, SSM_GROUPS, SSM_STATE)
    return xs, bm, cm


def ssd_chunked(x, dt, a, bm, cm, h0):
    b, L, H, P = x.shape
    nc = L // CHUNK
    rep = H // SSM_GROUPS
    xc = x.reshape(b, nc, CHUNK, H, P)
    dtc = dt.reshape(b, nc, CHUNK, H)
    bh = jnp.repeat(bm, rep, axis=2).reshape(b, nc, CHUNK, H, SSM_STATE)
    ch = jnp.repeat(cm, rep, axis=2).reshape(b, nc, CHUNK, H, SSM_STATE)
    cs = jnp.cumsum(dtc * a, axis=2)
    lower = jnp.tril(jnp.ones((CHUNK, CHUNK), dtype=bool))[None, None, :, :, None]
    seg = cs[:, :, :, None, :] - cs[:, :, None, :, :]
    decay = jnp.where(lower, jnp.exp(jnp.where(lower, seg, 0.0)), 0.0)
    scores = jnp.einsum('bcihn,bcjhn->bcijh', ch, bh) * decay * dtc[:, :, None, :, :]
    y_diag = jnp.einsum('bcijh,bcjhp->bcihp', scores, xc)
    to_end = jnp.exp(cs[:, :, -1:, :] - cs) * dtc
    states = jnp.einsum('bcjhn,bcjh,bcjhp->bchpn', bh, to_end, xc)
    chunk_decay = jnp.exp(cs[:, :, -1, :])

    def step(h, inp):
        s, dcy = inp
        return dcy[:, :, None, None] * h + s, h

    h_final, h_prev = lax.scan(step, h0, (jnp.moveaxis(states, 1, 0), jnp.moveaxis(chunk_decay, 1, 0)))
    h_prev = jnp.moveaxis(h_prev, 0, 1)
    y_off = jnp.einsum('bcihn,bchpn,bcih->bcihp', ch, h_prev, jnp.exp(cs))
    return (y_diag + y_off).reshape(b, L, H, P), h_final


def gated_rms_norm(y, z, w):
    b, L = y.shape[0], y.shape[1]
    g = y.reshape(b, L, SSM_INNER) * jax.nn.silu(z.astype(jnp.float32))
    g = g.reshape(b, L, SSM_GROUPS, SSM_INNER // SSM_GROUPS)
    g = g * lax.rsqrt(jnp.mean(g * g, axis=-1, keepdims=True) + EPS)
    return (g.reshape(b, L, SSM_INNER) * w.astype(jnp.float32)).astype(z.dtype)


def bidirectional_ssd(xbc, xbc_c, z, z_c, dt_raw, dt_raw_c, conv_w, conv_b, dt_bias, a_log, d_skip, norm_w):
    xs, bm, cm = conv_split(xbc, conv_w, conv_b)
    xs_c, bm_c, cm_c = conv_split(xbc_c, conv_w, conv_b)
    skip = d_skip.astype(jnp.float32)[:, None]
    y = skip * xs
    y_c = skip * xs_c
    h0 = jnp.zeros((xs.shape[0], SSM_HEADS, SSM_HEAD_DIM, SSM_STATE), jnp.float32)
    for d in range(N_DIRS):
        if d == 0:
            orient = lambda t: t
        else:
            orient = lambda t: jnp.flip(t, axis=1)
        a = -jnp.exp(a_log[d].astype(jnp.float32))
        dt = jax.nn.softplus(dt_raw[:, :, d].astype(jnp.float32) + dt_bias[d].astype(jnp.float32))
        dt_c = jax.nn.softplus(dt_raw_c[:, :, d].astype(jnp.float32) + dt_bias[d].astype(jnp.float32))
        yc_d, h_ctx = ssd_chunked(orient(xs_c), orient(dt_c), a, orient(bm_c), orient(cm_c), h0)
        y_d, _ = ssd_chunked(orient(xs), orient(dt), a, orient(bm), orient(cm), h_ctx)
        y = y + orient(y_d)
        y_c = y_c + orient(yc_d)
    return gated_rms_norm(y, z, norm_w), gated_rms_norm(y_c, z_c, norm_w)


def hybrid_mixer(h, hc, cos, sin, w_in, conv_w, conv_b, dt_bias, a_log, d_skip,
                 ssm_norm_w, attn_sinks, attn_norm_w, w_out, with_ctx_out):
    q, k, v, z, xbc, dt_raw = split_in_proj(h @ w_in)
    q_c, k_c, v_c, z_c, xbc_c, dt_raw_c = split_in_proj(hc @ w_in)
    q = apply_rope(q, cos, sin)
    k = apply_rope(k, cos, sin)
    attn = rms_norm(latent_window_attention(q, k, v, k_c, v_c, attn_sinks), attn_norm_w)
    ssm, ssm_c = bidirectional_ssd(xbc, xbc_c, z, z_c, dt_raw, dt_raw_c, conv_w, conv_b,
                                   dt_bias, a_log, d_skip, ssm_norm_w)
    out = jnp.concatenate([attn, ssm], axis=-1) @ w_out
    if not with_ctx_out:
        return out, None
    attn_c = rms_norm(context_attention(q_c, k_c, v_c, attn_sinks), attn_norm_w)
    out_c = jnp.concatenate([attn_c, ssm_c], axis=-1) @ w_out
    return out, out_c


def clamped_swiglu(gu):
    glu, lin = jnp.split(gu, 2, axis=-1)
    glu = jnp.minimum(glu, SWIGLU_LIMIT)
    lin = jnp.clip(lin, -SWIGLU_LIMIT, SWIGLU_LIMIT)
    return glu * jax.nn.sigmoid(SWIGLU_ALPHA * glu) * (lin + 1.0)


def moe_ffn(h, w_router, b_router, w_gate_up, b_gate_up, w_down, b_down):
    n, d = h.shape
    nk = n * TOP_K
    logits = (h @ w_router + b_router).astype(jnp.float32)
    top_logit, top_idx = lax.top_k(logits, TOP_K)
    gates = jax.nn.softmax(top_logit, axis=-1).astype(h.dtype)
    flat_e = top_idx.reshape(nk).astype(jnp.int32)
    flat_tok = jnp.arange(nk, dtype=jnp.int32) // TOP_K
    order = jnp.argsort(flat_e)
    s_e, s_tok, s_gate = flat_e[order], flat_tok[order], gates.reshape(nk)[order]
    counts = jnp.bincount(flat_e, length=N_EXPERTS).astype(jnp.int32)
    start = jnp.cumsum(counts) - counts
    padded = (counts + MOE_BLOCK - 1) // MOE_BLOCK * MOE_BLOCK
    pad_end = jnp.cumsum(padded)
    pad_start = pad_end - padded
    dest = pad_start[s_e] + jnp.arange(nk, dtype=jnp.int32) - start[s_e]
    n_blocks = -(-nk // MOE_BLOCK) + N_EXPERTS
    buf = jnp.zeros((n_blocks * MOE_BLOCK, d), h.dtype).at[dest].set(h[s_tok])
    block_e = jnp.minimum(
        jnp.searchsorted(pad_end, jnp.arange(n_blocks, dtype=jnp.int32) * MOE_BLOCK, side='right'),
        N_EXPERTS - 1)

    def expert_block(args):
        xb, e = args
        return clamped_swiglu(xb @ w_gate_up[e] + b_gate_up[e]) @ w_down[e] + b_down[e]

    out_buf = lax.map(expert_block, (buf.reshape(n_blocks, MOE_BLOCK, d), block_e))
    contrib = out_buf.reshape(n_blocks * MOE_BLOCK, d)[dest] * s_gate[:, None]
    return jax.ops.segment_sum(contrib, s_tok, num_segments=n)


def setup_inputs(seed: int = 0) -> dict:
    key = jax.random.key(seed)
    ks = jax.random.split(key, 26)

    def nrm(k, shape, s):
        return jax.random.normal(k, shape, jnp.float32) * s

    dt0 = jnp.exp(jax.random.uniform(ks[11], (DEPTH, N_DIRS, SSM_HEADS), jnp.float32,
                                     math.log(1e-3), math.log(1e-1)))
    return {
        'x': nrm(ks[0], (BATCH, SEQ, D_MODEL), 1.0),
        'c': nrm(ks[1], (BATCH, D_MODEL), 1.0),
        'ctx': nrm(ks[2], (BATCH, CTX_LEN, D_MODEL), 1.0),
        'c_ctx': nrm(ks[3], (D_MODEL,), 1.0),
        'w_ada': nrm(ks[4], (DEPTH, D_MODEL, N_MOD * D_MODEL), 0.5 * D_MODEL ** -0.5),
        'b_ada': nrm(ks[5], (DEPTH, N_MOD * D_MODEL), 0.02),
        'norm_mix_w': 1.0 + nrm(ks[6], (DEPTH, D_MODEL), 0.1),
        'norm_ffn_w': 1.0 + nrm(ks[7], (DEPTH, D_MODEL), 0.1),
        'w_in': nrm(ks[8], (DEPTH, D_MODEL, IN_WIDTH), D_MODEL ** -0.5),
        'conv_w': nrm(ks[9], (DEPTH, CONV_W, CONV_CH), CONV_W ** -0.5),
        'conv_b': nrm(ks[10], (DEPTH, CONV_CH), 0.02),
        'dt_bias': dt0 + jnp.log(-jnp.expm1(-dt0)),
        'a_log': jnp.log(jax.random.uniform(ks[12], (DEPTH, N_DIRS, SSM_HEADS), jnp.float32, 1.0, 16.0)),
        'd_skip': 1.0 + nrm(ks[13], (DEPTH, SSM_HEADS), 0.1),
        'ssm_norm_w': 1.0 + nrm(ks[14], (DEPTH, SSM_INNER), 0.1),
        'attn_sinks': nrm(ks[15], (DEPTH, N_HEADS), 0.5),
        'attn_norm_w': 1.0 + nrm(ks[16], (DEPTH, ATTN_WIDTH), 0.1),
        'w_out': nrm(ks[17], (DEPTH, MIX_WIDTH, D_MODEL), MIX_WIDTH ** -0.5),
        'w_router': nrm(ks[18], (DEPTH, D_MODEL, N_EXPERTS), D_MODEL ** -0.5),
        'b_router': nrm(ks[19], (DEPTH, N_EXPERTS), 0.01),
        'w_gate_up': nrm(ks[20], (DEPTH, N_EXPERTS, D_MODEL, 2 * D_FF), D_MODEL ** -0.5),
        'b_gate_up': nrm(ks[21], (DEPTH, N_EXPERTS, 2 * D_FF), 0.01),
        'w_down': nrm(ks[22], (DEPTH, N_EXPERTS, D_FF, D_MODEL), D_FF ** -0.5),
        'b_down': nrm(ks[23], (DEPTH, N_EXPERTS, D_MODEL), 0.01),
        'final_norm_w': 1.0 + nrm(ks[24], (D_MODEL,), 0.1),
    }


def reference(x, c, ctx, c_ctx, w_ada, b_ada, norm_mix_w, norm_ffn_w, w_in, conv_w, conv_b,
              dt_bias, a_log, d_skip, ssm_norm_w, attn_sinks, attn_norm_w, w_out,
              w_router, b_router, w_gate_up, b_gate_up, w_down, b_down, final_norm_w):
    b, S, D = x.shape
    n_ctx = ctx.shape[1]
    rows = S // GRID_W
    row_pos = jnp.repeat(jnp.arange(rows, dtype=jnp.int32), GRID_W)
    col_pos = jnp.tile(jnp.arange(GRID_W, dtype=jnp.int32), rows)
    cos, sin = axial_rope_tables(row_pos, col_pos)
    xc = ctx
    for l in range(DEPTH):
        last = l == DEPTH - 1
        mod = jax.nn.silu(c) @ w_ada[l] + b_ada[l]
        mod_c = jax.nn.silu(c_ctx) @ w_ada[l] + b_ada[l]
        sh1, sc1, g1, sh2, sc2, g2 = jnp.split(mod[:, None, :], N_MOD, axis=-1)
        sh1c, sc1c, g1c, sh2c, sc2c, g2c = jnp.split(mod_c, N_MOD, axis=-1)
        h = modulate(rms_norm(x, norm_mix_w[l]), sh1, sc1)
        hc = modulate(rms_norm(xc, norm_mix_w[l]), sh1c, sc1c)
        mix, mix_c = hybrid_mixer(h, hc, cos, sin, w_in[l], conv_w[l], conv_b[l], dt_bias[l], a_log[l],
                                  d_skip[l], ssm_norm_w[l], attn_sinks[l], attn_norm_w[l], w_out[l],
                                  not last)
        x = x + g1 * mix
        if not last:
            xc = xc + g1c * mix_c
            h2 = modulate(rms_norm(x, norm_ffn_w[l]), sh2, sc2)
            h2c = modulate(rms_norm(xc, norm_ffn_w[l]), sh2c, sc2c)
            tokens = jnp.concatenate([h2.reshape(b * S, D), h2c.reshape(b * n_ctx, D)], axis=0)
            f = moe_ffn(tokens, w_router[l], b_router[l], w_gate_up[l], b_gate_up[l], w_down[l], b_down[l])
            x = x + g2 * f[:b * S].reshape(b, S, D)
            xc = xc + g2c * f[b * S:].reshape(b, n_ctx, D)
        else:
            h2 = modulate(rms_norm(x, norm_ffn_w[l]), sh2, sc2)
            f = moe_ffn(h2.reshape(b * S, D), w_router[l], b_router[l], w_gate_up[l], b_gate_up[l],
                        w_down[l], b_down[l])
            x = x + g2 * f.reshape(b, S, D)
    return rms_norm(x, final_norm_w)
```

```python
import functools
import math

import jax
import jax.numpy as jnp
from jax import lax
from jax.experimental import pallas as pl
from jax.experimental.pallas import tpu as pltpu

F32 = jnp.float32
BF16 = jnp.bfloat16
I32 = jnp.int32

D_MODEL = 1024
GRID_W = 64
N_MOD = 6
EPS = 1e-6
NEG_INF = -1e30

HEAD_DIM = 64
ATTN_WIDTH = 512
N_HEADS = 8
N_KV_HEADS = 2
KV_WIDTH = 128
ATT_BLOCK = 128
ROPE_BASE = 10000.0

SSM_INNER = 512
SSM_HEADS = 8
SSM_GROUPS = 2
SSM_STATE = 64
CONV_W = 5
CONV_CH = 768
CHUNK = 128
N_DIRS = 2

N_EXPERTS = 32
TOP_K = 4
D_FF = 1024
SWIGLU_LIMIT = 7.0
SWIGLU_ALPHA = 1.702

LANES = 128
SUBLANES = 8
IN_PAD = 2176
VMEM_LIMIT = 56 * 1024 * 1024
HIGHEST = lax.Precision.HIGHEST


def _params(n_axes, vmem=VMEM_LIMIT):
    return pltpu.CompilerParams(dimension_semantics=("arbitrary",) * n_axes, vmem_limit_bytes=vmem)


def _pick(n, prefs):
    for t in prefs:
        if n % t == 0:
            return t
    raise ValueError(f"no tile in {prefs} divides {n}")


def _sigmoid(x):
    return 1.0 / (1.0 + jnp.exp(-x))


def _dot3(x, m_bf16, left=False):
    hi = x.astype(BF16)
    r1 = x - hi.astype(F32)
    mid = r1.astype(BF16)
    lo = (r1 - mid.astype(F32)).astype(BF16)
    mm = (lambda p: jnp.dot(m_bf16, p, preferred_element_type=F32)) if left else (
        lambda p: jnp.dot(p, m_bf16, preferred_element_type=F32))
    return mm(hi) + mm(mid) + mm(lo)


def _mod_kernel(c_ref, w_ref, b_ref, o_ref):
    c = c_ref[...]
    s = c * _sigmoid(c)
    o_ref[...] = jnp.dot(s, w_ref[...], preferred_element_type=F32, precision=HIGHEST) + b_ref[...]


def _modulation(c_all, w_ada, b_ada):
    depth, d, n = w_ada.shape
    r = c_all.shape[0]
    tn = _pick(n, (1536, 1024, 512, 128))
    out = pl.pallas_call(
        _mod_kernel,
        out_shape=jax.ShapeDtypeStruct((depth, r, n), F32),
        grid=(depth, n // tn),
        in_specs=[pl.BlockSpec((r, d), lambda l, j: (0, 0)),
                  pl.BlockSpec((None, d, tn), lambda l, j: (l, 0, j)),
                  pl.BlockSpec((None, 1, tn), lambda l, j: (l, 0, j))],
        out_specs=pl.BlockSpec((None, r, tn), lambda l, j: (l, 0, j)),
        compiler_params=_params(2),
        name="adaln_mod",
    )(c_all, w_ada, b_ada.reshape(depth, 1, n))
    return out.reshape(depth, r, N_MOD, d)


def _rms_mod(x, nw, shift, scale):
    ms = jnp.mean(x * x, axis=-1, keepdims=True)
    y = x * lax.rsqrt(ms + EPS) * nw
    return y * (1.0 + scale) + shift


def _in_proj_kernel(x_ref, nw_ref, mod_ref, w_ref, cos_ref, sin_ref,
                    q_ref, k_ref, v_ref, z_ref, xbc_ref, dt_ref, *, n_lat_tiles):
    i = pl.program_id(0)
    h = _rms_mod(x_ref[...], nw_ref[...], mod_ref[0, 0:1, :], mod_ref[0, 1:2, :])
    p = jnp.dot(h.astype(BF16), w_ref[...], preferred_element_type=F32)
    tm = p.shape[0]
    is_lat = i < n_lat_tiles
    cos = jnp.where(is_lat, cos_ref[...], 1.0)
    sin = jnp.where(is_lat, sin_ref[...], 0.0)
    lane = lax.broadcasted_iota(I32, (tm, LANES), 1)
    first_half = (lane & 31) < 16

    def rope(t):
        partner = jnp.where(first_half, pltpu.roll(t, LANES - 16, 1), pltpu.roll(t, 16, 1))
        return t * cos + partner * sin

    for j in range(ATTN_WIDTH // LANES):
        q_ref[:, j * LANES:(j + 1) * LANES] = rope(p[:, j * LANES:(j + 1) * LANES]).astype(BF16)
    k_ref[...] = rope(p[:, 512:640]).astype(BF16)
    v_ref[...] = p[:, 640:768].astype(BF16)
    z_ref[...] = p[:, 768:1280]
    xbc_ref[...] = p[:, 1280:2048]
    dt_ref[...] = p[:, 2048:2176]


def _in_proj(xt, nw, mod_l, w_in_p, cos_t, sin_t, *, n_lat, seq, batch):
    nt, d = xt.shape
    tm = _pick(math.gcd(seq, nt - n_lat), (512, 256, 128))
    n_pos_tiles = seq // tm
    kern = functools.partial(_in_proj_kernel, n_lat_tiles=n_lat // tm)
    row = lambda i: (i, 0)
    return pl.pallas_call(
        kern,
        out_shape=(jax.ShapeDtypeStruct((nt, ATTN_WIDTH), BF16),
                   jax.ShapeDtypeStruct((nt, KV_WIDTH), BF16),
                   jax.ShapeDtypeStruct((nt, KV_WIDTH), BF16),
                   jax.ShapeDtypeStruct((nt, SSM_INNER), F32),
                   jax.ShapeDtypeStruct((nt, CONV_CH), F32),
                   jax.ShapeDtypeStruct((nt, LANES), F32)),
        grid=(nt // tm,),
        in_specs=[pl.BlockSpec((tm, d), row),
                  pl.BlockSpec((1, d), lambda i: (0, 0)),
                  pl.BlockSpec((1, N_MOD, d), lambda i: (jnp.minimum(i * tm // seq, batch), 0, 0)),
                  pl.BlockSpec((d, IN_PAD), lambda i: (0, 0)),
                  pl.BlockSpec((tm, LANES), lambda i: (i % n_pos_tiles, 0)),
                  pl.BlockSpec((tm, LANES), lambda i: (i % n_pos_tiles, 0))],
        out_specs=(pl.BlockSpec((tm, ATTN_WIDTH), row), pl.BlockSpec((tm, KV_WIDTH), row),
                   pl.BlockSpec((tm, KV_WIDTH), row), pl.BlockSpec((tm, SSM_INNER), row),
                   pl.BlockSpec((tm, CONV_CH), row), pl.BlockSpec((tm, LANES), row)),
        compiler_params=_params(1),
        name="in_proj",
    )(xt, nw, mod_l, w_in_p, cos_t, sin_t)


def _attn_kernel(sink_ref, q_ref, kp_ref, kc_ref, kn_ref, vp_ref, vc_ref, vn_ref, kx_ref, vx_ref,
                 nw_ref, o_ref, acc_ref, *, nb, n_ctx):
    n = pl.program_id(1)
    is_lat = n < nb
    blk = ATT_BLOCK
    rep = N_HEADS // N_KV_HEADS
    nkeys = n_ctx + 3 * blk
    rows = rep * blk
    r = lax.broadcasted_iota(I32, (rows, nkeys), 0) & (blk - 1)
    c = lax.broadcasted_iota(I32, (rows, nkeys), 1)
    cw = c - n_ctx
    prev_ok = (cw >= r) & (cw < blk) & (n > 0)
    cur_ok = (cw >= blk) & (cw < 2 * blk)
    next_ok = (cw >= 2 * blk) & (cw - 2 * blk <= r) & (n < nb - 1)
    valid = (c < n_ctx) | (is_lat & (prev_ok | cur_ok | next_ok))
    q = q_ref[...]
    for g in range(N_KV_HEADS):
        sl = slice(g * HEAD_DIM, (g + 1) * HEAD_DIM)
        kg = jnp.concatenate([kx_ref[:, sl], kp_ref[:, sl], kc_ref[:, sl], kn_ref[:, sl]], axis=0)
        vg = jnp.concatenate([vx_ref[:, sl], vp_ref[:, sl], vc_ref[:, sl], vn_ref[:, sl]], axis=0)
        qg = jnp.concatenate([q[:, (g * rep + j) * HEAD_DIM:(g * rep + j + 1) * HEAD_DIM]
                              for j in range(rep)], axis=0)
        s = lax.dot_general(qg, kg, (((1,), (1,)), ((), ())), preferred_element_type=F32)
        s = jnp.where(valid, s * (HEAD_DIM ** -0.5), NEG_INF)
        sink = jnp.concatenate([jnp.full((blk, 1), sink_ref[g * rep + j], F32) for j in range(rep)], axis=0)
        m = jnp.maximum(jnp.max(s, axis=-1, keepdims=True), sink)
        e = jnp.exp(s - m)
        denom = jnp.sum(e, axis=-1, keepdims=True) + jnp.exp(sink - m)
        og = jnp.dot(e.astype(BF16), vg, preferred_element_type=F32) / denom
        for j in range(rep):
            hh = g * rep + j
            acc_ref[:, hh * HEAD_DIM:(hh + 1) * HEAD_DIM] = og[j * blk:(j + 1) * blk, :]
    a = acc_ref[...]
    ms = jnp.mean(a * a, axis=-1, keepdims=True)
    o_ref[...] = (a * lax.rsqrt(ms + EPS) * nw_ref[...]).astype(BF16)


def _attention(q, k, v, sinks, nw, *, batch, seq, n_ctx, with_ctx_queries):
    nt = q.shape[0]
    blk = ATT_BLOCK
    nb = seq // blk
    ncq = n_ctx // blk
    nq = nb + (ncq if with_ctx_queries else 0)
    ctx_blk0 = (batch * seq) // n_ctx

    def qmap(b, n, s):
        return (jnp.where(n < nb, b * nb + n, batch * nb + b * ncq + (n - nb)), 0)

    def kmap(off):
        return lambda b, n, s: (b * nb + jnp.clip(n + off, 0, nb - 1), 0)

    xmap = lambda b, n, s: (ctx_blk0 + b, 0)
    kern = functools.partial(_attn_kernel, nb=nb, n_ctx=n_ctx)
    kv_spec = lambda off: pl.BlockSpec((blk, KV_WIDTH), kmap(off))
    return pl.pallas_call(
        kern,
        out_shape=jax.ShapeDtypeStruct((nt, ATTN_WIDTH), BF16),
        grid_spec=pltpu.PrefetchScalarGridSpec(
            num_scalar_prefetch=1,
            grid=(batch, nq),
            in_specs=[pl.BlockSpec((blk, ATTN_WIDTH), qmap),
                      kv_spec(-1), kv_spec(0), kv_spec(1),
                      kv_spec(-1), kv_spec(0), kv_spec(1),
                      pl.BlockSpec((n_ctx, KV_WIDTH), xmap),
                      pl.BlockSpec((n_ctx, KV_WIDTH), xmap),
                      pl.BlockSpec((1, ATTN_WIDTH), lambda b, n, s: (0, 0))],
            out_specs=pl.BlockSpec((blk, ATTN_WIDTH), qmap),
            scratch_shapes=[pltpu.VMEM((blk, ATTN_WIDTH), F32)]),
        compiler_params=_params(2),
        name="attention",
    )(sinks, q, k, k, k, v, v, v, k, v, nw)


def _conv_kernel(xp_ref, xc_ref, xn_ref, w_ref, b_ref, o_ref, ext_ref, *, seq, n_lat, n_ctx):
    i = pl.program_id(0)
    tb = xc_ref.shape[0]
    row0 = i * tb
    in_lat = row0 < n_lat
    local = jnp.where(in_lat, row0 % seq, (row0 - n_lat) % n_ctx)
    length = jnp.where(in_lat, seq, n_ctx)
    first = local == 0
    last = local + tb == length
    h = SUBLANES
    ext_ref[0:h, :] = jnp.where(first, 0.0, xp_ref[...])
    ext_ref[h:h + tb, :] = xc_ref[...]
    ext_ref[h + tb:h + tb + h, :] = jnp.where(last, 0.0, xn_ref[...])
    acc = jnp.zeros((tb, CONV_CH), F32) + b_ref[...]
    for kk in range(CONV_W):
        off = h - CONV_W // 2 + kk
        acc = acc + ext_ref[off:off + tb, :] * w_ref[kk:kk + 1, :]
    o_ref[...] = acc * _sigmoid(acc)


def _conv_silu(xbc, conv_w, conv_b, *, seq, n_lat, n_ctx):
    nt = xbc.shape[0]
    tb = _pick(math.gcd(seq, n_ctx), (256, 128))
    h = SUBLANES
    per = tb // h
    n_h = nt // h
    kern = functools.partial(_conv_kernel, seq=seq, n_lat=n_lat, n_ctx=n_ctx)
    return pl.pallas_call(
        kern,
        out_shape=jax.ShapeDtypeStruct((nt, CONV_CH), F32),
        grid=(nt // tb,),
        in_specs=[pl.BlockSpec((h, CONV_CH), lambda i: (jnp.maximum(i * per - 1, 0), 0)),
                  pl.BlockSpec((tb, CONV_CH), lambda i: (i, 0)),
                  pl.BlockSpec((h, CONV_CH), lambda i: (jnp.minimum((i + 1) * per, n_h - 1), 0)),
                  pl.BlockSpec((CONV_W, CONV_CH), lambda i: (0, 0)),
                  pl.BlockSpec((1, CONV_CH), lambda i: (0, 0))],
        out_specs=pl.BlockSpec((tb, CONV_CH), lambda i: (i, 0)),
        scratch_shapes=[pltpu.VMEM((tb + 2 * h, CONV_CH), F32)],
        compiler_params=_params(1),
        name="conv_silu",
    )(xbc, xbc, xbc, conv_w, conv_b.reshape(1, CONV_CH))


def _ssd_chunk(u, dtraw, dtb, alog, state_ref, *, direction):
    q = CHUNK
    hp = SSM_INNER // SSM_HEADS
    per_g = SSM_HEADS // SSM_GROUPS
    gw = per_g * hp
    xs = u[:, :SSM_INNER]
    bm = u[:, SSM_INNER:SSM_INNER + SSM_GROUPS * SSM_STATE]
    cm = u[:, SSM_INNER + SSM_GROUPS * SSM_STATE:]

    xv = dtraw + dtb
    dt = jnp.maximum(xv, 0.0) + jnp.log1p(jnp.exp(-jnp.abs(xv)))
    dta = dt * (-jnp.exp(alog))

    ri = lax.broadcasted_iota(I32, (q, q), 0)
    ci = lax.broadcasted_iota(I32, (q, q), 1)
    tri = (ci <= ri) if direction == 0 else (ci >= ri)
    cs = _dot3(dta, tri.astype(BF16), left=True)
    cs_t = cs.T

    er = lax.broadcasted_iota(I32, (LANES, SSM_INNER), 0)
    ec = lax.broadcasted_iota(I32, (LANES, SSM_INNER), 1)
    expand = (er == direction * SSM_HEADS + jnp.right_shift(ec, hp.bit_length() - 1)).astype(BF16)
    cs_e = _dot3(cs, expand)
    dt_e = _dot3(dt, expand)
    last = q - 1 if direction == 0 else 0
    cs_last = cs_e[last:last + 1, :]

    xdt = (xs * dt_e).astype(BF16)
    xw = (xs * (jnp.exp(cs_last - cs_e) * dt_e)).astype(BF16)
    bm_t = bm.T.astype(BF16)
    cmb = cm.astype(BF16)
    bmb = bm.astype(BF16)
    state = state_ref[...]
    state_b = state.astype(BF16)

    y_diag = []
    y_off = []
    new_states = []
    for g in range(SSM_GROUPS):
        gs = slice(g * SSM_STATE, (g + 1) * SSM_STATE)
        cb = lax.dot_general(cmb[:, gs], bmb[:, gs], (((1,), (1,)), ((), ())), preferred_element_type=F32)
        y_off.append(jnp.dot(cmb[:, gs], state_b[:, g * gw:(g + 1) * gw], preferred_element_type=F32))
        new_states.append(jnp.dot(bm_t[gs, :], xw[:, g * gw:(g + 1) * gw], preferred_element_type=F32))
        for j in range(per_g):
            hh = g * per_g + j
            col = direction * SSM_HEADS + hh
            seg = cs[:, col:col + 1] - cs_t[col:col + 1, :]
            decay = jnp.exp(jnp.where(tri, seg, NEG_INF))
            scores = (cb * decay).astype(BF16)
            y_diag.append(jnp.dot(scores, xdt[:, hh * hp:(hh + 1) * hp], preferred_element_type=F32))
    y = jnp.concatenate(y_diag, axis=1) + jnp.exp(cs_e) * jnp.concatenate(y_off, axis=1)
    state_ref[...] = jnp.exp(cs_last) * state + jnp.concatenate(new_states, axis=1)
    return y


def _ssd_fwd_kernel(u_ref, dt_ref, dtb_ref, alog_ref, skip_ref, y_ref, state_ref):
    @pl.when(pl.program_id(1) == 0)
    def _():
        state_ref[...] = jnp.zeros_like(state_ref)

    u = u_ref[...]
    y = _ssd_chunk(u, dt_ref[...], dtb_ref[...], alog_ref[...], state_ref, direction=0)
    y_ref[...] = y + skip_ref[...] * u[:, :SSM_INNER]


def _ssd_bwd_kernel(u_ref, dt_ref, dtb_ref, alog_ref, y0_ref, z_ref, nw_ref, o_ref, state_ref):
    @pl.when(pl.program_id(1) == 0)
    def _():
        state_ref[...] = jnp.zeros_like(state_ref)

    y = y0_ref[...] + _ssd_chunk(u_ref[...], dt_ref[...], dtb_ref[...], alog_ref[...], state_ref, direction=1)
    z = z_ref[...]
    gt = y * (z * _sigmoid(z))
    gw = SSM_INNER // SSM_GROUPS
    outs = []
    for g in range(SSM_GROUPS):
        gg = gt[:, g * gw:(g + 1) * gw]
        ms = jnp.mean(gg * gg, axis=-1, keepdims=True)
        outs.append(gg * lax.rsqrt(ms + EPS))
    o_ref[...] = (jnp.concatenate(outs, axis=1) * nw_ref[...]).astype(BF16)


def _ssd(u, dt_raw, z, dtb, alog, skip, ssm_nw, *, batch, seq, n_ctx):
    nt = u.shape[0]
    q = CHUNK
    ncl = seq // q
    ncc = n_ctx // q
    steps = ncc + ncl
    lat0 = 0
    ctx0 = (batch * seq) // q

    def fmap(b, t):
        return (jnp.where(t < ncc, ctx0 + b * ncc + t, lat0 + b * ncl + (t - ncc)), 0)

    def rmap(b, t):
        return (jnp.where(t < ncc, ctx0 + b * ncc + (ncc - 1 - t), lat0 + b * ncl + (ncl - 1 - (t - ncc))), 0)

    const = lambda b, t: (0, 0)
    state = pltpu.VMEM((SSM_STATE, SSM_INNER), F32)
    y0 = pl.pallas_call(
        _ssd_fwd_kernel,
        out_shape=jax.ShapeDtypeStruct((nt, SSM_INNER), F32),
        grid=(batch, steps),
        in_specs=[pl.BlockSpec((q, CONV_CH), fmap), pl.BlockSpec((q, LANES), fmap),
                  pl.BlockSpec((1, LANES), const), pl.BlockSpec((1, LANES), const),
                  pl.BlockSpec((1, SSM_INNER), const)],
        out_specs=pl.BlockSpec((q, SSM_INNER), fmap),
        scratch_shapes=[state],
        compiler_params=_params(2),
        name="ssd_forward",
    )(u, dt_raw, dtb, alog, skip)
    return pl.pallas_call(
        _ssd_bwd_kernel,
        out_shape=jax.ShapeDtypeStruct((nt, SSM_INNER), BF16),
        grid=(batch, steps),
        in_specs=[pl.BlockSpec((q, CONV_CH), rmap), pl.BlockSpec((q, LANES), rmap),
                  pl.BlockSpec((1, LANES), const), pl.BlockSpec((1, LANES), const),
                  pl.BlockSpec((q, SSM_INNER), rmap), pl.BlockSpec((q, SSM_INNER), rmap),
                  pl.BlockSpec((1, SSM_INNER), const)],
        out_specs=pl.BlockSpec((q, SSM_INNER), rmap),
        scratch_shapes=[state],
        compiler_params=_params(2),
        name="ssd_backward",
    )(u, dt_raw, dtb, alog, y0, z, ssm_nw)


def _out_proj_kernel(x_ref, a_ref, s_ref, wa_ref, ws_ref, mod_ref, nw_ref, wr_ref, br_ref,
                     xo_ref, h_ref, idx_ref, gate_ref):
    mix = (jnp.dot(a_ref[...], wa_ref[...], preferred_element_type=F32)
           + jnp.dot(s_ref[...], ws_ref[...], preferred_element_type=F32))
    x = x_ref[...] + mod_ref[0, 2:3, :] * mix
    xo_ref[...] = x
    h = _rms_mod(x, nw_ref[...], mod_ref[0, 3:4, :], mod_ref[0, 4:5, :])
    h_ref[...] = h
    logits = jnp.dot(h, wr_ref[...], preferred_element_type=F32, precision=HIGHEST) + br_ref[...]
    tm = logits.shape[0]
    lane = lax.broadcasted_iota(I32, (tm, LANES), 1)
    lane_f = lane.astype(F32)
    work = jnp.where(lane < N_EXPERTS, logits, -jnp.inf)
    idx_out = jnp.zeros((tm, LANES), F32)
    val_out = jnp.full((tm, LANES), -jnp.inf, F32)
    for kk in range(TOP_K):
        m = jnp.max(work, axis=-1, keepdims=True)
        sel = jnp.min(jnp.where(work == m, lane_f, float(LANES)), axis=-1, keepdims=True)
        idx_out = jnp.where(lane == kk, sel, idx_out)
        val_out = jnp.where(lane == kk, m, val_out)
        work = jnp.where(lane_f == sel, -jnp.inf, work)
    top = jnp.max(val_out, axis=-1, keepdims=True)
    e = jnp.exp(val_out - top)
    idx_ref[...] = idx_out.astype(I32)
    gate_ref[...] = e / jnp.sum(e, axis=-1, keepdims=True)


def _out_proj(xt, attn, ssm, w_out_a, w_out_s, mod_l, nfw, wr_p, br_p, *, seq, batch, n_lat):
    nt, d = xt.shape
    tm = _pick(math.gcd(seq, nt - n_lat), (512, 256, 128))
    row = lambda i: (i, 0)
    const = lambda i: (0, 0)
    return pl.pallas_call(
        _out_proj_kernel,
        out_shape=(jax.ShapeDtypeStruct((nt, d), F32), jax.ShapeDtypeStruct((nt, d), F32),
                   jax.ShapeDtypeStruct((nt, LANES), I32), jax.ShapeDtypeStruct((nt, LANES), F32)),
        grid=(nt // tm,),
        in_specs=[pl.BlockSpec((tm, d), row), pl.BlockSpec((tm, ATTN_WIDTH), row),
                  pl.BlockSpec((tm, SSM_INNER), row),
                  pl.BlockSpec((ATTN_WIDTH, d), const), pl.BlockSpec((SSM_INNER, d), const),
                  pl.BlockSpec((1, N_MOD, d), lambda i: (jnp.minimum(i * tm // seq, batch), 0, 0)),
                  pl.BlockSpec((1, d), const), pl.BlockSpec((d, LANES), const), pl.BlockSpec((1, LANES), const)],
        out_specs=(pl.BlockSpec((tm, d), row), pl.BlockSpec((tm, d), row),
                   pl.BlockSpec((tm, LANES), row), pl.BlockSpec((tm, LANES), row)),
        compiler_params=_params(1),
        name="out_proj_router",
    )(xt, attn, ssm, w_out_a, w_out_s, mod_l, nfw, wr_p, br_p)


def _gather_kernel(idx_ref, tab_ref, o_ref, sem):
    g = o_ref.shape[0]

    def row_copy(r, src_row):
        return pltpu.make_async_copy(tab_ref.at[pl.ds(src_row, 1), :], o_ref.at[pl.ds(r, 1), :], sem)

    def issue(r, carry):
        row_copy(r, idx_ref[0, r]).start()
        return carry

    def drain(r, carry):
        row_copy(r, 0).wait()
        return carry

    lax.fori_loop(0, g, issue, 0)
    lax.fori_loop(0, g, drain, 0)


def _gather_rows(table, idx, *, rows_per_step):
    n = idx.shape[0]
    g = rows_per_step
    d = table.shape[1]
    return pl.pallas_call(
        _gather_kernel,
        out_shape=jax.ShapeDtypeStruct((n, d), table.dtype),
        grid=(n // g,),
        in_specs=[pl.BlockSpec((None, 1, g), lambda i: (i, 0, 0), memory_space=pltpu.SMEM),
                  pl.BlockSpec(memory_space=pl.ANY)],
        out_specs=pl.BlockSpec((g, d), lambda i: (i, 0)),
        scratch_shapes=[pltpu.SemaphoreType.DMA],
        compiler_params=_params(1),
        name="gather_rows",
    )(idx.reshape(n // g, 1, g), table)


def _expert_kernel(be_ref, nu_ref, x_ref, wgu_ref, bgu_ref, wd_ref, bd_ref, o_ref):
    i = pl.program_id(0)

    @pl.when(i < nu_ref[0])
    def _():
        gu = jnp.dot(x_ref[...].astype(BF16), wgu_ref[...], preferred_element_type=F32) + bgu_ref[...]
        glu = jnp.minimum(gu[:, :D_FF], SWIGLU_LIMIT)
        lin = jnp.clip(gu[:, D_FF:], -SWIGLU_LIMIT, SWIGLU_LIMIT)
        act = glu * _sigmoid(SWIGLU_ALPHA * glu) * (lin + 1.0)
        o_ref[...] = jnp.dot(act.astype(BF16), wd_ref[...], preferred_element_type=F32) + bd_ref[...]

    @pl.when(i >= nu_ref[0])
    def _():
        o_ref[...] = jnp.zeros_like(o_ref)


def _experts(buf, block_e, n_used, wgu, bgu, wd, bd, *, tme):
    n, d = buf.shape
    nblk = n // tme
    xmap = lambda i, be, nu: (jnp.minimum(i, nu[0] - 1), 0)
    emap3 = lambda i, be, nu: (be[i], 0, 0)
    return pl.pallas_call(
        _expert_kernel,
        out_shape=jax.ShapeDtypeStruct((n, d), F32),
        grid_spec=pltpu.PrefetchScalarGridSpec(
            num_scalar_prefetch=2,
            grid=(nblk,),
            in_specs=[pl.BlockSpec((tme, d), xmap),
                      pl.BlockSpec((None, d, 2 * D_FF), emap3),
                      pl.BlockSpec((None, 1, 2 * D_FF), emap3),
                      pl.BlockSpec((None, D_FF, d), emap3),
                      pl.BlockSpec((None, 1, d), emap3)],
            out_specs=pl.BlockSpec((tme, d), lambda i, be, nu: (i, 0))),
        compiler_params=_params(1),
        name="expert_ffn",
    )(block_e, n_used, buf, wgu, bgu, wd, bd)


def _combine_kernel(x_ref, y_ref, gate_ref, mod_ref, fw_ref, o_ref, *, final_norm):
    d = x_ref.shape[1]
    gates = gate_ref[...]
    f = jnp.zeros(x_ref.shape, F32)
    for kk in range(TOP_K):
        f = f + y_ref[:, kk * d:(kk + 1) * d] * gates[:, kk:kk + 1]
    x = x_ref[...] + mod_ref[0, 5:6, :] * f
    if final_norm:
        ms = jnp.mean(x * x, axis=-1, keepdims=True)
        x = x * lax.rsqrt(ms + EPS) * fw_ref[...]
    o_ref[...] = x


def _combine(xt, gathered, gates, mod_l, fw, *, seq, batch, n_lat, final_norm):
    nt, d = xt.shape
    tm = _pick(math.gcd(seq, nt - n_lat), (256, 128))
    row = lambda i: (i, 0)
    kern = functools.partial(_combine_kernel, final_norm=final_norm)
    return pl.pallas_call(
        kern,
        out_shape=jax.ShapeDtypeStruct((nt, d), F32),
        grid=(nt // tm,),
        in_specs=[pl.BlockSpec((tm, d), row), pl.BlockSpec((tm, TOP_K * d), row),
                  pl.BlockSpec((tm, LANES), row),
                  pl.BlockSpec((1, N_MOD, d), lambda i: (jnp.minimum(i * tm // seq, batch), 0, 0)),
                  pl.BlockSpec((1, d), lambda i: (0, 0))],
        out_specs=pl.BlockSpec((tm, d), row),
        compiler_params=_params(1),
        name="moe_combine",
    )(xt, gathered, gates, mod_l, fw)


def _route_meta(top_idx, *, tme, n_blocks):
    nt = top_idx.shape[0]
    nk = nt * TOP_K
    flat_e = top_idx.reshape(nk)
    onehot = (flat_e[:, None] == jnp.arange(N_EXPERTS, dtype=I32)[None, :]).astype(I32)
    csum = jnp.cumsum(onehot, axis=0)
    rank = jnp.take_along_axis(csum, flat_e[:, None], axis=1)[:, 0] - 1
    counts = csum[-1]
    padded = (counts + tme - 1) // tme * tme
    pad_end = jnp.cumsum(padded)
    pad_start = pad_end - padded
    dest = pad_start[flat_e] + rank
    n_used = (pad_end[-1] // tme).astype(I32).reshape(1)
    block_e = jnp.minimum(
        jnp.searchsorted(pad_end, jnp.arange(n_blocks, dtype=I32) * tme, side='right'),
        N_EXPERTS - 1).astype(I32)
    src = jnp.zeros((n_blocks * tme,), I32).at[dest].set(jnp.arange(nk, dtype=I32) // TOP_K)
    return dest.astype(I32), src, block_e, n_used


def _rope_tables(seq):
    rows = seq // GRID_W
    row_pos = jnp.repeat(jnp.arange(rows, dtype=I32), GRID_W).astype(F32)
    col_pos = jnp.tile(jnp.arange(GRID_W, dtype=I32), rows).astype(F32)
    n_freq = HEAD_DIM // 4
    inv_freq = ROPE_BASE ** (-jnp.arange(n_freq, dtype=F32) / n_freq)
    lane = jnp.arange(LANES)
    f = lane % n_freq
    use_col = (lane % HEAD_DIM) >= HEAD_DIM // 2
    ang = jnp.where(use_col[None, :], col_pos[:, None], row_pos[:, None]) * inv_freq[f][None, :]
    first_half = (lane % 32) < 16
    return jnp.cos(ang), jnp.where(first_half[None, :], -jnp.sin(ang), jnp.sin(ang))


def kernel(x, c, ctx, c_ctx, w_ada, b_ada, norm_mix_w, norm_ffn_w, w_in, conv_w, conv_b, dt_bias, a_log, d_skip, ssm_norm_w, attn_sinks, attn_norm_w, w_out, w_router, b_router, w_gate_up, b_gate_up, w_down, b_down, final_norm_w):
    batch, seq, d = x.shape
    n_ctx = ctx.shape[1]
    depth = w_ada.shape[0]
    n_lat = batch * seq
    nt = n_lat + batch * n_ctx
    assert d == D_MODEL and seq % ATT_BLOCK == 0 and n_ctx % ATT_BLOCK == 0 and n_lat % n_ctx == 0

    r_mod = -(-(batch + 1) // SUBLANES) * SUBLANES
    c_all = jnp.zeros((r_mod, d), F32).at[:batch].set(c).at[batch].set(c_ctx)
    mod = _modulation(c_all, w_ada, b_ada)
    cos_t, sin_t = _rope_tables(seq)

    xt = jnp.concatenate([x.reshape(n_lat, d), ctx.reshape(batch * n_ctx, d)], axis=0)
    tme = 512
    n_blocks = -(-(nt * TOP_K) // tme) + N_EXPERTS
    hp = SSM_INNER // SSM_HEADS

    for l in range(depth):
        last = l == depth - 1
        w_in_p = jnp.pad(w_in[l], ((0, 0), (0, IN_PAD - w_in.shape[2]))).astype(BF16)
        q, k, v, z, xbc, dt_raw = _in_proj(xt, norm_mix_w[l].reshape(1, d), mod[l], w_in_p, cos_t, sin_t,
                                           n_lat=n_lat, seq=seq, batch=batch)
        attn = _attention(q, k, v, attn_sinks[l], attn_norm_w[l].reshape(1, ATTN_WIDTH),
                          batch=batch, seq=seq, n_ctx=n_ctx, with_ctx_queries=True)
        u = _conv_silu(xbc, conv_w[l], conv_b[l], seq=seq, n_lat=n_lat, n_ctx=n_ctx)
        pad16 = lambda t: jnp.pad(t.reshape(1, N_DIRS * SSM_HEADS), ((0, 0), (0, LANES - N_DIRS * SSM_HEADS)))
        ssm = _ssd(u, dt_raw, z, pad16(dt_bias[l]), pad16(a_log[l]),
                   jnp.repeat(d_skip[l], hp).reshape(1, SSM_INNER), ssm_norm_w[l].reshape(1, SSM_INNER),
                   batch=batch, seq=seq, n_ctx=n_ctx)
        w_o = w_out[l].astype(BF16)
        wr_p = jnp.pad(w_router[l], ((0, 0), (0, LANES - N_EXPERTS)))
        br_p = jnp.pad(b_router[l].reshape(1, N_EXPERTS), ((0, 0), (0, LANES - N_EXPERTS)))
        xt, h2, top_idx, gates = _out_proj(xt, attn, ssm, w_o[:ATTN_WIDTH], w_o[ATTN_WIDTH:], mod[l],
                                           norm_ffn_w[l].reshape(1, d), wr_p, br_p,
                                           seq=seq, batch=batch, n_lat=n_lat)
        dest, src, block_e, n_used = _route_meta(top_idx[:, :TOP_K], tme=tme, n_blocks=n_blocks)
        buf = _gather_rows(h2, src, rows_per_step=256)
        out_buf = _experts(buf, block_e, n_used, w_gate_up[l].astype(BF16),
                           b_gate_up[l].reshape(N_EXPERTS, 1, 2 * D_FF), w_down[l].astype(BF16),
                           b_down[l].reshape(N_EXPERTS, 1, d), tme=tme)
        gathered = _gather_rows(out_buf, dest, rows_per_step=256).reshape(nt, TOP_K * d)
        xt = _combine(xt, gathered, gates, mod[l], final_norm_w.reshape(1, d),
                      seq=seq, batch=batch, n_lat=n_lat, final_norm=last)
    return xt[:n_lat].reshape(batch, seq, d)
```

```python
import functools
import math

import jax
import jax.numpy as jnp
from jax import lax
from jax.experimental import pallas as pl
from jax.experimental.pallas import tpu as pltpu

F32 = jnp.float32
BF16 = jnp.bfloat16
I32 = jnp.int32

D_MODEL = 1024
GRID_W = 64
N_MOD = 6
EPS = 1e-6
NEG_INF = -1e30

HEAD_DIM = 64
ATTN_WIDTH = 512
N_HEADS = 8
N_KV_HEADS = 2
KV_WIDTH = 128
ATT_BLOCK = 128
ROPE_BASE = 10000.0

SSM_INNER = 512
SSM_HEADS = 8
SSM_GROUPS = 2
SSM_STATE = 64
CONV_W = 5
CONV_CH = 768
CHUNK = 128
N_DIRS = 2

N_EXPERTS = 32
TOP_K = 4
D_FF = 1024
SWIGLU_LIMIT = 7.0
SWIGLU_ALPHA = 1.702

LANES = 128
SUBLANES = 8
IN_PAD = 2176
VMEM_LIMIT = 56 * 1024 * 1024
HIGHEST = lax.Precision.HIGHEST


def _params(n_axes, vmem=VMEM_LIMIT):
    return pltpu.CompilerParams(dimension_semantics=("arbitrary",) * n_axes, vmem_limit_bytes=vmem)


def _pick(n, prefs):
    for t in prefs:
        if n % t == 0:
            return t
    raise ValueError(f"no tile in {prefs} divides {n}")


def _sigmoid(x):
    return 1.0 / (1.0 + jnp.exp(-x))


def _dot3(x, m_bf16, left=False):
    hi = x.astype(BF16)
    r1 = x - hi.astype(F32)
    mid = r1.astype(BF16)
    lo = (r1 - mid.astype(F32)).astype(BF16)
    mm = (lambda p: jnp.dot(m_bf16, p, preferred_element_type=F32)) if left else (
        lambda p: jnp.dot(p, m_bf16, preferred_element_type=F32))
    return mm(hi) + mm(mid) + mm(lo)


def _mod_kernel(c_ref, w_ref, b_ref, o_ref):
    c = c_ref[...]
    s = c * _sigmoid(c)
    o_ref[...] = jnp.dot(s, w_ref[...], preferred_element_type=F32, precision=HIGHEST) + b_ref[...]


def _modulation(c_all, w_ada, b_ada):
    depth, d, n = w_ada.shape
    r = c_all.shape[0]
    tn = _pick(n, (1536, 1024, 512, 128))
    out = pl.pallas_call(
        _mod_kernel,
        out_shape=jax.ShapeDtypeStruct((depth, r, n), F32),
        grid=(depth, n // tn),
        in_specs=[pl.BlockSpec((r, d), lambda l, j: (0, 0)),
                  pl.BlockSpec((None, d, tn), lambda l, j: (l, 0, j)),
                  pl.BlockSpec((None, 1, tn), lambda l, j: (l, 0, j))],
        out_specs=pl.BlockSpec((None, r, tn), lambda l, j: (l, 0, j)),
        compiler_params=_params(2),
        name="adaln_mod",
    )(c_all, w_ada, b_ada.reshape(depth, 1, n))
    return out.reshape(depth, r, N_MOD, d)


def _rms_mod(x, nw, shift, scale):
    ms = jnp.mean(x * x, axis=-1, keepdims=True)
    y = x * lax.rsqrt(ms + EPS) * nw
    return y * (1.0 + scale) + shift


def _in_proj_kernel(x_ref, nw_ref, mod_ref, w_ref, cos_ref, sin_ref,
                    q_ref, k_ref, v_ref, z_ref, xbc_ref, dt_ref, *, n_lat_tiles):
    i = pl.program_id(0)
    h = _rms_mod(x_ref[...], nw_ref[...], mod_ref[0, 0:1, :], mod_ref[0, 1:2, :])
    p = jnp.dot(h.astype(BF16), w_ref[...], preferred_element_type=F32)
    tm = p.shape[0]
    is_lat = i < n_lat_tiles
    cos = jnp.where(is_lat, cos_ref[...], 1.0)
    sin = jnp.where(is_lat, sin_ref[...], 0.0)
    lane = lax.broadcasted_iota(I32, (tm, LANES), 1)
    first_half = (lane & 31) < 16

    def rope(t):
        partner = jnp.where(first_half, pltpu.roll(t, LANES - 16, 1), pltpu.roll(t, 16, 1))
        return t * cos + partner * sin

    for j in range(ATTN_WIDTH // LANES):
        q_ref[:, j * LANES:(j + 1) * LANES] = rope(p[:, j * LANES:(j + 1) * LANES]).astype(BF16)
    k_ref[...] = rope(p[:, 512:640]).astype(BF16)
    v_ref[...] = p[:, 640:768].astype(BF16)
    z_ref[...] = p[:, 768:1280]
    xbc_ref[...] = p[:, 1280:2048]
    dt_ref[...] = p[:, 2048:2176]


def _in_proj(xt, nw, mod_l, w_in_p, cos_t, sin_t, *, n_lat, seq, batch):
    nt, d = xt.shape
    tm = _pick(math.gcd(seq, nt - n_lat), (512, 256, 128))
    n_pos_tiles = seq // tm
    kern = functools.partial(_in_proj_kernel, n_lat_tiles=n_lat // tm)
    row = lambda i: (i, 0)
    return pl.pallas_call(
        kern,
        out_shape=(jax.ShapeDtypeStruct((nt, ATTN_WIDTH), BF16),
                   jax.ShapeDtypeStruct((nt, KV_WIDTH), BF16),
                   jax.ShapeDtypeStruct((nt, KV_WIDTH), BF16),
                   jax.ShapeDtypeStruct((nt, SSM_INNER), F32),
                   jax.ShapeDtypeStruct((nt, CONV_CH), F32),
                   jax.ShapeDtypeStruct((nt, LANES), F32)),
        grid=(nt // tm,),
        in_specs=[pl.BlockSpec((tm, d), row),
                  pl.BlockSpec((1, d), lambda i: (0, 0)),
                  pl.BlockSpec((1, N_MOD, d), lambda i: (jnp.minimum(i * tm // seq, batch), 0, 0)),
                  pl.BlockSpec((d, IN_PAD), lambda i: (0, 0)),
                  pl.BlockSpec((tm, LANES), lambda i: (i % n_pos_tiles, 0)),
                  pl.BlockSpec((tm, LANES), lambda i: (i % n_pos_tiles, 0))],
        out_specs=(pl.BlockSpec((tm, ATTN_WIDTH), row), pl.BlockSpec((tm, KV_WIDTH), row),
                   pl.BlockSpec((tm, KV_WIDTH), row), pl.BlockSpec((tm, SSM_INNER), row),
                   pl.BlockSpec((tm, CONV_CH), row), pl.BlockSpec((tm, LANES), row)),
        compiler_params=_params(1),
        name="in_proj",
    )(xt, nw, mod_l, w_in_p, cos_t, sin_t)


def _attn_kernel(sink_ref, q_ref, kp_ref, kc_ref, kn_ref, vp_ref, vc_ref, vn_ref, kx_ref, vx_ref,
                 nw_ref, o_ref, acc_ref, *, nb, n_ctx):
    n = pl.program_id(1)
    is_lat = n < nb
    blk = ATT_BLOCK
    rep = N_HEADS // N_KV_HEADS
    nkeys = n_ctx + 3 * blk
    rows = rep * blk
    r = lax.broadcasted_iota(I32, (rows, nkeys), 0) & (blk - 1)
    c = lax.broadcasted_iota(I32, (rows, nkeys), 1)
    cw = c - n_ctx
    prev_ok = (cw >= r) & (cw < blk) & (n > 0)
    cur_ok = (cw >= blk) & (cw < 2 * blk)
    next_ok = (cw >= 2 * blk) & (cw - 2 * blk <= r) & (n < nb - 1)
    valid = (c < n_ctx) | (is_lat & (prev_ok | cur_ok | next_ok))
    q = q_ref[...]
    for g in range(N_KV_HEADS):
        sl = slice(g * HEAD_DIM, (g + 1) * HEAD_DIM)
        kg = jnp.concatenate([kx_ref[:, sl], kp_ref[:, sl], kc_ref[:, sl], kn_ref[:, sl]], axis=0)
        vg = jnp.concatenate([vx_ref[:, sl], vp_ref[:, sl], vc_ref[:, sl], vn_ref[:, sl]], axis=0)
        qg = jnp.concatenate([q[:, (g * rep + j) * HEAD_DIM:(g * rep + j + 1) * HEAD_DIM]
                              for j in range(rep)], axis=0)
        s = lax.dot_general(qg, kg, (((1,), (1,)), ((), ())), preferred_element_type=F32)
        s = jnp.where(valid, s * (HEAD_DIM ** -0.5), NEG_INF)
        sink = jnp.concatenate([jnp.full((blk, 1), sink_ref[g * rep + j], F32) for j in range(rep)], axis=0)
        m = jnp.maximum(jnp.max(s, axis=-1, keepdims=True), sink)
        e = jnp.exp(s - m)
        denom = jnp.sum(e, axis=-1, keepdims=True) + jnp.exp(sink - m)
        og = jnp.dot(e.astype(BF16), vg, preferred_element_type=F32) / denom
        for j in range(rep):
            hh = g * rep + j
            acc_ref[:, hh * HEAD_DIM:(hh + 1) * HEAD_DIM] = og[j * blk:(j + 1) * blk, :]
    a = acc_ref[...]
    ms = jnp.mean(a * a, axis=-1, keepdims=True)
    o_ref[...] = (a * lax.rsqrt(ms + EPS) * nw_ref[...]).astype(BF16)


def _attention(q, k, v, sinks, nw, *, batch, seq, n_ctx, with_ctx_queries):
    nt = q.shape[0]
    blk = ATT_BLOCK
    nb = seq // blk
    ncq = n_ctx // blk
    nq = nb + (ncq if with_ctx_queries else 0)
    ctx_blk0 = (batch * seq) // n_ctx

    def qmap(b, n, s):
        return (jnp.where(n < nb, b * nb + n, batch * nb + b * ncq + (n - nb)), 0)

    def kmap(off):
        return lambda b, n, s: (b * nb + jnp.clip(n + off, 0, nb - 1), 0)

    xmap = lambda b, n, s: (ctx_blk0 + b, 0)
    kern = functools.partial(_attn_kernel, nb=nb, n_ctx=n_ctx)
    kv_spec = lambda off: pl.BlockSpec((blk, KV_WIDTH), kmap(off))
    return pl.pallas_call(
        kern,
        out_shape=jax.ShapeDtypeStruct((nt, ATTN_WIDTH), BF16),
        grid_spec=pltpu.PrefetchScalarGridSpec(
            num_scalar_prefetch=1,
            grid=(batch, nq),
            in_specs=[pl.BlockSpec((blk, ATTN_WIDTH), qmap),
                      kv_spec(-1), kv_spec(0), kv_spec(1),
                      kv_spec(-1), kv_spec(0), kv_spec(1),
                      pl.BlockSpec((n_ctx, KV_WIDTH), xmap),
                      pl.BlockSpec((n_ctx, KV_WIDTH), xmap),
                      pl.BlockSpec((1, ATTN_WIDTH), lambda b, n, s: (0, 0))],
            out_specs=pl.BlockSpec((blk, ATTN_WIDTH), qmap),
            scratch_shapes=[pltpu.VMEM((blk, ATTN_WIDTH), F32)]),
        compiler_params=_params(2),
        name="attention",
    )(sinks, q, k, k, k, v, v, v, k, v, nw)


def _conv_kernel(xp_ref, xc_ref, xn_ref, w_ref, b_ref, o_ref, ext_ref, *, seq, n_lat, n_ctx):
    i = pl.program_id(0)
    tb = xc_ref.shape[0]
    row0 = i * tb
    in_lat = row0 < n_lat
    local = jnp.where(in_lat, row0 % seq, (row0 - n_lat) % n_ctx)
    length = jnp.where(in_lat, seq, n_ctx)
    first = local == 0
    last = local + tb == length
    h = SUBLANES
    ext_ref[0:h, :] = jnp.where(first, 0.0, xp_ref[...])
    ext_ref[h:h + tb, :] = xc_ref[...]
    ext_ref[h + tb:h + tb + h, :] = jnp.where(last, 0.0, xn_ref[...])
    acc = jnp.zeros((tb, CONV_CH), F32) + b_ref[...]
    for kk in range(CONV_W):
        off = h - CONV_W // 2 + kk
        acc = acc + ext_ref[off:off + tb, :] * w_ref[kk:kk + 1, :]
    o_ref[...] = acc * _sigmoid(acc)


def _conv_silu(xbc, conv_w, conv_b, *, seq, n_lat, n_ctx):
    nt = xbc.shape[0]
    tb = _pick(math.gcd(seq, n_ctx), (256, 128))
    h = SUBLANES
    per = tb // h
    n_h = nt // h
    kern = functools.partial(_conv_kernel, seq=seq, n_lat=n_lat, n_ctx=n_ctx)
    return pl.pallas_call(
        kern,
        out_shape=jax.ShapeDtypeStruct((nt, CONV_CH), F32),
        grid=(nt // tb,),
        in_specs=[pl.BlockSpec((h, CONV_CH), lambda i: (jnp.maximum(i * per - 1, 0), 0)),
                  pl.BlockSpec((tb, CONV_CH), lambda i: (i, 0)),
                  pl.BlockSpec((h, CONV_CH), lambda i: (jnp.minimum((i + 1) * per, n_h - 1), 0)),
                  pl.BlockSpec((CONV_W, CONV_CH), lambda i: (0, 0)),
                  pl.BlockSpec((1, CONV_CH), lambda i: (0, 0))],
        out_specs=pl.BlockSpec((tb, CONV_CH), lambda i: (i, 0)),
        scratch_shapes=[pltpu.VMEM((tb + 2 * h, CONV_CH), F32)],
        compiler_params=_params(1),
        name="conv_silu",
    )(xbc, xbc, xbc, conv_w, conv_b.reshape(1, CONV_CH))


def _ssd_chunk(u, dtraw, dtb, alog, state_ref, *, direction):
    q = CHUNK
    hp = SSM_INNER // SSM_HEADS
    per_g = SSM_HEADS // SSM_GROUPS
    gw = per_g * hp
    xs = u[:, :SSM_INNER]
    bm = u[:, SSM_INNER:SSM_INNER + SSM_GROUPS * SSM_STATE]
    cm = u[:, SSM_INNER + SSM_GROUPS * SSM_STATE:]

    xv = dtraw + dtb
    dt = jnp.maximum(xv, 0.0) + jnp.log1p(jnp.exp(-jnp.abs(xv)))
    dta = dt * (-jnp.exp(alog))

    ri = lax.broadcasted_iota(I32, (q, q), 0)
    ci = lax.broadcasted_iota(I32, (q, q), 1)
    tri = (ci <= ri) if direction == 0 else (ci >= ri)
    cs = _dot3(dta, tri.astype(BF16), left=True)
    cs_t = cs.T

    er = lax.broadcasted_iota(I32, (LANES, SSM_INNER), 0)
    ec = lax.broadcasted_iota(I32, (LANES, SSM_INNER), 1)
    expand = (er == direction * SSM_HEADS + jnp.right_shift(ec, hp.bit_length() - 1)).astype(BF16)
    cs_e = _dot3(cs, expand)
    dt_e = _dot3(dt, expand)
    last = q - 1 if direction == 0 else 0
    cs_last = cs_e[last:last + 1, :]

    xdt = (xs * dt_e).astype(BF16)
    xw = (xs * (jnp.exp(cs_last - cs_e) * dt_e)).astype(BF16)
    bm_t = bm.T.astype(BF16)
    cmb = cm.astype(BF16)
    bmb = bm.astype(BF16)
    state = state_ref[...]
    state_b = state.astype(BF16)

    y_diag = []
    y_off = []
    new_states = []
    for g in range(SSM_GROUPS):
        gs = slice(g * SSM_STATE, (g + 1) * SSM_STATE)
        cb = lax.dot_general(cmb[:, gs], bmb[:, gs], (((1,), (1,)), ((), ())), preferred_element_type=F32)
        y_off.append(jnp.dot(cmb[:, gs], state_b[:, g * gw:(g + 1) * gw], preferred_element_type=F32))
        new_states.append(jnp.dot(bm_t[gs, :], xw[:, g * gw:(g + 1) * gw], preferred_element_type=F32))
        for j in range(per_g):
            hh = g * per_g + j
            col = direction * SSM_HEADS + hh
            seg = cs[:, col:col + 1] - cs_t[col:col + 1, :]
            decay = jnp.exp(jnp.where(tri, seg, NEG_INF))
            scores = (cb * decay).astype(BF16)
            y_diag.append(jnp.dot(scores, xdt[:, hh * hp:(hh + 1) * hp], preferred_element_type=F32))
    y = jnp.concatenate(y_diag, axis=1) + jnp.exp(cs_e) * jnp.concatenate(y_off, axis=1)
    state_ref[...] = jnp.exp(cs_last) * state + jnp.concatenate(new_states, axis=1)
    return y


def _ssd_fwd_kernel(u_ref, dt_ref, dtb_ref, alog_ref, skip_ref, y_ref, state_ref):
    @pl.when(pl.program_id(1) == 0)
    def _():
        state_ref[...] = jnp.zeros_like(state_ref)

    u = u_ref[...]
    y = _ssd_chunk(u, dt_ref[...], dtb_ref[...], alog_ref[...], state_ref, direction=0)
    y_ref[...] = y + skip_ref[...] * u[:, :SSM_INNER]


def _ssd_bwd_kernel(u_ref, dt_ref, dtb_ref, alog_ref, y0_ref, z_ref, nw_ref, o_ref, state_ref):
    @pl.when(pl.program_id(1) == 0)
    def _():
        state_ref[...] = jnp.zeros_like(state_ref)

    y = y0_ref[...] + _ssd_chunk(u_ref[...], dt_ref[...], dtb_ref[...], alog_ref[...], state_ref, direction=1)
    z = z_ref[...]
    gt = y * (z * _sigmoid(z))
    gw = SSM_INNER // SSM_GROUPS
    outs = []
    for g in range(SSM_GROUPS):
        gg = gt[:, g * gw:(g + 1) * gw]
        ms = jnp.mean(gg * gg, axis=-1, keepdims=True)
        outs.append(gg * lax.rsqrt(ms + EPS))
    o_ref[...] = (jnp.concatenate(outs, axis=1) * nw_ref[...]).astype(BF16)


def _ssd(u, dt_raw, z, dtb, alog, skip, ssm_nw, *, batch, seq, n_ctx):
    nt = u.shape[0]
    q = CHUNK
    ncl = seq // q
    ncc = n_ctx // q
    steps = ncc + ncl
    lat0 = 0
    ctx0 = (batch * seq) // q

    def fmap(b, t):
        return (jnp.where(t < ncc, ctx0 + b * ncc + t, lat0 + b * ncl + (t - ncc)), 0)

    def rmap(b, t):
        return (jnp.where(t < ncc, ctx0 + b * ncc + (ncc - 1 - t), lat0 + b * ncl + (ncl - 1 - (t - ncc))), 0)

    const = lambda b, t: (0, 0)
    state = pltpu.VMEM((SSM_STATE, SSM_INNER), F32)
    y0 = pl.pallas_call(
        _ssd_fwd_kernel,
        out_shape=jax.ShapeDtypeStruct((nt, SSM_INNER), F32),
        grid=(batch, steps),
        in_specs=[pl.BlockSpec((q, CONV_CH), fmap), pl.BlockSpec((q, LANES), fmap),
                  pl.BlockSpec((1, LANES), const), pl.BlockSpec((1, LANES), const),
                  pl.BlockSpec((1, SSM_INNER), const)],
        out_specs=pl.BlockSpec((q, SSM_INNER), fmap),
        scratch_shapes=[state],
        compiler_params=_params(2),
        name="ssd_forward",
    )(u, dt_raw, dtb, alog, skip)
    return pl.pallas_call(
        _ssd_bwd_kernel,
        out_shape=jax.ShapeDtypeStruct((nt, SSM_INNER), BF16),
        grid=(batch, steps),
        in_specs=[pl.BlockSpec((q, CONV_CH), rmap), pl.BlockSpec((q, LANES), rmap),
                  pl.BlockSpec((1, LANES), const), pl.BlockSpec((1, LANES), const),
                  pl.BlockSpec((q, SSM_INNER), rmap), pl.BlockSpec((q, SSM_INNER), rmap),
                  pl.BlockSpec((1, SSM_INNER), const)],
        out_specs=pl.BlockSpec((q, SSM_INNER), rmap),
        scratch_shapes=[state],
        compiler_params=_params(2),
        name="ssd_backward",
    )(u, dt_raw, dtb, alog, y0, z, ssm_nw)


MOE_TILE = 256


def _out_proj_kernel(x_ref, a_ref, s_ref, wa_ref, ws_ref, mod_ref, nw_ref, wr_ref, br_ref,
                     xo_ref, h_ref, idxt_ref, gate_ref, cnt_ref):
    mix = (jnp.dot(a_ref[...], wa_ref[...], preferred_element_type=F32)
           + jnp.dot(s_ref[...], ws_ref[...], preferred_element_type=F32))
    x = x_ref[...] + mod_ref[0, 2:3, :] * mix
    xo_ref[...] = x
    h = _rms_mod(x, nw_ref[...], mod_ref[0, 3:4, :], mod_ref[0, 4:5, :])
    h_ref[...] = h.astype(BF16)
    logits = jnp.dot(h, wr_ref[...], preferred_element_type=F32, precision=HIGHEST) + br_ref[...]
    tm = logits.shape[0]
    lane = lax.broadcasted_iota(I32, (tm, LANES), 1)
    lane_f = lane.astype(F32)
    work = jnp.where(lane < N_EXPERTS, logits, -jnp.inf)
    idx_out = jnp.zeros((tm, LANES), F32)
    val_out = jnp.full((tm, LANES), -jnp.inf, F32)
    onehot = jnp.zeros((tm, LANES), F32)
    for kk in range(TOP_K):
        m = jnp.max(work, axis=-1, keepdims=True)
        sel = jnp.min(jnp.where(work == m, lane_f, float(LANES)), axis=-1, keepdims=True)
        idx_out = jnp.where(lane == kk, sel, idx_out)
        val_out = jnp.where(lane == kk, m, val_out)
        picked = lane_f == sel
        onehot = onehot + jnp.where(picked, 1.0, 0.0)
        work = jnp.where(picked, -jnp.inf, work)
    top = jnp.max(val_out, axis=-1, keepdims=True)
    e = jnp.exp(val_out - top)
    gate_ref[...] = e / jnp.sum(e, axis=-1, keepdims=True)
    idxt_ref[...] = idx_out.T[0:SUBLANES, :].astype(I32)
    ones = jnp.ones((SUBLANES, MOE_TILE), BF16)
    oh_b = onehot.astype(BF16)
    for j in range(tm // MOE_TILE):
        cnt_ref[j] = jnp.dot(ones, oh_b[j * MOE_TILE:(j + 1) * MOE_TILE, :],
                             preferred_element_type=F32).astype(I32)


def _out_proj(xt, attn, ssm, w_out_a, w_out_s, mod_l, nfw, wr_p, br_p, *, seq, batch, n_lat):
    nt, d = xt.shape
    tm = _pick(math.gcd(seq, nt - n_lat), (512, 256))
    sub = tm // MOE_TILE
    row = lambda i: (i, 0)
    const = lambda i: (0, 0)
    return pl.pallas_call(
        _out_proj_kernel,
        out_shape=(jax.ShapeDtypeStruct((nt, d), F32), jax.ShapeDtypeStruct((nt, d), BF16),
                   jax.ShapeDtypeStruct((SUBLANES, nt), I32), jax.ShapeDtypeStruct((nt, LANES), F32),
                   jax.ShapeDtypeStruct((nt // MOE_TILE, SUBLANES, LANES), I32)),
        grid=(nt // tm,),
        in_specs=[pl.BlockSpec((tm, d), row), pl.BlockSpec((tm, ATTN_WIDTH), row),
                  pl.BlockSpec((tm, SSM_INNER), row),
                  pl.BlockSpec((ATTN_WIDTH, d), const), pl.BlockSpec((SSM_INNER, d), const),
                  pl.BlockSpec((1, N_MOD, d), lambda i: (jnp.minimum(i * tm // seq, batch), 0, 0)),
                  pl.BlockSpec((1, d), const), pl.BlockSpec((d, LANES), const), pl.BlockSpec((1, LANES), const)],
        out_specs=(pl.BlockSpec((tm, d), row), pl.BlockSpec((tm, d), row),
                   pl.BlockSpec((SUBLANES, tm), lambda i: (0, i)), pl.BlockSpec((tm, LANES), row),
                   pl.BlockSpec((sub, SUBLANES, LANES), lambda i: (i, 0, 0))),
        compiler_params=_params(1),
        name="out_proj_router",
    )(xt, attn, ssm, w_out_a, w_out_s, mod_l, nfw, wr_p, br_p)


RUN_BITS = MOE_TILE.bit_length()
TAB_CNT, TAB_OFF, TAB_BASE = 0, N_EXPERTS, 2 * N_EXPERTS
ROW_TILE = D_MODEL // LANES
assert ROW_TILE == SUBLANES


def _to_row_tiles(ref, base, val):
    n = val.shape[0]
    for s in range(ROW_TILE):
        ref[pl.ds(base + s, n, stride=ROW_TILE), :] = val[:, s * LANES:(s + 1) * LANES]


def _from_row_tiles(ref, base, n):
    return jnp.concatenate([ref[pl.ds(base + s, n, stride=ROW_TILE), :] for s in range(ROW_TILE)], axis=1)


def _tile_rows(start, size):
    return pl.ds(pl.multiple_of(start * ROW_TILE, ROW_TILE), size * ROW_TILE)


def _for_each_run(tab_ref, lanes, make_copy, fn):
    def per_expert(e, carry):
        cnt = tab_ref[0, lanes[0] + e]
        local = tab_ref[0, lanes[1] + e]
        glob = tab_ref[0, lanes[2] + e]
        for b in range(RUN_BITS - 1, -1, -1):
            size = 1 << b
            done = (cnt >> (b + 1)) << (b + 1)

            @pl.when((cnt & size) != 0)
            def _():
                fn(make_copy(local + done, glob + done, size))
        return carry

    lax.fori_loop(0, N_EXPERTS, per_expert, 0)


def _start(copy):
    copy.start()


def _wait(copy):
    copy.wait()


def _dispatch_kernel(tab_ref, tabp_ref, pad_ref, idxt_ref, h_ref, buf_ref, pos_ref,
                     scr_ref, zero_ref, sem, zsem, *, n_tiles):
    i = pl.program_id(0)
    slot = i % 2
    t = h_ref.shape[0]
    rows = TOP_K * t
    run_lanes = (TAB_CNT, TAB_OFF, TAB_BASE)

    def copy_out(s):
        return lambda local, glob, size: pltpu.make_async_copy(
            scr_ref.at[_tile_rows(s * rows + local, size), :], buf_ref.at[_tile_rows(glob, size), :], sem.at[s])

    @pl.when(i == 0)
    def _():
        zero_ref[...] = jnp.zeros_like(zero_ref)
        zero_copy = lambda local, glob, size: pltpu.make_async_copy(
            zero_ref.at[_tile_rows(0, size), :], buf_ref.at[_tile_rows(glob, size), :], zsem)
        pad_lanes = (0, 0, N_EXPERTS)
        _for_each_run(pad_ref, pad_lanes, zero_copy, _start)
        _for_each_run(pad_ref, pad_lanes, zero_copy, _wait)
        tail_start = pad_ref[0, 2 * N_EXPERTS]
        n_tail = pad_ref[0, 2 * N_EXPERTS + 1]

        def tail(fn):
            def body(j, carry):
                fn(zero_copy(0, tail_start + j * t, t))
                return carry
            return body

        lax.fori_loop(0, n_tail, tail(_start), 0)
        lax.fori_loop(0, n_tail, tail(_wait), 0)

    e_iota = lax.broadcasted_iota(I32, (N_EXPERTS, t), 0)
    upper = (lax.broadcasted_iota(I32, (t, t), 0) < lax.broadcasted_iota(I32, (t, t), 1)).astype(BF16)
    onehots, counts, before = [], [], []
    for kk in range(TOP_K):
        oh = jnp.where(e_iota == idxt_ref[kk:kk + 1, :], 1.0, 0.0)
        onehots.append(oh)
        counts.append(jnp.sum(oh, axis=1, keepdims=True))
        before.append(jnp.dot(oh.astype(BF16), upper, preferred_element_type=F32))
    total = counts[0] + counts[1] + counts[2] + counts[3]
    below = (lax.broadcasted_iota(I32, (N_EXPERTS, N_EXPERTS), 1)
             < lax.broadcasted_iota(I32, (N_EXPERTS, N_EXPERTS), 0)).astype(BF16)
    start = _dot3(jnp.broadcast_to(total, (N_EXPERTS, LANES)), below, left=True)[:, 0:1]
    pos_rows = []
    for kk in range(TOP_K):
        pos_rows.append(jnp.sum(onehots[kk] * (before[kk] + start), axis=0, keepdims=True))
        start = start + counts[kk]
    r_iota = lax.broadcasted_iota(I32, (rows, t), 0)
    perm = jnp.zeros((rows, t), F32)
    for kk in range(TOP_K):
        perm = jnp.where(r_iota == pos_rows[kk].astype(I32), 1.0, perm)
    _to_row_tiles(scr_ref, slot * rows * ROW_TILE,
                  jnp.dot(perm.astype(BF16), h_ref[...], preferred_element_type=F32))
    pos_t = jnp.concatenate(pos_rows + [jnp.zeros((LANES - TOP_K, t), F32)], axis=0)
    pos_ref[...] = pos_t.T.astype(I32)

    _for_each_run(tab_ref, run_lanes, copy_out(slot), _start)

    @pl.when(i > 0)
    def _():
        _for_each_run(tabp_ref, run_lanes, copy_out(1 - slot), _wait)

    @pl.when(i == n_tiles - 1)
    def _():
        _for_each_run(tab_ref, run_lanes, copy_out(slot), _wait)


def _dispatch(h2, idxt, table, padtab, *, n_rows):
    nt, d = h2.shape
    t = MOE_TILE
    n_tiles = nt // t
    smem = lambda imap: pl.BlockSpec((None, 1, LANES), imap, memory_space=pltpu.SMEM)
    kern = functools.partial(_dispatch_kernel, n_tiles=n_tiles)
    return pl.pallas_call(
        kern,
        out_shape=(jax.ShapeDtypeStruct((n_rows * ROW_TILE, LANES), F32),
                   jax.ShapeDtypeStruct((nt, LANES), I32)),
        grid=(n_tiles,),
        in_specs=[smem(lambda i: (i, 0, 0)), smem(lambda i: (jnp.maximum(i - 1, 0), 0, 0)),
                  smem(lambda i: (0, 0, 0)),
                  pl.BlockSpec((SUBLANES, t), lambda i: (0, i)),
                  pl.BlockSpec((t, d), lambda i: (i, 0))],
        out_specs=(pl.BlockSpec(memory_space=pl.ANY), pl.BlockSpec((t, LANES), lambda i: (i, 0))),
        scratch_shapes=[pltpu.VMEM((2 * TOP_K * t * ROW_TILE, LANES), F32),
                        pltpu.VMEM((t * ROW_TILE, LANES), F32),
                        pltpu.SemaphoreType.DMA((2,)), pltpu.SemaphoreType.DMA],
        compiler_params=_params(1),
        name="moe_dispatch",
    )(table, table, padtab, idxt, h2)


def _expert_kernel(be_ref, nu_ref, x_ref, wgu_ref, bgu_ref, wd_ref, bd_ref, o_ref):
    i = pl.program_id(0)

    tme = x_ref.shape[0] // ROW_TILE

    @pl.when(i < nu_ref[0])
    def _():
        x = _from_row_tiles(x_ref, 0, tme)
        gu = jnp.dot(x.astype(BF16), wgu_ref[...], preferred_element_type=F32) + bgu_ref[...]
        glu = jnp.minimum(gu[:, :D_FF], SWIGLU_LIMIT)
        lin = jnp.clip(gu[:, D_FF:], -SWIGLU_LIMIT, SWIGLU_LIMIT)
        act = glu * _sigmoid(SWIGLU_ALPHA * glu) * (lin + 1.0)
        _to_row_tiles(o_ref, 0, jnp.dot(act.astype(BF16), wd_ref[...], preferred_element_type=F32) + bd_ref[...])

    @pl.when(i >= nu_ref[0])
    def _():
        o_ref[...] = jnp.zeros_like(o_ref)


def _experts(buf, block_e, n_used, wgu, bgu, wd, bd, *, tme):
    d = D_MODEL
    nblk = buf.shape[0] // (tme * ROW_TILE)
    xmap = lambda i, be, nu: (jnp.minimum(i, nu[0] - 1), 0)
    emap3 = lambda i, be, nu: (be[i], 0, 0)
    return pl.pallas_call(
        _expert_kernel,
        out_shape=jax.ShapeDtypeStruct(buf.shape, F32),
        grid_spec=pltpu.PrefetchScalarGridSpec(
            num_scalar_prefetch=2,
            grid=(nblk,),
            in_specs=[pl.BlockSpec((tme * ROW_TILE, LANES), xmap),
                      pl.BlockSpec((None, d, 2 * D_FF), emap3),
                      pl.BlockSpec((None, 1, 2 * D_FF), emap3),
                      pl.BlockSpec((None, D_FF, d), emap3),
                      pl.BlockSpec((None, 1, d), emap3)],
            out_specs=pl.BlockSpec((tme * ROW_TILE, LANES), lambda i, be, nu: (i, 0))),
        compiler_params=_params(1),
        name="expert_ffn",
    )(block_e, n_used, buf, wgu, bgu, wd, bd)


def _combine_kernel(tab_ref, tabn_ref, pos_ref, gate_ref, x_ref, mod_ref, fw_ref, ob_ref, o_ref,
                    scr_ref, sem, *, n_tiles, final_norm):
    i = pl.program_id(0)
    slot = i % 2
    t = x_ref.shape[0]
    rows = TOP_K * t
    run_lanes = (TAB_CNT, TAB_OFF, TAB_BASE)

    def copy_in(s):
        return lambda local, glob, size: pltpu.make_async_copy(
            ob_ref.at[_tile_rows(glob, size), :], scr_ref.at[_tile_rows(s * rows + local, size), :], sem.at[s])

    @pl.when(i == 0)
    def _():
        _for_each_run(tab_ref, run_lanes, copy_in(slot), _start)

    @pl.when(i + 1 < n_tiles)
    def _():
        _for_each_run(tabn_ref, run_lanes, copy_in(1 - slot), _start)

    _for_each_run(tab_ref, run_lanes, copy_in(slot), _wait)

    lane = lax.broadcasted_iota(I32, (t, rows), 1)
    pw = jnp.zeros((t, rows), F32)
    for kk in range(TOP_K):
        pw = jnp.where(lane == pos_ref[:, kk:kk + 1], gate_ref[:, kk:kk + 1], pw)
    pw_hi = pw.astype(BF16)
    pw_lo = (pw - pw_hi.astype(F32)).astype(BF16)
    y = _from_row_tiles(scr_ref, slot * rows * ROW_TILE, rows).astype(BF16)
    f = jnp.dot(pw_hi, y, preferred_element_type=F32) + jnp.dot(pw_lo, y, preferred_element_type=F32)
    x = x_ref[...] + mod_ref[0, 5:6, :] * f
    if final_norm:
        ms = jnp.mean(x * x, axis=-1, keepdims=True)
        x = x * lax.rsqrt(ms + EPS) * fw_ref[...]
    o_ref[...] = x


def _combine(xt, out_buf, pos, gates, table, mod_l, fw, *, seq, batch, final_norm):
    nt, d = xt.shape
    t = MOE_TILE
    n_tiles = nt // t
    row = lambda i: (i, 0)
    smem = lambda imap: pl.BlockSpec((None, 1, LANES), imap, memory_space=pltpu.SMEM)
    kern = functools.partial(_combine_kernel, n_tiles=n_tiles, final_norm=final_norm)
    return pl.pallas_call(
        kern,
        out_shape=jax.ShapeDtypeStruct((nt, d), F32),
        grid=(n_tiles,),
        in_specs=[smem(lambda i: (i, 0, 0)), smem(lambda i: (jnp.minimum(i + 1, n_tiles - 1), 0, 0)),
                  pl.BlockSpec((t, LANES), row), pl.BlockSpec((t, LANES), row), pl.BlockSpec((t, d), row),
                  pl.BlockSpec((1, N_MOD, d), lambda i: (jnp.minimum(i * t // seq, batch), 0, 0)),
                  pl.BlockSpec((1, d), lambda i: (0, 0)),
                  pl.BlockSpec(memory_space=pl.ANY)],
        out_specs=pl.BlockSpec((t, d), row),
        scratch_shapes=[pltpu.VMEM((2 * TOP_K * t * ROW_TILE, LANES), F32), pltpu.SemaphoreType.DMA((2,))],
        compiler_params=_params(1),
        name="moe_combine",
    )(table, table, pos, gates, xt, mod_l, fw, out_buf)


def _route_tables(counts, *, tme, n_blocks):
    cnt = counts[:, 0, :N_EXPERTS]
    total = jnp.sum(cnt, axis=0)
    padded = (total + tme - 1) // tme * tme
    pad_end = jnp.cumsum(padded)
    pad_start = pad_end - padded
    base = pad_start[None, :] + jnp.cumsum(cnt, axis=0) - cnt
    off = jnp.cumsum(cnt, axis=1) - cnt
    table = jnp.concatenate([cnt, off, base, jnp.zeros_like(cnt)], axis=1).astype(I32)[:, None, :]
    tail = jnp.stack([pad_end[-1], (n_blocks * tme - pad_end[-1]) // MOE_TILE])
    padtab = jnp.concatenate([padded - total, pad_start + total, tail,
                              jnp.zeros((LANES - 2 * N_EXPERTS - 2,), I32)]).astype(I32)[None, None, :]
    n_used = (pad_end[-1] // tme).astype(I32).reshape(1)
    block_e = jnp.minimum(
        jnp.searchsorted(pad_end, jnp.arange(n_blocks, dtype=I32) * tme, side='right'),
        N_EXPERTS - 1).astype(I32)
    return table, padtab, block_e, n_used


def _rope_tables(seq):
    rows = seq // GRID_W
    row_pos = jnp.repeat(jnp.arange(rows, dtype=I32), GRID_W).astype(F32)
    col_pos = jnp.tile(jnp.arange(GRID_W, dtype=I32), rows).astype(F32)
    n_freq = HEAD_DIM // 4
    inv_freq = ROPE_BASE ** (-jnp.arange(n_freq, dtype=F32) / n_freq)
    lane = jnp.arange(LANES)
    f = lane % n_freq
    use_col = (lane % HEAD_DIM) >= HEAD_DIM // 2
    ang = jnp.where(use_col[None, :], col_pos[:, None], row_pos[:, None]) * inv_freq[f][None, :]
    first_half = (lane % 32) < 16
    return jnp.cos(ang), jnp.where(first_half[None, :], -jnp.sin(ang), jnp.sin(ang))


def kernel(x, c, ctx, c_ctx, w_ada, b_ada, norm_mix_w, norm_ffn_w, w_in, conv_w, conv_b, dt_bias, a_log, d_skip, ssm_norm_w, attn_sinks, attn_norm_w, w_out, w_router, b_router, w_gate_up, b_gate_up, w_down, b_down, final_norm_w):
    batch, seq, d = x.shape
    n_ctx = ctx.shape[1]
    depth = w_ada.shape[0]
    n_lat = batch * seq
    nt = n_lat + batch * n_ctx
    assert d == D_MODEL and seq % ATT_BLOCK == 0 and n_ctx % ATT_BLOCK == 0 and n_lat % n_ctx == 0

    r_mod = -(-(batch + 1) // SUBLANES) * SUBLANES
    c_all = jnp.zeros((r_mod, d), F32).at[:batch].set(c).at[batch].set(c_ctx)
    mod = _modulation(c_all, w_ada, b_ada)
    cos_t, sin_t = _rope_tables(seq)

    xt = jnp.concatenate([x.reshape(n_lat, d), ctx.reshape(batch * n_ctx, d)], axis=0)
    tme = 512
    assert nt % MOE_TILE == 0 and tme % MOE_TILE == 0 and tme <= 1 << RUN_BITS
    n_blocks = -(-(nt * TOP_K) // tme) + N_EXPERTS
    hp = SSM_INNER // SSM_HEADS

    for l in range(depth):
        last = l == depth - 1
        w_in_p = jnp.pad(w_in[l], ((0, 0), (0, IN_PAD - w_in.shape[2]))).astype(BF16)
        q, k, v, z, xbc, dt_raw = _in_proj(xt, norm_mix_w[l].reshape(1, d), mod[l], w_in_p, cos_t, sin_t,
                                           n_lat=n_lat, seq=seq, batch=batch)
        attn = _attention(q, k, v, attn_sinks[l], attn_norm_w[l].reshape(1, ATTN_WIDTH),
                          batch=batch, seq=seq, n_ctx=n_ctx, with_ctx_queries=True)
        u = _conv_silu(xbc, conv_w[l], conv_b[l], seq=seq, n_lat=n_lat, n_ctx=n_ctx)
        pad16 = lambda t: jnp.pad(t.reshape(1, N_DIRS * SSM_HEADS), ((0, 0), (0, LANES - N_DIRS * SSM_HEADS)))
        ssm = _ssd(u, dt_raw, z, pad16(dt_bias[l]), pad16(a_log[l]),
                   jnp.repeat(d_skip[l], hp).reshape(1, SSM_INNER), ssm_norm_w[l].reshape(1, SSM_INNER),
                   batch=batch, seq=seq, n_ctx=n_ctx)
        w_o = w_out[l].astype(BF16)
        wr_p = jnp.pad(w_router[l], ((0, 0), (0, LANES - N_EXPERTS)))
        br_p = jnp.pad(b_router[l].reshape(1, N_EXPERTS), ((0, 0), (0, LANES - N_EXPERTS)))
        xt, h2, idxt, gates, counts = _out_proj(xt, attn, ssm, w_o[:ATTN_WIDTH], w_o[ATTN_WIDTH:], mod[l],
                                                norm_ffn_w[l].reshape(1, d), wr_p, br_p,
                                                seq=seq, batch=batch, n_lat=n_lat)
        table, padtab, block_e, n_used = _route_tables(counts, tme=tme, n_blocks=n_blocks)
        buf, pos = _dispatch(h2, idxt, table, padtab, n_rows=n_blocks * tme)
        out_buf = _experts(buf, block_e, n_used, w_gate_up[l].astype(BF16),
                           b_gate_up[l].reshape(N_EXPERTS, 1, 2 * D_FF), w_down[l].astype(BF16),
                           b_down[l].reshape(N_EXPERTS, 1, d), tme=tme)
        xt = _combine(xt, out_buf, pos, gates, table, mod[l], final_norm_w.reshape(1, d),
                      seq=seq, batch=batch, final_norm=last)
    return xt[:n_lat].reshape(batch, seq, d)
```

```python
import functools
import math

import jax
import jax.numpy as jnp
from jax import lax
from jax.experimental import pallas as pl
from jax.experimental.pallas import tpu as pltpu

F32 = jnp.float32
BF16 = jnp.bfloat16
I32 = jnp.int32

D_MODEL = 1024
GRID_W = 64
N_MOD = 6
EPS = 1e-6
NEG_INF = -1e30

HEAD_DIM = 64
ATTN_WIDTH = 512
N_HEADS = 8
N_KV_HEADS = 2
KV_WIDTH = 128
ATT_BLOCK = 128
ROPE_BASE = 10000.0

SSM_INNER = 512
SSM_HEADS = 8
SSM_GROUPS = 2
SSM_STATE = 64
CONV_W = 5
CONV_CH = 768
CHUNK = 128
N_DIRS = 2

N_EXPERTS = 32
TOP_K = 4
D_FF = 1024
SWIGLU_LIMIT = 7.0
SWIGLU_ALPHA = 1.702

LANES = 128
SUBLANES = 8
IN_PAD = 2176
VMEM_LIMIT = 56 * 1024 * 1024
HIGHEST = lax.Precision.HIGHEST


def _params(n_axes, vmem=VMEM_LIMIT):
    return pltpu.CompilerParams(dimension_semantics=("arbitrary",) * n_axes, vmem_limit_bytes=vmem)


def _pick(n, prefs):
    for t in prefs:
        if n % t == 0:
            return t
    raise ValueError(f"no tile in {prefs} divides {n}")


def _sigmoid(x):
    return 1.0 / (1.0 + jnp.exp(-x))


def _dot3(x, m_bf16, left=False):
    hi = x.astype(BF16)
    r1 = x - hi.astype(F32)
    mid = r1.astype(BF16)
    lo = (r1 - mid.astype(F32)).astype(BF16)
    mm = (lambda p: jnp.dot(m_bf16, p, preferred_element_type=F32)) if left else (
        lambda p: jnp.dot(p, m_bf16, preferred_element_type=F32))
    return mm(hi) + mm(mid) + mm(lo)


def _mod_kernel(c_ref, w_ref, b_ref, o_ref):
    c = c_ref[...]
    s = c * _sigmoid(c)
    o_ref[...] = jnp.dot(s, w_ref[...], preferred_element_type=F32, precision=HIGHEST) + b_ref[...]


def _modulation(c_all, w_ada, b_ada):
    depth, d, n = w_ada.shape
    r = c_all.shape[0]
    tn = _pick(n, (1536, 1024, 512, 128))
    out = pl.pallas_call(
        _mod_kernel,
        out_shape=jax.ShapeDtypeStruct((depth, r, n), F32),
        grid=(depth, n // tn),
        in_specs=[pl.BlockSpec((r, d), lambda l, j: (0, 0)),
                  pl.BlockSpec((None, d, tn), lambda l, j: (l, 0, j)),
                  pl.BlockSpec((None, 1, tn), lambda l, j: (l, 0, j))],
        out_specs=pl.BlockSpec((None, r, tn), lambda l, j: (l, 0, j)),
        compiler_params=_params(2),
        name="adaln_mod",
    )(c_all, w_ada, b_ada.reshape(depth, 1, n))
    return out.reshape(depth, r, N_MOD, d)


def _rms_mod(x, nw, shift, scale):
    ms = jnp.mean(x * x, axis=-1, keepdims=True)
    y = x * lax.rsqrt(ms + EPS) * nw
    return y * (1.0 + scale) + shift


def _in_proj_kernel(xl_ref, xc_ref, nw_ref, mod_ref, w_ref, cos_ref, sin_ref,
                    q_ref, k_ref, v_ref, z_ref, xbc_ref, dt_ref, *, n_lat_tiles):
    i = pl.program_id(0)
    is_lat = i < n_lat_tiles
    x = jnp.where(is_lat, xl_ref[...], xc_ref[...])
    h = _rms_mod(x, nw_ref[...], mod_ref[0, 0:1, :], mod_ref[0, 1:2, :])
    p = jnp.dot(h.astype(BF16), w_ref[...], preferred_element_type=F32)
    tm = p.shape[0]
    cos = jnp.where(is_lat, cos_ref[...], 1.0)
    sin = jnp.where(is_lat, sin_ref[...], 0.0)
    lane = lax.broadcasted_iota(I32, (tm, LANES), 1)
    first_half = (lane & 31) < 16

    def rope(t):
        partner = jnp.where(first_half, pltpu.roll(t, LANES - 16, 1), pltpu.roll(t, 16, 1))
        return t * cos + partner * sin

    for j in range(ATTN_WIDTH // LANES):
        q_ref[:, j * LANES:(j + 1) * LANES] = rope(p[:, j * LANES:(j + 1) * LANES]).astype(BF16)
    k_ref[...] = rope(p[:, 512:640]).astype(BF16)
    v_ref[...] = p[:, 640:768].astype(BF16)
    z_ref[...] = p[:, 768:1280]
    xbc_ref[...] = p[:, 1280:2048]
    dt_ref[...] = p[:, 2048:2176]


def _token_specs(tm, d, n_lat_tiles, ctx_tile0):
    return [pl.BlockSpec((tm, d), lambda i: (jnp.minimum(i, n_lat_tiles - 1), 0)),
            pl.BlockSpec((tm, d), lambda i: (jnp.maximum(i - n_lat_tiles, 0) + ctx_tile0, 0))]


def _in_proj(x_lat, x_ctx, ctx_row0, nw, mod_l, w_in_p, cos_t, sin_t, *, nt, n_lat, seq, batch):
    d = x_lat.shape[1]
    tm = _pick(math.gcd(seq, nt - n_lat), (512, 256, 128))
    n_pos_tiles = seq // tm
    kern = functools.partial(_in_proj_kernel, n_lat_tiles=n_lat // tm)
    row = lambda i: (i, 0)
    return pl.pallas_call(
        kern,
        out_shape=(jax.ShapeDtypeStruct((nt, ATTN_WIDTH), BF16),
                   jax.ShapeDtypeStruct((nt, KV_WIDTH), BF16),
                   jax.ShapeDtypeStruct((nt, KV_WIDTH), BF16),
                   jax.ShapeDtypeStruct((nt, SSM_INNER), F32),
                   jax.ShapeDtypeStruct((nt, CONV_CH), F32),
                   jax.ShapeDtypeStruct((nt, LANES), F32)),
        grid=(nt // tm,),
        in_specs=_token_specs(tm, d, n_lat // tm, ctx_row0 // tm) + [
                  pl.BlockSpec((1, d), lambda i: (0, 0)),
                  pl.BlockSpec((1, N_MOD, d), lambda i: (jnp.minimum(i * tm // seq, batch), 0, 0)),
                  pl.BlockSpec((d, IN_PAD), lambda i: (0, 0)),
                  pl.BlockSpec((tm, LANES), lambda i: (i % n_pos_tiles, 0)),
                  pl.BlockSpec((tm, LANES), lambda i: (i % n_pos_tiles, 0))],
        out_specs=(pl.BlockSpec((tm, ATTN_WIDTH), row), pl.BlockSpec((tm, KV_WIDTH), row),
                   pl.BlockSpec((tm, KV_WIDTH), row), pl.BlockSpec((tm, SSM_INNER), row),
                   pl.BlockSpec((tm, CONV_CH), row), pl.BlockSpec((tm, LANES), row)),
        compiler_params=_params(1),
        name="in_proj",
    )(x_lat, x_ctx, nw, mod_l, w_in_p, cos_t, sin_t)


def _attn_kernel(sink_ref, q_ref, kp_ref, kc_ref, kn_ref, vp_ref, vc_ref, vn_ref, kx_ref, vx_ref,
                 nw_ref, o_ref, acc_ref, *, nb, n_ctx):
    n = pl.program_id(1)
    is_lat = n < nb
    blk = ATT_BLOCK
    rep = N_HEADS // N_KV_HEADS
    nkeys = n_ctx + 3 * blk
    rows = rep * blk
    r = lax.broadcasted_iota(I32, (rows, nkeys), 0) & (blk - 1)
    c = lax.broadcasted_iota(I32, (rows, nkeys), 1)
    cw = c - n_ctx
    prev_ok = (cw >= r) & (cw < blk) & (n > 0)
    cur_ok = (cw >= blk) & (cw < 2 * blk)
    next_ok = (cw >= 2 * blk) & (cw - 2 * blk <= r) & (n < nb - 1)
    valid = (c < n_ctx) | (is_lat & (prev_ok | cur_ok | next_ok))
    q = q_ref[...]
    for g in range(N_KV_HEADS):
        sl = slice(g * HEAD_DIM, (g + 1) * HEAD_DIM)
        kg = jnp.concatenate([kx_ref[:, sl], kp_ref[:, sl], kc_ref[:, sl], kn_ref[:, sl]], axis=0)
        vg = jnp.concatenate([vx_ref[:, sl], vp_ref[:, sl], vc_ref[:, sl], vn_ref[:, sl]], axis=0)
        qg = jnp.concatenate([q[:, (g * rep + j) * HEAD_DIM:(g * rep + j + 1) * HEAD_DIM]
                              for j in range(rep)], axis=0)
        s = lax.dot_general(qg, kg, (((1,), (1,)), ((), ())), preferred_element_type=F32)
        s = jnp.where(valid, s * (HEAD_DIM ** -0.5), NEG_INF)
        sink = jnp.concatenate([jnp.full((blk, 1), sink_ref[g * rep + j], F32) for j in range(rep)], axis=0)
        m = jnp.maximum(jnp.max(s, axis=-1, keepdims=True), sink)
        e = jnp.exp(s - m)
        denom = jnp.sum(e, axis=-1, keepdims=True) + jnp.exp(sink - m)
        og = jnp.dot(e.astype(BF16), vg, preferred_element_type=F32) / denom
        for j in range(rep):
            hh = g * rep + j
            acc_ref[:, hh * HEAD_DIM:(hh + 1) * HEAD_DIM] = og[j * blk:(j + 1) * blk, :]
    a = acc_ref[...]
    ms = jnp.mean(a * a, axis=-1, keepdims=True)
    o_ref[...] = (a * lax.rsqrt(ms + EPS) * nw_ref[...]).astype(BF16)


def _attention(q, k, v, sinks, nw, *, batch, seq, n_ctx, with_ctx_queries):
    nt = q.shape[0]
    blk = ATT_BLOCK
    nb = seq // blk
    ncq = n_ctx // blk
    nq = nb + (ncq if with_ctx_queries else 0)
    ctx_blk0 = (batch * seq) // n_ctx

    def qmap(b, n, s):
        return (jnp.where(n < nb, b * nb + n, batch * nb + b * ncq + (n - nb)), 0)

    def kmap(off):
        return lambda b, n, s: (b * nb + jnp.clip(n + off, 0, nb - 1), 0)

    xmap = lambda b, n, s: (ctx_blk0 + b, 0)
    kern = functools.partial(_attn_kernel, nb=nb, n_ctx=n_ctx)
    kv_spec = lambda off: pl.BlockSpec((blk, KV_WIDTH), kmap(off))
    return pl.pallas_call(
        kern,
        out_shape=jax.ShapeDtypeStruct((nt if with_ctx_queries else batch * seq, ATTN_WIDTH), BF16),
        grid_spec=pltpu.PrefetchScalarGridSpec(
            num_scalar_prefetch=1,
            grid=(batch, nq),
            in_specs=[pl.BlockSpec((blk, ATTN_WIDTH), qmap),
                      kv_spec(-1), kv_spec(0), kv_spec(1),
                      kv_spec(-1), kv_spec(0), kv_spec(1),
                      pl.BlockSpec((n_ctx, KV_WIDTH), xmap),
                      pl.BlockSpec((n_ctx, KV_WIDTH), xmap),
                      pl.BlockSpec((1, ATTN_WIDTH), lambda b, n, s: (0, 0))],
            out_specs=pl.BlockSpec((blk, ATTN_WIDTH), qmap),
            scratch_shapes=[pltpu.VMEM((blk, ATTN_WIDTH), F32)]),
        compiler_params=_params(2),
        name="attention",
    )(sinks, q, k, k, k, v, v, v, k, v, nw)


def _conv_kernel(xp_ref, xc_ref, xn_ref, w_ref, b_ref, o_ref, ext_ref, *, seq, n_lat, n_ctx):
    i = pl.program_id(0)
    tb = xc_ref.shape[0]
    row0 = i * tb
    in_lat = row0 < n_lat
    local = jnp.where(in_lat, row0 % seq, (row0 - n_lat) % n_ctx)
    length = jnp.where(in_lat, seq, n_ctx)
    first = local == 0
    last = local + tb == length
    h = SUBLANES
    ext_ref[0:h, :] = jnp.where(first, 0.0, xp_ref[...])
    ext_ref[h:h + tb, :] = xc_ref[...]
    ext_ref[h + tb:h + tb + h, :] = jnp.where(last, 0.0, xn_ref[...])
    acc = jnp.zeros((tb, CONV_CH), F32) + b_ref[...]
    for kk in range(CONV_W):
        off = h - CONV_W // 2 + kk
        acc = acc + ext_ref[off:off + tb, :] * w_ref[kk:kk + 1, :]
    o_ref[...] = acc * _sigmoid(acc)


def _conv_silu(xbc, conv_w, conv_b, *, seq, n_lat, n_ctx):
    nt = xbc.shape[0]
    tb = _pick(math.gcd(seq, n_ctx), (256, 128))
    h = SUBLANES
    per = tb // h
    n_h = nt // h
    kern = functools.partial(_conv_kernel, seq=seq, n_lat=n_lat, n_ctx=n_ctx)
    return pl.pallas_call(
        kern,
        out_shape=jax.ShapeDtypeStruct((nt, CONV_CH), F32),
        grid=(nt // tb,),
        in_specs=[pl.BlockSpec((h, CONV_CH), lambda i: (jnp.maximum(i * per - 1, 0), 0)),
                  pl.BlockSpec((tb, CONV_CH), lambda i: (i, 0)),
                  pl.BlockSpec((h, CONV_CH), lambda i: (jnp.minimum((i + 1) * per, n_h - 1), 0)),
                  pl.BlockSpec((CONV_W, CONV_CH), lambda i: (0, 0)),
                  pl.BlockSpec((1, CONV_CH), lambda i: (0, 0))],
        out_specs=pl.BlockSpec((tb, CONV_CH), lambda i: (i, 0)),
        scratch_shapes=[pltpu.VMEM((tb + 2 * h, CONV_CH), F32)],
        compiler_params=_params(1),
        name="conv_silu",
    )(xbc, xbc, xbc, conv_w, conv_b.reshape(1, CONV_CH))


def _ssd_chunk(u, dtraw, dtb, alog, state_ref, *, direction):
    q = CHUNK
    hp = SSM_INNER // SSM_HEADS
    per_g = SSM_HEADS // SSM_GROUPS
    gw = per_g * hp
    xs = u[:, :SSM_INNER]
    bm = u[:, SSM_INNER:SSM_INNER + SSM_GROUPS * SSM_STATE]
    cm = u[:, SSM_INNER + SSM_GROUPS * SSM_STATE:]

    xv = dtraw + dtb
    dt = jnp.maximum(xv, 0.0) + jnp.log1p(jnp.exp(-jnp.abs(xv)))
    dta = dt * (-jnp.exp(alog))

    ri = lax.broadcasted_iota(I32, (q, q), 0)
    ci = lax.broadcasted_iota(I32, (q, q), 1)
    tri = (ci <= ri) if direction == 0 else (ci >= ri)
    cs = _dot3(dta, tri.astype(BF16), left=True)
    cs_t = cs.T

    er = lax.broadcasted_iota(I32, (LANES, SSM_INNER), 0)
    ec = lax.broadcasted_iota(I32, (LANES, SSM_INNER), 1)
    expand = (er == direction * SSM_HEADS + jnp.right_shift(ec, hp.bit_length() - 1)).astype(BF16)
    cs_e = _dot3(cs, expand)
    dt_e = _dot3(dt, expand)
    last = q - 1 if direction == 0 else 0
    cs_last = cs_e[last:last + 1, :]

    xdt = (xs * dt_e).astype(BF16)
    xw = (xs * (jnp.exp(cs_last - cs_e) * dt_e)).astype(BF16)
    bm_t = bm.T.astype(BF16)
    cmb = cm.astype(BF16)
    bmb = bm.astype(BF16)
    state = state_ref[...]
    state_b = state.astype(BF16)

    y_diag = []
    y_off = []
    new_states = []
    for g in range(SSM_GROUPS):
        gs = slice(g * SSM_STATE, (g + 1) * SSM_STATE)
        cb = lax.dot_general(cmb[:, gs], bmb[:, gs], (((1,), (1,)), ((), ())), preferred_element_type=F32)
        y_off.append(jnp.dot(cmb[:, gs], state_b[:, g * gw:(g + 1) * gw], preferred_element_type=F32))
        new_states.append(jnp.dot(bm_t[gs, :], xw[:, g * gw:(g + 1) * gw], preferred_element_type=F32))
        for j in range(per_g):
            hh = g * per_g + j
            col = direction * SSM_HEADS + hh
            seg = cs[:, col:col + 1] - cs_t[col:col + 1, :]
            decay = jnp.exp(jnp.where(tri, seg, NEG_INF))
            scores = (cb * decay).astype(BF16)
            y_diag.append(jnp.dot(scores, xdt[:, hh * hp:(hh + 1) * hp], preferred_element_type=F32))
    y = jnp.concatenate(y_diag, axis=1) + jnp.exp(cs_e) * jnp.concatenate(y_off, axis=1)
    state_ref[...] = jnp.exp(cs_last) * state + jnp.concatenate(new_states, axis=1)
    return y


def _ssd_fwd_kernel(u_ref, dt_ref, dtb_ref, alog_ref, skip_ref, y_ref, state_ref):
    @pl.when(pl.program_id(1) == 0)
    def _():
        state_ref[...] = jnp.zeros_like(state_ref)

    u = u_ref[...]
    y = _ssd_chunk(u, dt_ref[...], dtb_ref[...], alog_ref[...], state_ref, direction=0)
    y_ref[...] = y + skip_ref[...] * u[:, :SSM_INNER]


def _ssd_bwd_kernel(u_ref, dt_ref, dtb_ref, alog_ref, y0_ref, z_ref, nw_ref, o_ref, state_ref):
    @pl.when(pl.program_id(1) == 0)
    def _():
        state_ref[...] = jnp.zeros_like(state_ref)

    y = y0_ref[...] + _ssd_chunk(u_ref[...], dt_ref[...], dtb_ref[...], alog_ref[...], state_ref, direction=1)
    z = z_ref[...]
    gt = y * (z * _sigmoid(z))
    gw = SSM_INNER // SSM_GROUPS
    outs = []
    for g in range(SSM_GROUPS):
        gg = gt[:, g * gw:(g + 1) * gw]
        ms = jnp.mean(gg * gg, axis=-1, keepdims=True)
        outs.append(gg * lax.rsqrt(ms + EPS))
    o_ref[...] = (jnp.concatenate(outs, axis=1) * nw_ref[...]).astype(BF16)


def _ssd(u, dt_raw, z, dtb, alog, skip, ssm_nw, *, batch, seq, n_ctx):
    nt = u.shape[0]
    q = CHUNK
    ncl = seq // q
    ncc = n_ctx // q
    steps = ncc + ncl
    lat0 = 0
    ctx0 = (batch * seq) // q

    def fmap(b, t):
        return (jnp.where(t < ncc, ctx0 + b * ncc + t, lat0 + b * ncl + (t - ncc)), 0)

    def rmap(b, t):
        return (jnp.where(t < ncc, ctx0 + b * ncc + (ncc - 1 - t), lat0 + b * ncl + (ncl - 1 - (t - ncc))), 0)

    const = lambda b, t: (0, 0)
    state = pltpu.VMEM((SSM_STATE, SSM_INNER), F32)
    y0 = pl.pallas_call(
        _ssd_fwd_kernel,
        out_shape=jax.ShapeDtypeStruct((nt, SSM_INNER), F32),
        grid=(batch, steps),
        in_specs=[pl.BlockSpec((q, CONV_CH), fmap), pl.BlockSpec((q, LANES), fmap),
                  pl.BlockSpec((1, LANES), const), pl.BlockSpec((1, LANES), const),
                  pl.BlockSpec((1, SSM_INNER), const)],
        out_specs=pl.BlockSpec((q, SSM_INNER), fmap),
        scratch_shapes=[state],
        compiler_params=_params(2),
        name="ssd_forward",
    )(u, dt_raw, dtb, alog, skip)
    return pl.pallas_call(
        _ssd_bwd_kernel,
        out_shape=jax.ShapeDtypeStruct((nt, SSM_INNER), BF16),
        grid=(batch, steps),
        in_specs=[pl.BlockSpec((q, CONV_CH), rmap), pl.BlockSpec((q, LANES), rmap),
                  pl.BlockSpec((1, LANES), const), pl.BlockSpec((1, LANES), const),
                  pl.BlockSpec((q, SSM_INNER), rmap), pl.BlockSpec((q, SSM_INNER), rmap),
                  pl.BlockSpec((1, SSM_INNER), const)],
        out_specs=pl.BlockSpec((q, SSM_INNER), rmap),
        scratch_shapes=[state],
        compiler_params=_params(2),
        name="ssd_backward",
    )(u, dt_raw, dtb, alog, y0, z, ssm_nw)


MOE_TILE = 256


def _out_proj_kernel(xl_ref, xc_ref, a_ref, s_ref, wa_ref, ws_ref, mod_ref, nw_ref, wrh_ref, wrl_ref, br_ref,
                     xo_ref, h_ref, idxt_ref, gate_ref, cnt_ref, *, n_lat_tiles):
    mix = (jnp.dot(a_ref[...], wa_ref[...], preferred_element_type=F32)
           + jnp.dot(s_ref[...], ws_ref[...], preferred_element_type=F32))
    x_in = jnp.where(pl.program_id(0) < n_lat_tiles, xl_ref[...], xc_ref[...])
    x = x_in + mod_ref[0, 2:3, :] * mix
    xo_ref[...] = x
    h = _rms_mod(x, nw_ref[...], mod_ref[0, 3:4, :], mod_ref[0, 4:5, :])
    h_hi = h.astype(BF16)
    h_ref[...] = h_hi
    h_lo = (h - h_hi.astype(F32)).astype(BF16)
    logits = (jnp.dot(h_hi, wrh_ref[...], preferred_element_type=F32)
              + jnp.dot(h_lo, wrh_ref[...], preferred_element_type=F32)
              + jnp.dot(h_hi, wrl_ref[...], preferred_element_type=F32)) + br_ref[...]
    tm = logits.shape[0]
    lane = lax.broadcasted_iota(I32, (tm, LANES), 1)
    lane_f = lane.astype(F32)
    work = jnp.where(lane < N_EXPERTS, logits, -jnp.inf)
    idx_out = jnp.zeros((tm, LANES), F32)
    val_out = jnp.full((tm, LANES), -jnp.inf, F32)
    onehot = jnp.zeros((tm, LANES), F32)
    for kk in range(TOP_K):
        m = jnp.max(work, axis=-1, keepdims=True)
        sel = jnp.min(jnp.where(work == m, lane_f, float(LANES)), axis=-1, keepdims=True)
        idx_out = jnp.where(lane == kk, sel, idx_out)
        val_out = jnp.where(lane == kk, m, val_out)
        picked = lane_f == sel
        onehot = onehot + jnp.where(picked, 1.0, 0.0)
        work = jnp.where(picked, -jnp.inf, work)
    top = jnp.max(val_out, axis=-1, keepdims=True)
    e = jnp.exp(val_out - top)
    gate_ref[...] = e / jnp.sum(e, axis=-1, keepdims=True)
    idxt_ref[...] = idx_out.T[0:SUBLANES, :].astype(I32)
    ones = jnp.ones((SUBLANES, MOE_TILE), BF16)
    oh_b = onehot.astype(BF16)
    for j in range(tm // MOE_TILE):
        cnt_ref[j] = jnp.dot(ones, oh_b[j * MOE_TILE:(j + 1) * MOE_TILE, :],
                             preferred_element_type=F32).astype(I32)


def _out_proj(x_lat, x_ctx, ctx_row0, attn, ssm, w_out_a, w_out_s, mod_l, nfw, wr_hi, wr_lo, br_p,
              *, n_rows, n_lat, seq, batch):
    d = x_lat.shape[1]
    tm = _pick(math.gcd(seq, n_rows - n_lat) if n_rows > n_lat else seq, (512, 256))
    sub = tm // MOE_TILE
    row = lambda i: (i, 0)
    const = lambda i: (0, 0)
    kern = functools.partial(_out_proj_kernel, n_lat_tiles=n_lat // tm)
    return pl.pallas_call(
        kern,
        out_shape=(jax.ShapeDtypeStruct((n_rows, d), F32), jax.ShapeDtypeStruct((n_rows, d), BF16),
                   jax.ShapeDtypeStruct((SUBLANES, n_rows), I32), jax.ShapeDtypeStruct((n_rows, LANES), F32),
                   jax.ShapeDtypeStruct((n_rows // MOE_TILE, SUBLANES, LANES), I32)),
        grid=(n_rows // tm,),
        in_specs=_token_specs(tm, d, n_lat // tm, ctx_row0 // tm) + [
                  pl.BlockSpec((tm, ATTN_WIDTH), row), pl.BlockSpec((tm, SSM_INNER), row),
                  pl.BlockSpec((ATTN_WIDTH, d), const), pl.BlockSpec((SSM_INNER, d), const),
                  pl.BlockSpec((1, N_MOD, d), lambda i: (jnp.minimum(i * tm // seq, batch), 0, 0)),
                  pl.BlockSpec((1, d), const), pl.BlockSpec((d, LANES), const),
                  pl.BlockSpec((d, LANES), const), pl.BlockSpec((1, LANES), const)],
        out_specs=(pl.BlockSpec((tm, d), row), pl.BlockSpec((tm, d), row),
                   pl.BlockSpec((SUBLANES, tm), lambda i: (0, i)), pl.BlockSpec((tm, LANES), row),
                   pl.BlockSpec((sub, SUBLANES, LANES), lambda i: (i, 0, 0))),
        compiler_params=_params(1),
        name="out_proj_router",
    )(x_lat, x_ctx, attn, ssm, w_out_a, w_out_s, mod_l, nfw, wr_hi, wr_lo, br_p)


TAB_CNT, TAB_OFF, TAB_BASE = 0, N_EXPERTS, 2 * N_EXPERTS
ROW_TILE = D_MODEL // LANES
assert ROW_TILE == SUBLANES


def _to_row_tiles(ref, base, val):
    n = val.shape[0]
    for s in range(ROW_TILE):
        ref[pl.ds(base + s, n, stride=ROW_TILE), :] = val[:, s * LANES:(s + 1) * LANES]


def _from_row_tiles(ref, base, n):
    return jnp.concatenate([ref[pl.ds(base + s, n, stride=ROW_TILE), :] for s in range(ROW_TILE)], axis=1)


def _tile_rows(start, size):
    return pl.ds(pl.multiple_of(start * ROW_TILE, ROW_TILE), size * ROW_TILE)


def _for_each_run(tab_ref, lanes, make_copy, fn, *, enabled=None, unrolled=False):
    def per_expert(e, carry=0):
        cnt = tab_ref[0, lanes[0] + e]
        if enabled is not None:
            cnt = jnp.where(enabled, cnt, 0)

        @pl.when(cnt > 0)
        def _():
            fn(make_copy(tab_ref[0, lanes[1] + e], tab_ref[0, lanes[2] + e], cnt))
        return carry

    if unrolled:
        for e in range(N_EXPERTS):
            per_expert(e)
    else:
        lax.fori_loop(0, N_EXPERTS, per_expert, 0)


def _start(copy):
    copy.start()


def _wait(copy):
    copy.wait()


def _dispatch_kernel(tab_ref, tabp_ref, pad_ref, idxt_ref, h_ref, buf_ref, pos_ref,
                     scr_ref, zero_ref, sem, zsem, *, n_tiles):
    i = pl.program_id(0)
    slot = i % 2
    t = h_ref.shape[0]
    rows = TOP_K * t
    run_lanes = (TAB_CNT, TAB_OFF, TAB_BASE)

    def copy_out(s):
        return lambda local, glob, size: pltpu.make_async_copy(
            scr_ref.at[_tile_rows(s * rows + local, size), :], buf_ref.at[_tile_rows(glob, size), :], sem.at[s])

    @pl.when(i == 0)
    def _():
        zero_ref[...] = jnp.zeros_like(zero_ref)
        zero_copy = lambda local, glob, size: pltpu.make_async_copy(
            zero_ref.at[_tile_rows(0, size), :], buf_ref.at[_tile_rows(glob, size), :], zsem)
        pad_lanes = (0, 0, N_EXPERTS)
        _for_each_run(pad_ref, pad_lanes, zero_copy, _start)
        _for_each_run(pad_ref, pad_lanes, zero_copy, _wait)
        tail_start = pad_ref[0, 2 * N_EXPERTS]
        n_tail = pad_ref[0, 2 * N_EXPERTS + 1]
        zrows = zero_ref.shape[0] // ROW_TILE

        def tail(fn):
            def body(j, carry):
                fn(zero_copy(0, tail_start + j * zrows, zrows))
                return carry
            return body

        lax.fori_loop(0, n_tail, tail(_start), 0)
        lax.fori_loop(0, n_tail, tail(_wait), 0)

    e_iota = lax.broadcasted_iota(I32, (N_EXPERTS, t), 0)
    upper = (lax.broadcasted_iota(I32, (t, t), 0) < lax.broadcasted_iota(I32, (t, t), 1)).astype(BF16)
    onehots, counts, before = [], [], []
    for kk in range(TOP_K):
        oh = jnp.where(e_iota == idxt_ref[kk:kk + 1, :], 1.0, 0.0)
        onehots.append(oh)
        counts.append(jnp.sum(oh, axis=1, keepdims=True))
        before.append(jnp.dot(oh.astype(BF16), upper, preferred_element_type=F32))
    total = counts[0] + counts[1] + counts[2] + counts[3]
    below = (lax.broadcasted_iota(I32, (N_EXPERTS, N_EXPERTS), 1)
             < lax.broadcasted_iota(I32, (N_EXPERTS, N_EXPERTS), 0)).astype(BF16)
    start = _dot3(jnp.broadcast_to(total, (N_EXPERTS, LANES)), below, left=True)[:, 0:1]
    pos_rows = []
    for kk in range(TOP_K):
        pos_rows.append(jnp.sum(onehots[kk] * (before[kk] + start), axis=0, keepdims=True))
        start = start + counts[kk]
    r_iota = lax.broadcasted_iota(I32, (rows, t), 0)
    perm = jnp.zeros((rows, t), F32)
    for kk in range(TOP_K):
        perm = jnp.where(r_iota == pos_rows[kk].astype(I32), 1.0, perm)
    _to_row_tiles(scr_ref, slot * rows * ROW_TILE,
                  jnp.dot(perm.astype(BF16), h_ref[...], preferred_element_type=F32))
    pos_t = jnp.concatenate(pos_rows + [jnp.zeros((LANES - TOP_K, t), F32)], axis=0)
    pos_ref[...] = pos_t.T.astype(I32)

    _for_each_run(tab_ref, run_lanes, copy_out(slot), _start, unrolled=True)
    _for_each_run(tabp_ref, run_lanes, copy_out(1 - slot), _wait, enabled=i > 0, unrolled=True)

    @pl.when(i == n_tiles - 1)
    def _():
        _for_each_run(tab_ref, run_lanes, copy_out(slot), _wait)


def _dispatch(h2, idxt, table, padtab, *, n_rows, tme):
    nt, d = h2.shape
    t = MOE_TILE
    n_tiles = nt // t
    smem = lambda imap: pl.BlockSpec((None, 1, LANES), imap, memory_space=pltpu.SMEM)
    kern = functools.partial(_dispatch_kernel, n_tiles=n_tiles)
    return pl.pallas_call(
        kern,
        out_shape=(jax.ShapeDtypeStruct((n_rows * ROW_TILE, LANES), F32),
                   jax.ShapeDtypeStruct((nt, LANES), I32)),
        grid=(n_tiles,),
        in_specs=[smem(lambda i: (i, 0, 0)), smem(lambda i: (jnp.maximum(i - 1, 0), 0, 0)),
                  smem(lambda i: (0, 0, 0)),
                  pl.BlockSpec((SUBLANES, t), lambda i: (0, i)),
                  pl.BlockSpec((t, d), lambda i: (i, 0))],
        out_specs=(pl.BlockSpec(memory_space=pl.ANY), pl.BlockSpec((t, LANES), lambda i: (i, 0))),
        scratch_shapes=[pltpu.VMEM((2 * TOP_K * t * ROW_TILE, LANES), F32),
                        pltpu.VMEM((tme * ROW_TILE, LANES), F32),
                        pltpu.SemaphoreType.DMA((2,)), pltpu.SemaphoreType.DMA],
        compiler_params=_params(1),
        name="moe_dispatch",
    )(table, table, padtab, idxt, h2)


def _expert_kernel(be_ref, nu_ref, x_ref, wgu_ref, bgu_ref, wd_ref, bd_ref, o_ref):
    i = pl.program_id(0)

    tme = x_ref.shape[0] // ROW_TILE

    @pl.when(i < nu_ref[0])
    def _():
        x = _from_row_tiles(x_ref, 0, tme)
        gu = jnp.dot(x.astype(BF16), wgu_ref[...], preferred_element_type=F32) + bgu_ref[...]
        glu = jnp.minimum(gu[:, :D_FF], SWIGLU_LIMIT)
        lin = jnp.clip(gu[:, D_FF:], -SWIGLU_LIMIT, SWIGLU_LIMIT)
        act = glu * _sigmoid(SWIGLU_ALPHA * glu) * (lin + 1.0)
        _to_row_tiles(o_ref, 0, jnp.dot(act.astype(BF16), wd_ref[...], preferred_element_type=F32) + bd_ref[...])

    @pl.when(i >= nu_ref[0])
    def _():
        o_ref[...] = jnp.zeros_like(o_ref)


def _experts(buf, block_e, n_used, wgu, bgu, wd, bd, *, layer, tme):
    d = D_MODEL
    nblk = buf.shape[0] // (tme * ROW_TILE)
    xmap = lambda i, be, nu: (jnp.maximum(jnp.minimum(i, nu[0] - 1), 0), 0)
    emap = lambda i, be, nu: (layer, be[i], 0, 0)
    return pl.pallas_call(
        _expert_kernel,
        out_shape=jax.ShapeDtypeStruct(buf.shape, F32),
        grid_spec=pltpu.PrefetchScalarGridSpec(
            num_scalar_prefetch=2,
            grid=(nblk,),
            in_specs=[pl.BlockSpec((tme * ROW_TILE, LANES), xmap),
                      pl.BlockSpec((None, None, d, 2 * D_FF), emap),
                      pl.BlockSpec((None, None, 1, 2 * D_FF), emap),
                      pl.BlockSpec((None, None, D_FF, d), emap),
                      pl.BlockSpec((None, None, 1, d), emap)],
            out_specs=pl.BlockSpec((tme * ROW_TILE, LANES), lambda i, be, nu: (i, 0))),
        compiler_params=_params(1),
        name="expert_ffn",
    )(block_e, n_used, buf, wgu, bgu, wd, bd)


def _combine_kernel(tab_ref, tabn_ref, pos_ref, gate_ref, x_ref, mod_ref, fw_ref, ob_ref, o_ref,
                    scr_ref, sem, *, n_tiles, final_norm):
    i = pl.program_id(0)
    slot = i % 2
    t = x_ref.shape[0]
    rows = TOP_K * t
    run_lanes = (TAB_CNT, TAB_OFF, TAB_BASE)

    def copy_in(s):
        return lambda local, glob, size: pltpu.make_async_copy(
            ob_ref.at[_tile_rows(glob, size), :], scr_ref.at[_tile_rows(s * rows + local, size), :], sem.at[s])

    @pl.when(i == 0)
    def _():
        _for_each_run(tab_ref, run_lanes, copy_in(slot), _start)

    _for_each_run(tabn_ref, run_lanes, copy_in(1 - slot), _start, enabled=i + 1 < n_tiles, unrolled=True)

    lane = lax.broadcasted_iota(I32, (t, rows), 1)
    pw = jnp.zeros((t, rows), F32)
    for kk in range(TOP_K):
        pw = jnp.where(lane == pos_ref[:, kk:kk + 1], gate_ref[:, kk:kk + 1], pw)
    pw_hi = pw.astype(BF16)
    pw_lo = (pw - pw_hi.astype(F32)).astype(BF16)

    _for_each_run(tab_ref, run_lanes, copy_in(slot), _wait, unrolled=True)
    y = _from_row_tiles(scr_ref, slot * rows * ROW_TILE, rows).astype(BF16)
    f = jnp.dot(pw_hi, y, preferred_element_type=F32) + jnp.dot(pw_lo, y, preferred_element_type=F32)
    x = x_ref[...] + mod_ref[0, 5:6, :] * f
    if final_norm:
        ms = jnp.mean(x * x, axis=-1, keepdims=True)
        x = x * lax.rsqrt(ms + EPS) * fw_ref[...]
    o_ref[...] = x


def _combine(xt, out_buf, pos, gates, table, mod_l, fw, *, seq, batch, final_norm):
    nt, d = xt.shape
    t = MOE_TILE
    n_tiles = nt // t
    row = lambda i: (i, 0)
    smem = lambda imap: pl.BlockSpec((None, 1, LANES), imap, memory_space=pltpu.SMEM)
    kern = functools.partial(_combine_kernel, n_tiles=n_tiles, final_norm=final_norm)
    return pl.pallas_call(
        kern,
        out_shape=jax.ShapeDtypeStruct((nt, d), F32),
        grid=(n_tiles,),
        in_specs=[smem(lambda i: (i, 0, 0)), smem(lambda i: (jnp.minimum(i + 1, n_tiles - 1), 0, 0)),
                  pl.BlockSpec((t, LANES), row), pl.BlockSpec((t, LANES), row), pl.BlockSpec((t, d), row),
                  pl.BlockSpec((1, N_MOD, d), lambda i: (jnp.minimum(i * t // seq, batch), 0, 0)),
                  pl.BlockSpec((1, d), lambda i: (0, 0)),
                  pl.BlockSpec(memory_space=pl.ANY)],
        out_specs=pl.BlockSpec((t, d), row),
        scratch_shapes=[pltpu.VMEM((2 * TOP_K * t * ROW_TILE, LANES), F32), pltpu.SemaphoreType.DMA((2,))],
        compiler_params=_params(1),
        name="moe_combine",
    )(table, table, pos, gates, xt, mod_l, fw, out_buf)


def _route_tables(counts, *, tme, n_blocks):
    cnt = counts[:, 0, :N_EXPERTS]
    total = jnp.sum(cnt, axis=0)
    padded = (total + tme - 1) // tme * tme
    pad_end = jnp.cumsum(padded)
    pad_start = pad_end - padded
    base = pad_start[None, :] + jnp.cumsum(cnt, axis=0) - cnt
    off = jnp.cumsum(cnt, axis=1) - cnt
    table = jnp.concatenate([cnt, off, base, jnp.zeros_like(cnt)], axis=1).astype(I32)[:, None, :]
    tail = jnp.stack([pad_end[-1], n_blocks - pad_end[-1] // tme])
    padtab = jnp.concatenate([padded - total, pad_start + total, tail,
                              jnp.zeros((LANES - 2 * N_EXPERTS - 2,), I32)]).astype(I32)[None, None, :]
    n_used = (pad_end[-1] // tme).astype(I32).reshape(1)
    block_row0 = jnp.arange(n_blocks, dtype=I32) * tme
    block_e = jnp.minimum(jnp.sum((pad_end[None, :] <= block_row0[:, None]).astype(I32), axis=1),
                          N_EXPERTS - 1).astype(I32)
    return table, padtab, block_e, n_used


def _rope_tables(seq):
    rows = seq // GRID_W
    row_pos = jnp.repeat(jnp.arange(rows, dtype=I32), GRID_W).astype(F32)
    col_pos = jnp.tile(jnp.arange(GRID_W, dtype=I32), rows).astype(F32)
    n_freq = HEAD_DIM // 4
    inv_freq = ROPE_BASE ** (-jnp.arange(n_freq, dtype=F32) / n_freq)
    lane = jnp.arange(LANES)
    f = lane % n_freq
    use_col = (lane % HEAD_DIM) >= HEAD_DIM // 2
    ang = jnp.where(use_col[None, :], col_pos[:, None], row_pos[:, None]) * inv_freq[f][None, :]
    first_half = (lane % 32) < 16
    return jnp.cos(ang), jnp.where(first_half[None, :], -jnp.sin(ang), jnp.sin(ang))


def kernel(x, c, ctx, c_ctx, w_ada, b_ada, norm_mix_w, norm_ffn_w, w_in, conv_w, conv_b, dt_bias, a_log, d_skip, ssm_norm_w, attn_sinks, attn_norm_w, w_out, w_router, b_router, w_gate_up, b_gate_up, w_down, b_down, final_norm_w):
    batch, seq, d = x.shape
    n_ctx = ctx.shape[1]
    depth = w_ada.shape[0]
    n_lat = batch * seq
    nt = n_lat + batch * n_ctx
    assert d == D_MODEL and seq % ATT_BLOCK == 0 and n_ctx % ATT_BLOCK == 0 and n_lat % n_ctx == 0

    r_mod = -(-(batch + 1) // SUBLANES) * SUBLANES
    c_all = jnp.zeros((r_mod, d), F32).at[:batch].set(c).at[batch].set(c_ctx)
    mod = _modulation(c_all, w_ada, b_ada)
    cos_t, sin_t = _rope_tables(seq)

    tme = 512
    assert nt % MOE_TILE == 0 and n_lat % MOE_TILE == 0
    hp = SSM_INNER // SSM_HEADS
    wgu_b = w_gate_up.astype(BF16)
    wd_b = w_down.astype(BF16)
    bgu = b_gate_up.reshape(depth, N_EXPERTS, 1, 2 * D_FF)
    bdn = b_down.reshape(depth, N_EXPERTS, 1, d)

    x_lat, x_ctx, ctx_row0 = x.reshape(n_lat, d), ctx.reshape(batch * n_ctx, d), 0
    for l in range(depth):
        last = l == depth - 1
        w_in_p = jnp.pad(w_in[l], ((0, 0), (0, IN_PAD - w_in.shape[2]))).astype(BF16)
        q, k, v, z, xbc, dt_raw = _in_proj(x_lat, x_ctx, ctx_row0, norm_mix_w[l].reshape(1, d), mod[l], w_in_p,
                                           cos_t, sin_t, nt=nt, n_lat=n_lat, seq=seq, batch=batch)
        attn = _attention(q, k, v, attn_sinks[l], attn_norm_w[l].reshape(1, ATTN_WIDTH),
                          batch=batch, seq=seq, n_ctx=n_ctx, with_ctx_queries=not last)
        u = _conv_silu(xbc, conv_w[l], conv_b[l], seq=seq, n_lat=n_lat, n_ctx=n_ctx)
        pad16 = lambda t: jnp.pad(t.reshape(1, N_DIRS * SSM_HEADS), ((0, 0), (0, LANES - N_DIRS * SSM_HEADS)))
        ssm = _ssd(u, dt_raw, z, pad16(dt_bias[l]), pad16(a_log[l]),
                   jnp.repeat(d_skip[l], hp).reshape(1, SSM_INNER), ssm_norm_w[l].reshape(1, SSM_INNER),
                   batch=batch, seq=seq, n_ctx=n_ctx)
        n_rows = n_lat if last else nt
        n_blocks = -(-(n_rows * TOP_K) // tme) + N_EXPERTS
        w_o = w_out[l].astype(BF16)
        wr_p = jnp.pad(w_router[l], ((0, 0), (0, LANES - N_EXPERTS)))
        wr_hi = wr_p.astype(BF16)
        wr_lo = (wr_p - wr_hi.astype(F32)).astype(BF16)
        br_p = jnp.pad(b_router[l].reshape(1, N_EXPERTS), ((0, 0), (0, LANES - N_EXPERTS)))
        xt, h2, idxt, gates, counts = _out_proj(x_lat, x_ctx, ctx_row0, attn, ssm, w_o[:ATTN_WIDTH],
                                                w_o[ATTN_WIDTH:], mod[l], norm_ffn_w[l].reshape(1, d),
                                                wr_hi, wr_lo, br_p,
                                                n_rows=n_rows, n_lat=n_lat, seq=seq, batch=batch)
        table, padtab, block_e, n_used = _route_tables(counts, tme=tme, n_blocks=n_blocks)
        buf, pos = _dispatch(h2, idxt, table, padtab, n_rows=n_blocks * tme, tme=tme)
        out_buf = _experts(buf, block_e, n_used, wgu_b, bgu, wd_b, bdn, layer=l, tme=tme)
        xt = _combine(xt, out_buf, pos, gates, table, mod[l], final_norm_w.reshape(1, d),
                      seq=seq, batch=batch, final_norm=last)
        x_lat, x_ctx, ctx_row0 = xt, xt, n_lat
    return xt.reshape(batch, seq, d)
```

```python
import functools
import math

import jax
import jax.numpy as jnp
from jax import lax
from jax.experimental import pallas as pl
from jax.experimental.pallas import tpu as pltpu

F32 = jnp.float32
BF16 = jnp.bfloat16
I32 = jnp.int32

D_MODEL = 1024
GRID_W = 64
N_MOD = 6
EPS = 1e-6
NEG_INF = -1e30

HEAD_DIM = 64
ATTN_WIDTH = 512
N_HEADS = 8
N_KV_HEADS = 2
KV_WIDTH = 128
ATT_BLOCK = 128
ROPE_BASE = 10000.0

SSM_INNER = 512
SSM_HEADS = 8
SSM_GROUPS = 2
SSM_STATE = 64
CONV_W = 5
CONV_CH = 768
CHUNK = 128
N_DIRS = 2

N_EXPERTS = 32
TOP_K = 4
D_FF = 1024
SWIGLU_LIMIT = 7.0
SWIGLU_ALPHA = 1.702

LANES = 128
SUBLANES = 8
IN_PAD = 2176
VMEM_LIMIT = 56 * 1024 * 1024
HIGHEST = lax.Precision.HIGHEST


def _params(n_axes, vmem=VMEM_LIMIT):
    return pltpu.CompilerParams(dimension_semantics=("arbitrary",) * n_axes, vmem_limit_bytes=vmem)


def _pick(n, prefs):
    for t in prefs:
        if n % t == 0:
            return t
    raise ValueError(f"no tile in {prefs} divides {n}")


def _sigmoid(x):
    return 1.0 / (1.0 + jnp.exp(-x))


def _dot3(x, m_bf16, left=False):
    hi = x.astype(BF16)
    r1 = x - hi.astype(F32)
    mid = r1.astype(BF16)
    lo = (r1 - mid.astype(F32)).astype(BF16)
    mm = (lambda p: jnp.dot(m_bf16, p, preferred_element_type=F32)) if left else (
        lambda p: jnp.dot(p, m_bf16, preferred_element_type=F32))
    return mm(hi) + mm(mid) + mm(lo)


def _mod_kernel(c_ref, w_ref, b_ref, o_ref):
    c = c_ref[...]
    s = c * _sigmoid(c)
    o_ref[...] = jnp.dot(s, w_ref[...], preferred_element_type=F32, precision=HIGHEST) + b_ref[...]


def _modulation(c_all, w_ada, b_ada):
    depth, d, n = w_ada.shape
    r = c_all.shape[0]
    tn = _pick(n, (1536, 1024, 512, 128))
    out = pl.pallas_call(
        _mod_kernel,
        out_shape=jax.ShapeDtypeStruct((depth, r, n), F32),
        grid=(depth, n // tn),
        in_specs=[pl.BlockSpec((r, d), lambda l, j: (0, 0)),
                  pl.BlockSpec((None, d, tn), lambda l, j: (l, 0, j)),
                  pl.BlockSpec((None, 1, tn), lambda l, j: (l, 0, j))],
        out_specs=pl.BlockSpec((None, r, tn), lambda l, j: (l, 0, j)),
        compiler_params=_params(2),
        name="adaln_mod",
    )(c_all, w_ada, b_ada.reshape(depth, 1, n))
    return out.reshape(depth, r, N_MOD, d)


def _rms_mod(x, nw, shift, scale):
    ms = jnp.mean(x * x, axis=-1, keepdims=True)
    y = x * lax.rsqrt(ms + EPS) * nw
    return y * (1.0 + scale) + shift


def _in_proj_kernel(xl_ref, xc_ref, nw_ref, mod_ref, w_ref, cos_ref, sin_ref,
                    q_ref, k_ref, v_ref, z_ref, xbc_ref, dt_ref, *, n_lat_tiles):
    i = pl.program_id(0)
    is_lat = i < n_lat_tiles
    x = jnp.where(is_lat, xl_ref[...], xc_ref[...])
    h = _rms_mod(x, nw_ref[...], mod_ref[0, 0:1, :], mod_ref[0, 1:2, :])
    p = jnp.dot(h.astype(BF16), w_ref[...], preferred_element_type=F32)
    tm = p.shape[0]
    cos = jnp.where(is_lat, cos_ref[...], 1.0)
    sin = jnp.where(is_lat, sin_ref[...], 0.0)
    lane = lax.broadcasted_iota(I32, (tm, LANES), 1)
    first_half = (lane & 31) < 16

    def rope(t):
        partner = jnp.where(first_half, pltpu.roll(t, LANES - 16, 1), pltpu.roll(t, 16, 1))
        return t * cos + partner * sin

    for j in range(ATTN_WIDTH // LANES):
        q_ref[:, j * LANES:(j + 1) * LANES] = rope(p[:, j * LANES:(j + 1) * LANES]).astype(BF16)
    k_ref[...] = rope(p[:, 512:640]).astype(BF16)
    v_ref[...] = p[:, 640:768].T.astype(BF16)
    z_ref[...] = p[:, 768:1280]
    xbc_ref[...] = p[:, 1280:2048]
    dt_ref[...] = p[:, 2048:2176]


def _token_specs(tm, d, n_lat_tiles, ctx_tile0):
    return [pl.BlockSpec((tm, d), lambda i: (jnp.minimum(i, n_lat_tiles - 1), 0)),
            pl.BlockSpec((tm, d), lambda i: (jnp.maximum(i - n_lat_tiles, 0) + ctx_tile0, 0))]


def _in_proj(x_lat, x_ctx, ctx_row0, nw, mod_l, w_in_p, cos_t, sin_t, *, nt, n_lat, seq, batch):
    d = x_lat.shape[1]
    tm = _pick(math.gcd(seq, nt - n_lat), (512, 256, 128))
    n_pos_tiles = seq // tm
    kern = functools.partial(_in_proj_kernel, n_lat_tiles=n_lat // tm)
    row = lambda i: (i, 0)
    return pl.pallas_call(
        kern,
        out_shape=(jax.ShapeDtypeStruct((nt, ATTN_WIDTH), BF16),
                   jax.ShapeDtypeStruct((nt, KV_WIDTH), BF16),
                   jax.ShapeDtypeStruct((KV_WIDTH, nt), BF16),
                   jax.ShapeDtypeStruct((nt, SSM_INNER), F32),
                   jax.ShapeDtypeStruct((nt, CONV_CH), F32),
                   jax.ShapeDtypeStruct((nt, LANES), F32)),
        grid=(nt // tm,),
        in_specs=_token_specs(tm, d, n_lat // tm, ctx_row0 // tm) + [
                  pl.BlockSpec((1, d), lambda i: (0, 0)),
                  pl.BlockSpec((1, N_MOD, d), lambda i: (jnp.minimum(i * tm // seq, batch), 0, 0)),
                  pl.BlockSpec((d, IN_PAD), lambda i: (0, 0)),
                  pl.BlockSpec((tm, LANES), lambda i: (i % n_pos_tiles, 0)),
                  pl.BlockSpec((tm, LANES), lambda i: (i % n_pos_tiles, 0))],
        out_specs=(pl.BlockSpec((tm, ATTN_WIDTH), row), pl.BlockSpec((tm, KV_WIDTH), row),
                   pl.BlockSpec((KV_WIDTH, tm), lambda i: (0, i)), pl.BlockSpec((tm, SSM_INNER), row),
                   pl.BlockSpec((tm, CONV_CH), row), pl.BlockSpec((tm, LANES), row)),
        compiler_params=_params(1),
        name="in_proj",
    )(x_lat, x_ctx, nw, mod_l, w_in_p, cos_t, sin_t)


def _attn_kernel(sink_ref, q_ref, kp_ref, kc_ref, kn_ref, vp_ref, vc_ref, vn_ref, kx_ref, vx_ref,
                 bias_ref, nw_ref, o_ref):
    blk = ATT_BLOCK
    rep = N_HEADS // N_KV_HEADS
    n_ctx = kx_ref.shape[0]
    scale = HEAD_DIM ** -0.5
    q = q_ref[...] * jnp.asarray(scale, BF16)
    bias = jnp.concatenate([bias_ref[...]] * rep, axis=1)
    heads = []
    for g in range(N_KV_HEADS):
        sl = slice(g * HEAD_DIM, (g + 1) * HEAD_DIM)
        qg = jnp.concatenate([q[:, (g * rep + j) * HEAD_DIM:(g * rep + j + 1) * HEAD_DIM]
                              for j in range(rep)], axis=0)
        kg = jnp.concatenate([kx_ref[:, sl], kp_ref[:, sl], kc_ref[:, sl], kn_ref[:, sl]], axis=0)
        vg = jnp.concatenate([vx_ref[sl, :], vp_ref[sl, :], vc_ref[sl, :], vn_ref[sl, :]], axis=1)
        s = lax.dot_general(kg, qg, (((1,), (1,)), ((), ())), preferred_element_type=F32)
        s = jnp.concatenate([s[:n_ctx], s[n_ctx:] + bias], axis=0)
        sink = jnp.concatenate([jnp.full((1, blk), sink_ref[g * rep + j], F32) for j in range(rep)], axis=1)
        m = jnp.maximum(jnp.max(s, axis=0, keepdims=True), sink)
        e = jnp.exp(s - m)
        denom = jnp.sum(e, axis=0, keepdims=True) + jnp.exp(sink - m)
        og = jnp.dot(vg, e.astype(BF16), preferred_element_type=F32) / denom
        heads += [og[:, j * blk:(j + 1) * blk] for j in range(rep)]
    ssq = heads[0] * heads[0]
    for h in heads[1:]:
        ssq = ssq + h * h
    inv = lax.rsqrt(jnp.sum(ssq, axis=0, keepdims=True) * (1.0 / ATTN_WIDTH) + EPS)
    out_t = jnp.concatenate(heads, axis=0) * inv * nw_ref[...]
    o_ref[...] = out_t.T.astype(BF16)


def _window_bias():
    blk = ATT_BLOCK
    j = jnp.arange(blk)[:, None]
    i = jnp.arange(blk)[None, :]
    zero = jnp.zeros((blk, blk), F32)
    hidden = jnp.full((blk, blk), NEG_INF, F32)
    prev = jnp.where(j >= i, 0.0, NEG_INF).astype(F32)
    nxt = jnp.where(j <= i, 0.0, NEG_INF).astype(F32)
    variants = []
    for v in range(4):
        variants.append(jnp.concatenate([hidden if v & 1 else prev, zero, hidden if v & 2 else nxt], axis=0))
    variants.append(jnp.concatenate([hidden, hidden, hidden], axis=0))
    return jnp.stack(variants)


def _attention(q, k, vt, sinks, nw, *, batch, seq, n_ctx, with_ctx_queries):
    nt = q.shape[0]
    blk = ATT_BLOCK
    nb = seq // blk
    ncq = n_ctx // blk
    nq = nb + (ncq if with_ctx_queries else 0)
    ctx_blk0 = (batch * seq) // n_ctx

    def qmap(b, n, s):
        return (jnp.where(n < nb, b * nb + n, batch * nb + b * ncq + (n - nb)), 0)

    def win(off):
        return lambda b, n, s: b * nb + jnp.clip(n + off, 0, nb - 1)

    def bias_map(b, n, s):
        edge = (n == 0).astype(I32) + 2 * (n == nb - 1).astype(I32)
        return (jnp.where(n < nb, edge, 4), 0, 0)

    k_spec = lambda off: pl.BlockSpec((blk, KV_WIDTH), lambda b, n, s: (win(off)(b, n, s), 0))
    v_spec = lambda off: pl.BlockSpec((KV_WIDTH, blk), lambda b, n, s: (0, win(off)(b, n, s)))
    nw_b = jnp.broadcast_to(nw.reshape(ATTN_WIDTH, 1), (ATTN_WIDTH, blk))
    return pl.pallas_call(
        _attn_kernel,
        out_shape=jax.ShapeDtypeStruct((nt if with_ctx_queries else batch * seq, ATTN_WIDTH), BF16),
        grid_spec=pltpu.PrefetchScalarGridSpec(
            num_scalar_prefetch=1,
            grid=(batch, nq),
            in_specs=[pl.BlockSpec((blk, ATTN_WIDTH), qmap),
                      k_spec(-1), k_spec(0), k_spec(1),
                      v_spec(-1), v_spec(0), v_spec(1),
                      pl.BlockSpec((n_ctx, KV_WIDTH), lambda b, n, s: (ctx_blk0 + b, 0)),
                      pl.BlockSpec((KV_WIDTH, n_ctx), lambda b, n, s: (0, ctx_blk0 + b)),
                      pl.BlockSpec((None, 3 * blk, blk), bias_map),
                      pl.BlockSpec((ATTN_WIDTH, blk), lambda b, n, s: (0, 0))],
            out_specs=pl.BlockSpec((blk, ATTN_WIDTH), qmap)),
        compiler_params=_params(2),
        name="attention",
    )(sinks, q, k, k, k, vt, vt, vt, k, vt, _window_bias(), nw_b)


def _conv_kernel(xp_ref, xc_ref, xn_ref, w_ref, b_ref, o_ref, ext_ref, *, seq, n_lat, n_ctx):
    i = pl.program_id(0)
    tb = xc_ref.shape[0]
    row0 = i * tb
    in_lat = row0 < n_lat
    local = jnp.where(in_lat, row0 % seq, (row0 - n_lat) % n_ctx)
    length = jnp.where(in_lat, seq, n_ctx)
    first = local == 0
    last = local + tb == length
    h = SUBLANES
    ext_ref[0:h, :] = jnp.where(first, 0.0, xp_ref[...])
    ext_ref[h:h + tb, :] = xc_ref[...]
    ext_ref[h + tb:h + tb + h, :] = jnp.where(last, 0.0, xn_ref[...])
    acc = jnp.zeros((tb, CONV_CH), F32) + b_ref[...]
    for kk in range(CONV_W):
        off = h - CONV_W // 2 + kk
        acc = acc + ext_ref[off:off + tb, :] * w_ref[kk:kk + 1, :]
    o_ref[...] = acc * _sigmoid(acc)


def _conv_silu(xbc, conv_w, conv_b, *, seq, n_lat, n_ctx):
    nt = xbc.shape[0]
    tb = _pick(math.gcd(seq, n_ctx), (256, 128))
    h = SUBLANES
    per = tb // h
    n_h = nt // h
    kern = functools.partial(_conv_kernel, seq=seq, n_lat=n_lat, n_ctx=n_ctx)
    return pl.pallas_call(
        kern,
        out_shape=jax.ShapeDtypeStruct((nt, CONV_CH), F32),
        grid=(nt // tb,),
        in_specs=[pl.BlockSpec((h, CONV_CH), lambda i: (jnp.maximum(i * per - 1, 0), 0)),
                  pl.BlockSpec((tb, CONV_CH), lambda i: (i, 0)),
                  pl.BlockSpec((h, CONV_CH), lambda i: (jnp.minimum((i + 1) * per, n_h - 1), 0)),
                  pl.BlockSpec((CONV_W, CONV_CH), lambda i: (0, 0)),
                  pl.BlockSpec((1, CONV_CH), lambda i: (0, 0))],
        out_specs=pl.BlockSpec((tb, CONV_CH), lambda i: (i, 0)),
        scratch_shapes=[pltpu.VMEM((tb + 2 * h, CONV_CH), F32)],
        compiler_params=_params(1),
        name="conv_silu",
    )(xbc, xbc, xbc, conv_w, conv_b.reshape(1, CONV_CH))


def _ssd_chunk(u, dtraw, dtb, alog, state_ref, *, direction):
    q = CHUNK
    hp = SSM_INNER // SSM_HEADS
    per_g = SSM_HEADS // SSM_GROUPS
    gw = per_g * hp
    xs = u[:, :SSM_INNER]
    bm = u[:, SSM_INNER:SSM_INNER + SSM_GROUPS * SSM_STATE]
    cm = u[:, SSM_INNER + SSM_GROUPS * SSM_STATE:]

    xv = dtraw + dtb
    dt = jnp.maximum(xv, 0.0) + jnp.log1p(jnp.exp(-jnp.abs(xv)))
    dta = dt * (-jnp.exp(alog))

    ri = lax.broadcasted_iota(I32, (q, q), 0)
    ci = lax.broadcasted_iota(I32, (q, q), 1)
    tri = (ci <= ri) if direction == 0 else (ci >= ri)
    cs = _dot3(dta, tri.astype(BF16), left=True)
    cs_t = cs.T

    er = lax.broadcasted_iota(I32, (LANES, SSM_INNER), 0)
    ec = lax.broadcasted_iota(I32, (LANES, SSM_INNER), 1)
    expand = (er == direction * SSM_HEADS + jnp.right_shift(ec, hp.bit_length() - 1)).astype(BF16)
    cs_e = _dot3(cs, expand)
    dt_e = _dot3(dt, expand)
    last = q - 1 if direction == 0 else 0
    cs_last = cs_e[last:last + 1, :]

    xdt = (xs * dt_e).astype(BF16)
    xw = (xs * (jnp.exp(cs_last - cs_e) * dt_e)).astype(BF16)
    bm_t = bm.T.astype(BF16)
    cmb = cm.astype(BF16)
    bmb = bm.astype(BF16)
    state = state_ref[...]
    state_b = state.astype(BF16)

    y_diag = []
    y_off = []
    new_states = []
    for g in range(SSM_GROUPS):
        gs = slice(g * SSM_STATE, (g + 1) * SSM_STATE)
        cb = lax.dot_general(cmb[:, gs], bmb[:, gs], (((1,), (1,)), ((), ())), preferred_element_type=F32)
        y_off.append(jnp.dot(cmb[:, gs], state_b[:, g * gw:(g + 1) * gw], preferred_element_type=F32))
        new_states.append(jnp.dot(bm_t[gs, :], xw[:, g * gw:(g + 1) * gw], preferred_element_type=F32))
        for j in range(per_g):
            hh = g * per_g + j
            col = direction * SSM_HEADS + hh
            seg = cs[:, col:col + 1] - cs_t[col:col + 1, :]
            decay = jnp.exp(jnp.where(tri, seg, NEG_INF))
            scores = (cb * decay).astype(BF16)
            y_diag.append(jnp.dot(scores, xdt[:, hh * hp:(hh + 1) * hp], preferred_element_type=F32))
    y = jnp.concatenate(y_diag, axis=1) + jnp.exp(cs_e) * jnp.concatenate(y_off, axis=1)
    state_ref[...] = jnp.exp(cs_last) * state + jnp.concatenate(new_states, axis=1)
    return y


SSD_CHUNKS_PER_STEP = 2


def _ssd_fwd_kernel(u_ref, dt_ref, dtb_ref, alog_ref, skip_ref, y_ref, state_ref):
    @pl.when(pl.program_id(1) == 0)
    def _():
        state_ref[...] = jnp.zeros_like(state_ref)

    for j in range(u_ref.shape[0] // CHUNK):
        rows = slice(j * CHUNK, (j + 1) * CHUNK)
        u = u_ref[rows, :]
        y = _ssd_chunk(u, dt_ref[rows, :], dtb_ref[...], alog_ref[...], state_ref, direction=0)
        y_ref[rows, :] = y + skip_ref[...] * u[:, :SSM_INNER]


def _ssd_bwd_kernel(u_ref, dt_ref, dtb_ref, alog_ref, y0_ref, z_ref, nw_ref, o_ref, state_ref):
    @pl.when(pl.program_id(1) == 0)
    def _():
        state_ref[...] = jnp.zeros_like(state_ref)

    n = u_ref.shape[0] // CHUNK
    for j in range(n - 1, -1, -1):
        rows = slice(j * CHUNK, (j + 1) * CHUNK)
        y = y0_ref[rows, :] + _ssd_chunk(u_ref[rows, :], dt_ref[rows, :], dtb_ref[...], alog_ref[...],
                                         state_ref, direction=1)
        z = z_ref[rows, :]
        gt = y * (z * _sigmoid(z))
        gw = SSM_INNER // SSM_GROUPS
        outs = []
        for g in range(SSM_GROUPS):
            gg = gt[:, g * gw:(g + 1) * gw]
            ms = jnp.mean(gg * gg, axis=-1, keepdims=True)
            outs.append(gg * lax.rsqrt(ms + EPS))
        o_ref[rows, :] = (jnp.concatenate(outs, axis=1) * nw_ref[...]).astype(BF16)


def _ssd(u, dt_raw, z, dtb, alog, skip, ssm_nw, *, batch, seq, n_ctx):
    nt = u.shape[0]
    cps = SSD_CHUNKS_PER_STEP
    q = cps * CHUNK
    assert seq % q == 0 and n_ctx % q == 0
    ncl = seq // q
    ncc = n_ctx // q
    steps = ncc + ncl
    ctx0 = (batch * seq) // q

    def fmap(b, t):
        return (jnp.where(t < ncc, ctx0 + b * ncc + t, b * ncl + (t - ncc)), 0)

    def rmap(b, t):
        return (jnp.where(t < ncc, ctx0 + b * ncc + (ncc - 1 - t), b * ncl + (ncl - 1 - (t - ncc))), 0)

    const = lambda b, t: (0, 0)
    state = pltpu.VMEM((SSM_STATE, SSM_INNER), F32)
    y0 = pl.pallas_call(
        _ssd_fwd_kernel,
        out_shape=jax.ShapeDtypeStruct((nt, SSM_INNER), F32),
        grid=(batch, steps),
        in_specs=[pl.BlockSpec((q, CONV_CH), fmap), pl.BlockSpec((q, LANES), fmap),
                  pl.BlockSpec((1, LANES), const), pl.BlockSpec((1, LANES), const),
                  pl.BlockSpec((1, SSM_INNER), const)],
        out_specs=pl.BlockSpec((q, SSM_INNER), fmap),
        scratch_shapes=[state],
        compiler_params=_params(2),
        name="ssd_forward",
    )(u, dt_raw, dtb, alog, skip)
    return pl.pallas_call(
        _ssd_bwd_kernel,
        out_shape=jax.ShapeDtypeStruct((nt, SSM_INNER), BF16),
        grid=(batch, steps),
        in_specs=[pl.BlockSpec((q, CONV_CH), rmap), pl.BlockSpec((q, LANES), rmap),
                  pl.BlockSpec((1, LANES), const), pl.BlockSpec((1, LANES), const),
                  pl.BlockSpec((q, SSM_INNER), rmap), pl.BlockSpec((q, SSM_INNER), rmap),
                  pl.BlockSpec((1, SSM_INNER), const)],
        out_specs=pl.BlockSpec((q, SSM_INNER), rmap),
        scratch_shapes=[state],
        compiler_params=_params(2),
        name="ssd_backward",
    )(u, dt_raw, dtb, alog, y0, z, ssm_nw)


MOE_TILE = 256


def _out_proj_kernel(xl_ref, xc_ref, a_ref, s_ref, wa_ref, ws_ref, mod_ref, nw_ref, wrh_ref, wrl_ref, br_ref,
                     xo_ref, h_ref, idxt_ref, gate_ref, cnt_ref, *, n_lat_tiles):
    mix = (jnp.dot(a_ref[...], wa_ref[...], preferred_element_type=F32)
           + jnp.dot(s_ref[...], ws_ref[...], preferred_element_type=F32))
    x_in = jnp.where(pl.program_id(0) < n_lat_tiles, xl_ref[...], xc_ref[...])
    x = x_in + mod_ref[0, 2:3, :] * mix
    xo_ref[...] = x
    h = _rms_mod(x, nw_ref[...], mod_ref[0, 3:4, :], mod_ref[0, 4:5, :])
    h_hi = h.astype(BF16)
    h_ref[...] = h_hi
    h_lo = (h - h_hi.astype(F32)).astype(BF16)
    logits = (jnp.dot(h_hi, wrh_ref[...], preferred_element_type=F32)
              + jnp.dot(h_lo, wrh_ref[...], preferred_element_type=F32)
              + jnp.dot(h_hi, wrl_ref[...], preferred_element_type=F32)) + br_ref[...]
    tm = logits.shape[0]
    lane = lax.broadcasted_iota(I32, (tm, LANES), 1)
    lane_f = lane.astype(F32)
    work = jnp.where(lane < N_EXPERTS, logits, -jnp.inf)
    idx_out = jnp.zeros((tm, LANES), F32)
    val_out = jnp.full((tm, LANES), -jnp.inf, F32)
    onehot = jnp.zeros((tm, LANES), F32)
    for kk in range(TOP_K):
        m = jnp.max(work, axis=-1, keepdims=True)
        sel = jnp.min(jnp.where(work == m, lane_f, float(LANES)), axis=-1, keepdims=True)
        idx_out = jnp.where(lane == kk, sel, idx_out)
        val_out = jnp.where(lane == kk, m, val_out)
        picked = lane_f == sel
        onehot = onehot + jnp.where(picked, 1.0, 0.0)
        work = jnp.where(picked, -jnp.inf, work)
    top = jnp.max(val_out, axis=-1, keepdims=True)
    e = jnp.exp(val_out - top)
    gate_ref[...] = e / jnp.sum(e, axis=-1, keepdims=True)
    idxt_ref[...] = idx_out.T[0:SUBLANES, :].astype(I32)
    ones = jnp.ones((SUBLANES, MOE_TILE), BF16)
    oh_b = onehot.astype(BF16)
    for j in range(tm // MOE_TILE):
        cnt_ref[j] = jnp.dot(ones, oh_b[j * MOE_TILE:(j + 1) * MOE_TILE, :],
                             preferred_element_type=F32).astype(I32)


def _out_proj(x_lat, x_ctx, ctx_row0, attn, ssm, w_out_a, w_out_s, mod_l, nfw, wr_hi, wr_lo, br_p,
              *, n_rows, n_lat, seq, batch):
    d = x_lat.shape[1]
    tm = _pick(math.gcd(seq, n_rows - n_lat) if n_rows > n_lat else seq, (512, 256))
    sub = tm // MOE_TILE
    row = lambda i: (i, 0)
    const = lambda i: (0, 0)
    kern = functools.partial(_out_proj_kernel, n_lat_tiles=n_lat // tm)
    return pl.pallas_call(
        kern,
        out_shape=(jax.ShapeDtypeStruct((n_rows, d), F32), jax.ShapeDtypeStruct((n_rows, d), BF16),
                   jax.ShapeDtypeStruct((SUBLANES, n_rows), I32), jax.ShapeDtypeStruct((n_rows, LANES), F32),
                   jax.ShapeDtypeStruct((n_rows // MOE_TILE, SUBLANES, LANES), I32)),
        grid=(n_rows // tm,),
        in_specs=_token_specs(tm, d, n_lat // tm, ctx_row0 // tm) + [
                  pl.BlockSpec((tm, ATTN_WIDTH), row), pl.BlockSpec((tm, SSM_INNER), row),
                  pl.BlockSpec((ATTN_WIDTH, d), const), pl.BlockSpec((SSM_INNER, d), const),
                  pl.BlockSpec((1, N_MOD, d), lambda i: (jnp.minimum(i * tm // seq, batch), 0, 0)),
                  pl.BlockSpec((1, d), const), pl.BlockSpec((d, LANES), const),
                  pl.BlockSpec((d, LANES), const), pl.BlockSpec((1, LANES), const)],
        out_specs=(pl.BlockSpec((tm, d), row), pl.BlockSpec((tm, d), row),
                   pl.BlockSpec((SUBLANES, tm), lambda i: (0, i)), pl.BlockSpec((tm, LANES), row),
                   pl.BlockSpec((sub, SUBLANES, LANES), lambda i: (i, 0, 0))),
        compiler_params=_params(1),
        name="out_proj_router",
    )(x_lat, x_ctx, attn, ssm, w_out_a, w_out_s, mod_l, nfw, wr_hi, wr_lo, br_p)


TAB_CNT, TAB_OFF, TAB_BASE = 0, N_EXPERTS, 2 * N_EXPERTS
ROW_TILE = D_MODEL // LANES
assert ROW_TILE == SUBLANES


def _to_row_tiles(ref, base, val):
    n = val.shape[0]
    for s in range(ROW_TILE):
        ref[pl.ds(base + s, n, stride=ROW_TILE), :] = val[:, s * LANES:(s + 1) * LANES]


def _from_row_tiles(ref, base, n):
    return jnp.concatenate([ref[pl.ds(base + s, n, stride=ROW_TILE), :] for s in range(ROW_TILE)], axis=1)


def _tile_rows(start, size):
    return pl.ds(pl.multiple_of(start * ROW_TILE, ROW_TILE), size * ROW_TILE)


def _for_each_run(tab_ref, lanes, make_copy, fn, *, enabled=None, unrolled=False):
    def per_expert(e, carry=0):
        cnt = tab_ref[0, lanes[0] + e]
        if enabled is not None:
            cnt = jnp.where(enabled, cnt, 0)

        @pl.when(cnt > 0)
        def _():
            fn(make_copy(tab_ref[0, lanes[1] + e], tab_ref[0, lanes[2] + e], cnt))
        return carry

    if unrolled:
        for e in range(N_EXPERTS):
            per_expert(e)
    else:
        lax.fori_loop(0, N_EXPERTS, per_expert, 0)


def _start(copy):
    copy.start()


def _wait(copy):
    copy.wait()


def _dispatch_kernel(tab_ref, tabp_ref, pad_ref, idxt_ref, h_ref, buf_ref, pos_ref,
                     scr_ref, zero_ref, sem, zsem, *, n_tiles):
    i = pl.program_id(0)
    slot = i % 2
    t = h_ref.shape[0]
    rows = TOP_K * t
    run_lanes = (TAB_CNT, TAB_OFF, TAB_BASE)

    def copy_out(s):
        return lambda local, glob, size: pltpu.make_async_copy(
            scr_ref.at[_tile_rows(s * rows + local, size), :], buf_ref.at[_tile_rows(glob, size), :], sem.at[s])

    @pl.when(i == 0)
    def _():
        zero_ref[...] = jnp.zeros_like(zero_ref)
        zero_copy = lambda local, glob, size: pltpu.make_async_copy(
            zero_ref.at[_tile_rows(0, size), :], buf_ref.at[_tile_rows(glob, size), :], zsem)
        pad_lanes = (0, 0, N_EXPERTS)
        _for_each_run(pad_ref, pad_lanes, zero_copy, _start)
        _for_each_run(pad_ref, pad_lanes, zero_copy, _wait)
        tail_start = pad_ref[0, 2 * N_EXPERTS]
        n_tail = pad_ref[0, 2 * N_EXPERTS + 1]
        zrows = zero_ref.shape[0] // ROW_TILE

        def tail(fn):
            def body(j, carry):
                fn(zero_copy(0, tail_start + j * zrows, zrows))
                return carry
            return body

        lax.fori_loop(0, n_tail, tail(_start), 0)
        lax.fori_loop(0, n_tail, tail(_wait), 0)

    e_iota = lax.broadcasted_iota(I32, (N_EXPERTS, t), 0)
    upper = (lax.broadcasted_iota(I32, (t, t), 0) < lax.broadcasted_iota(I32, (t, t), 1)).astype(BF16)
    onehots, counts, before = [], [], []
    for kk in range(TOP_K):
        oh = jnp.where(e_iota == idxt_ref[kk:kk + 1, :], 1.0, 0.0)
        onehots.append(oh)
        counts.append(jnp.sum(oh, axis=1, keepdims=True))
        before.append(jnp.dot(oh.astype(BF16), upper, preferred_element_type=F32))
    total = counts[0] + counts[1] + counts[2] + counts[3]
    below = (lax.broadcasted_iota(I32, (N_EXPERTS, N_EXPERTS), 1)
             < lax.broadcasted_iota(I32, (N_EXPERTS, N_EXPERTS), 0)).astype(BF16)
    start = _dot3(jnp.broadcast_to(total, (N_EXPERTS, LANES)), below, left=True)[:, 0:1]
    pos_rows = []
    for kk in range(TOP_K):
        pos_rows.append(jnp.sum(onehots[kk] * (before[kk] + start), axis=0, keepdims=True))
        start = start + counts[kk]
    r_iota = lax.broadcasted_iota(I32, (rows, t), 0)
    perm = jnp.zeros((rows, t), F32)
    for kk in range(TOP_K):
        perm = jnp.where(r_iota == pos_rows[kk].astype(I32), 1.0, perm)
    _to_row_tiles(scr_ref, slot * rows * ROW_TILE,
                  jnp.dot(perm.astype(BF16), h_ref[...], preferred_element_type=F32))
    pos_t = jnp.concatenate(pos_rows + [jnp.zeros((LANES - TOP_K, t), F32)], axis=0)
    pos_ref[...] = pos_t.T.astype(I32)

    _for_each_run(tab_ref, run_lanes, copy_out(slot), _start, unrolled=True)
    _for_each_run(tabp_ref, run_lanes, copy_out(1 - slot), _wait, enabled=i > 0, unrolled=True)

    @pl.when(i == n_tiles - 1)
    def _():
        _for_each_run(tab_ref, run_lanes, copy_out(slot), _wait)


def _dispatch(h2, idxt, table, padtab, *, n_rows, tme):
    nt, d = h2.shape
    t = MOE_TILE
    n_tiles = nt // t
    smem = lambda imap: pl.BlockSpec((None, 1, LANES), imap, memory_space=pltpu.SMEM)
    kern = functools.partial(_dispatch_kernel, n_tiles=n_tiles)
    return pl.pallas_call(
        kern,
        out_shape=(jax.ShapeDtypeStruct((n_rows * ROW_TILE, LANES), F32),
                   jax.ShapeDtypeStruct((nt, LANES), I32)),
        grid=(n_tiles,),
        in_specs=[smem(lambda i: (i, 0, 0)), smem(lambda i: (jnp.maximum(i - 1, 0), 0, 0)),
                  smem(lambda i: (0, 0, 0)),
                  pl.BlockSpec((SUBLANES, t), lambda i: (0, i)),
                  pl.BlockSpec((t, d), lambda i: (i, 0))],
        out_specs=(pl.BlockSpec(memory_space=pl.ANY), pl.BlockSpec((t, LANES), lambda i: (i, 0))),
        scratch_shapes=[pltpu.VMEM((2 * TOP_K * t * ROW_TILE, LANES), F32),
                        pltpu.VMEM((tme * ROW_TILE, LANES), F32),
                        pltpu.SemaphoreType.DMA((2,)), pltpu.SemaphoreType.DMA],
        compiler_params=_params(1),
        name="moe_dispatch",
    )(table, table, padtab, idxt, h2)


def _expert_kernel(be_ref, nu_ref, x_ref, wgu_ref, bgu_ref, wd_ref, bd_ref, o_ref, wgu_b, wd_b):
    i = pl.program_id(0)

    tme = x_ref.shape[0] // ROW_TILE

    @pl.when(i < nu_ref[0])
    def _():
        @pl.when((i == 0) | (be_ref[i] != be_ref[jnp.maximum(i - 1, 0)]))
        def _():
            wgu_b[...] = wgu_ref[...].astype(BF16)
            wd_b[...] = wd_ref[...].astype(BF16)

        x = _from_row_tiles(x_ref, 0, tme)
        gu = jnp.dot(x.astype(BF16), wgu_b[...], preferred_element_type=F32) + bgu_ref[...]
        glu = jnp.minimum(gu[:, :D_FF], SWIGLU_LIMIT)
        lin = jnp.clip(gu[:, D_FF:], -SWIGLU_LIMIT, SWIGLU_LIMIT)
        act = glu * _sigmoid(SWIGLU_ALPHA * glu) * (lin + 1.0)
        _to_row_tiles(o_ref, 0, jnp.dot(act.astype(BF16), wd_b[...], preferred_element_type=F32) + bd_ref[...])

    @pl.when(i >= nu_ref[0])
    def _():
        o_ref[...] = jnp.zeros_like(o_ref)


def _experts(buf, block_e, n_used, wgu, bgu, wd, bd, *, layer, tme):
    d = D_MODEL
    nblk = buf.shape[0] // (tme * ROW_TILE)
    xmap = lambda i, be, nu: (jnp.maximum(jnp.minimum(i, nu[0] - 1), 0), 0)
    emap = lambda i, be, nu: (layer, be[i], 0, 0)
    return pl.pallas_call(
        _expert_kernel,
        out_shape=jax.ShapeDtypeStruct(buf.shape, F32),
        grid_spec=pltpu.PrefetchScalarGridSpec(
            num_scalar_prefetch=2,
            grid=(nblk,),
            in_specs=[pl.BlockSpec((tme * ROW_TILE, LANES), xmap),
                      pl.BlockSpec((None, None, d, 2 * D_FF), emap),
                      pl.BlockSpec((None, None, 1, 2 * D_FF), emap),
                      pl.BlockSpec((None, None, D_FF, d), emap),
                      pl.BlockSpec((None, None, 1, d), emap)],
            out_specs=pl.BlockSpec((tme * ROW_TILE, LANES), lambda i, be, nu: (i, 0)),
            scratch_shapes=[pltpu.VMEM((d, 2 * D_FF), BF16), pltpu.VMEM((D_FF, d), BF16)]),
        compiler_params=_params(1),
        name="expert_ffn",
    )(block_e, n_used, buf, wgu, bgu, wd, bd)


def _combine_kernel(tab_ref, tabn_ref, pos_ref, gate_ref, x_ref, mod_ref, fw_ref, ob_ref, o_ref,
                    scr_ref, sem, *, n_tiles, final_norm):
    i = pl.program_id(0)
    slot = i % 2
    t = x_ref.shape[0]
    rows = TOP_K * t
    run_lanes = (TAB_CNT, TAB_OFF, TAB_BASE)

    def copy_in(s):
        return lambda local, glob, size: pltpu.make_async_copy(
            ob_ref.at[_tile_rows(glob, size), :], scr_ref.at[_tile_rows(s * rows + local, size), :], sem.at[s])

    @pl.when(i == 0)
    def _():
        _for_each_run(tab_ref, run_lanes, copy_in(slot), _start)

    _for_each_run(tabn_ref, run_lanes, copy_in(1 - slot), _start, enabled=i + 1 < n_tiles, unrolled=True)

    lane = lax.broadcasted_iota(I32, (t, rows), 1)
    pw = jnp.zeros((t, rows), F32)
    for kk in range(TOP_K):
        pw = jnp.where(lane == pos_ref[:, kk:kk + 1], gate_ref[:, kk:kk + 1], pw)
    pw_hi = pw.astype(BF16)
    pw_lo = (pw - pw_hi.astype(F32)).astype(BF16)

    _for_each_run(tab_ref, run_lanes, copy_in(slot), _wait, unrolled=True)
    y = _from_row_tiles(scr_ref, slot * rows * ROW_TILE, rows).astype(BF16)
    f = jnp.dot(pw_hi, y, preferred_element_type=F32) + jnp.dot(pw_lo, y, preferred_element_type=F32)
    x = x_ref[...] + mod_ref[0, 5:6, :] * f
    if final_norm:
        ms = jnp.mean(x * x, axis=-1, keepdims=True)
        x = x * lax.rsqrt(ms + EPS) * fw_ref[...]
    o_ref[...] = x


def _combine(xt, out_buf, pos, gates, table, mod_l, fw, *, seq, batch, final_norm):
    nt, d = xt.shape
    t = MOE_TILE
    n_tiles = nt // t
    row = lambda i: (i, 0)
    smem = lambda imap: pl.BlockSpec((None, 1, LANES), imap, memory_space=pltpu.SMEM)
    kern = functools.partial(_combine_kernel, n_tiles=n_tiles, final_norm=final_norm)
    return pl.pallas_call(
        kern,
        out_shape=jax.ShapeDtypeStruct((nt, d), F32),
        grid=(n_tiles,),
        in_specs=[smem(lambda i: (i, 0, 0)), smem(lambda i: (jnp.minimum(i + 1, n_tiles - 1), 0, 0)),
                  pl.BlockSpec((t, LANES), row), pl.BlockSpec((t, LANES), row), pl.BlockSpec((t, d), row),
                  pl.BlockSpec((1, N_MOD, d), lambda i: (jnp.minimum(i * t // seq, batch), 0, 0)),
                  pl.BlockSpec((1, d), lambda i: (0, 0)),
                  pl.BlockSpec(memory_space=pl.ANY)],
        out_specs=pl.BlockSpec((t, d), row),
        scratch_shapes=[pltpu.VMEM((2 * TOP_K * t * ROW_TILE, LANES), F32), pltpu.SemaphoreType.DMA((2,))],
        compiler_params=_params(1),
        name="moe_combine",
    )(table, table, pos, gates, xt, mod_l, fw, out_buf)


def _route_tables(counts, *, tme, n_blocks):
    cnt = counts[:, 0, :N_EXPERTS]
    total = jnp.sum(cnt, axis=0)
    padded = (total + tme - 1) // tme * tme
    pad_end = jnp.cumsum(padded)
    pad_start = pad_end - padded
    base = pad_start[None, :] + jnp.cumsum(cnt, axis=0) - cnt
    off = jnp.cumsum(cnt, axis=1) - cnt
    table = jnp.concatenate([cnt, off, base, jnp.zeros_like(cnt)], axis=1).astype(I32)[:, None, :]
    tail = jnp.stack([pad_end[-1], n_blocks - pad_end[-1] // tme])
    padtab = jnp.concatenate([padded - total, pad_start + total, tail,
                              jnp.zeros((LANES - 2 * N_EXPERTS - 2,), I32)]).astype(I32)[None, None, :]
    n_used = (pad_end[-1] // tme).astype(I32).reshape(1)
    block_row0 = jnp.arange(n_blocks, dtype=I32) * tme
    block_e = jnp.minimum(jnp.sum((pad_end[None, :] <= block_row0[:, None]).astype(I32), axis=1),
                          N_EXPERTS - 1).astype(I32)
    return table, padtab, block_e, n_used


def _rope_tables(seq):
    rows = seq // GRID_W
    row_pos = jnp.repeat(jnp.arange(rows, dtype=I32), GRID_W).astype(F32)
    col_pos = jnp.tile(jnp.arange(GRID_W, dtype=I32), rows).astype(F32)
    n_freq = HEAD_DIM // 4
    inv_freq = ROPE_BASE ** (-jnp.arange(n_freq, dtype=F32) / n_freq)
    lane = jnp.arange(LANES)
    f = lane % n_freq
    use_col = (lane % HEAD_DIM) >= HEAD_DIM // 2
    ang = jnp.where(use_col[None, :], col_pos[:, None], row_pos[:, None]) * inv_freq[f][None, :]
    first_half = (lane % 32) < 16
    return jnp.cos(ang), jnp.where(first_half[None, :], -jnp.sin(ang), jnp.sin(ang))


def kernel(x, c, ctx, c_ctx, w_ada, b_ada, norm_mix_w, norm_ffn_w, w_in, conv_w, conv_b, dt_bias, a_log, d_skip, ssm_norm_w, attn_sinks, attn_norm_w, w_out, w_router, b_router, w_gate_up, b_gate_up, w_down, b_down, final_norm_w):
    batch, seq, d = x.shape
    n_ctx = ctx.shape[1]
    depth = w_ada.shape[0]
    n_lat = batch * seq
    nt = n_lat + batch * n_ctx
    assert d == D_MODEL and seq % ATT_BLOCK == 0 and n_ctx % ATT_BLOCK == 0 and n_lat % n_ctx == 0

    r_mod = -(-(batch + 1) // SUBLANES) * SUBLANES
    c_all = jnp.zeros((r_mod, d), F32).at[:batch].set(c).at[batch].set(c_ctx)
    mod = _modulation(c_all, w_ada, b_ada)
    cos_t, sin_t = _rope_tables(seq)

    tme = 512
    assert nt % MOE_TILE == 0 and n_lat % MOE_TILE == 0
    hp = SSM_INNER // SSM_HEADS
    bgu = b_gate_up.reshape(depth, N_EXPERTS, 1, 2 * D_FF)
    bdn = b_down.reshape(depth, N_EXPERTS, 1, d)

    x_lat, x_ctx, ctx_row0 = x.reshape(n_lat, d), ctx.reshape(batch * n_ctx, d), 0
    for l in range(depth):
        last = l == depth - 1
        w_in_p = jnp.pad(w_in[l], ((0, 0), (0, IN_PAD - w_in.shape[2]))).astype(BF16)
        q, k, v, z, xbc, dt_raw = _in_proj(x_lat, x_ctx, ctx_row0, norm_mix_w[l].reshape(1, d), mod[l], w_in_p,
                                           cos_t, sin_t, nt=nt, n_lat=n_lat, seq=seq, batch=batch)
        attn = _attention(q, k, v, attn_sinks[l], attn_norm_w[l].reshape(1, ATTN_WIDTH),
                          batch=batch, seq=seq, n_ctx=n_ctx, with_ctx_queries=not last)
        u = _conv_silu(xbc, conv_w[l], conv_b[l], seq=seq, n_lat=n_lat, n_ctx=n_ctx)
        pad16 = lambda t: jnp.pad(t.reshape(1, N_DIRS * SSM_HEADS), ((0, 0), (0, LANES - N_DIRS * SSM_HEADS)))
        ssm = _ssd(u, dt_raw, z, pad16(dt_bias[l]), pad16(a_log[l]),
                   jnp.repeat(d_skip[l], hp).reshape(1, SSM_INNER), ssm_norm_w[l].reshape(1, SSM_INNER),
                   batch=batch, seq=seq, n_ctx=n_ctx)
        n_rows = n_lat if last else nt
        n_blocks = -(-(n_rows * TOP_K) // tme) + N_EXPERTS
        w_o = w_out[l].astype(BF16)
        wr_p = jnp.pad(w_router[l], ((0, 0), (0, LANES - N_EXPERTS)))
        wr_hi = wr_p.astype(BF16)
        wr_lo = (wr_p - wr_hi.astype(F32)).astype(BF16)
        br_p = jnp.pad(b_router[l].reshape(1, N_EXPERTS), ((0, 0), (0, LANES - N_EXPERTS)))
        xt, h2, idxt, gates, counts = _out_proj(x_lat, x_ctx, ctx_row0, attn, ssm, w_o[:ATTN_WIDTH],
                                                w_o[ATTN_WIDTH:], mod[l], norm_ffn_w[l].reshape(1, d),
                                                wr_hi, wr_lo, br_p,
                                                n_rows=n_rows, n_lat=n_lat, seq=seq, batch=batch)
        table, padtab, block_e, n_used = _route_tables(counts, tme=tme, n_blocks=n_blocks)
        buf, pos = _dispatch(h2, idxt, table, padtab, n_rows=n_blocks * tme, tme=tme)
        out_buf = _experts(buf, block_e, n_used, w_gate_up, bgu, w_down, bdn, layer=l, tme=tme)
        xt = _combine(xt, out_buf, pos, gates, table, mod[l], final_norm_w.reshape(1, d),
                      seq=seq, batch=batch, final_norm=last)
        x_lat, x_ctx, ctx_row0 = xt, xt, n_lat
    return xt.reshape(batch, seq, d)
```

```python
import functools
import math

import jax
import jax.numpy as jnp
from jax import lax
from jax.experimental import pallas as pl
from jax.experimental.pallas import tpu as pltpu

F32 = jnp.float32
BF16 = jnp.bfloat16
I32 = jnp.int32

D_MODEL = 1024
GRID_W = 64
N_MOD = 6
EPS = 1e-6
NEG_INF = -1e30

HEAD_DIM = 64
ATTN_WIDTH = 512
N_HEADS = 8
N_KV_HEADS = 2
KV_WIDTH = 128
ATT_BLOCK = 128
ROPE_BASE = 10000.0

SSM_INNER = 512
SSM_HEADS = 8
SSM_GROUPS = 2
SSM_STATE = 64
CONV_W = 5
CONV_CH = 768
CHUNK = 128
N_DIRS = 2

N_EXPERTS = 32
TOP_K = 4
D_FF = 1024
SWIGLU_LIMIT = 7.0
SWIGLU_ALPHA = 1.702

LANES = 128
SUBLANES = 8
IN_PAD = 2176
VMEM_LIMIT = 56 * 1024 * 1024
HIGHEST = lax.Precision.HIGHEST


def _params(n_axes, vmem=VMEM_LIMIT):
    return pltpu.CompilerParams(dimension_semantics=("arbitrary",) * n_axes, vmem_limit_bytes=vmem)


def _pick(n, prefs):
    for t in prefs:
        if n % t == 0:
            return t
    raise ValueError(f"no tile in {prefs} divides {n}")


def _sigmoid(x):
    return 1.0 / (1.0 + jnp.exp(-x))


def _dot3(x, m_bf16, left=False):
    hi = x.astype(BF16)
    r1 = x - hi.astype(F32)
    mid = r1.astype(BF16)
    lo = (r1 - mid.astype(F32)).astype(BF16)
    mm = (lambda p: jnp.dot(m_bf16, p, preferred_element_type=F32)) if left else (
        lambda p: jnp.dot(p, m_bf16, preferred_element_type=F32))
    return mm(hi) + mm(mid) + mm(lo)


def _mod_kernel(c_ref, w_ref, b_ref, o_ref):
    c = c_ref[...]
    s = c * _sigmoid(c)
    o_ref[...] = jnp.dot(s, w_ref[...], preferred_element_type=F32, precision=HIGHEST) + b_ref[...]


def _modulation(c_all, w_ada, b_ada):
    depth, d, n = w_ada.shape
    r = c_all.shape[0]
    tn = _pick(n, (1536, 1024, 512, 128))
    out = pl.pallas_call(
        _mod_kernel,
        out_shape=jax.ShapeDtypeStruct((depth, r, n), F32),
        grid=(depth, n // tn),
        in_specs=[pl.BlockSpec((r, d), lambda l, j: (0, 0)),
                  pl.BlockSpec((None, d, tn), lambda l, j: (l, 0, j)),
                  pl.BlockSpec((None, 1, tn), lambda l, j: (l, 0, j))],
        out_specs=pl.BlockSpec((None, r, tn), lambda l, j: (l, 0, j)),
        compiler_params=_params(2),
        name="adaln_mod",
    )(c_all, w_ada, b_ada.reshape(depth, 1, n))
    return out.reshape(depth, r, N_MOD, d)


def _rms_mod(x, nw, shift, scale):
    ms = jnp.mean(x * x, axis=-1, keepdims=True)
    y = x * lax.rsqrt(ms + EPS) * nw
    return y * (1.0 + scale) + shift


def _in_proj_kernel(xl_ref, xc_ref, nw_ref, mod_ref, w_ref, cos_ref, sin_ref,
                    q_ref, k_ref, v_ref, z_ref, xbc_ref, dt_ref, *, n_lat_tiles):
    i = pl.program_id(0)
    is_lat = i < n_lat_tiles
    x = jnp.where(is_lat, xl_ref[...], xc_ref[...])
    h = _rms_mod(x, nw_ref[...], mod_ref[0, 0:1, :], mod_ref[0, 1:2, :])
    p = jnp.dot(h.astype(BF16), w_ref[...], preferred_element_type=F32)
    tm = p.shape[0]
    cos = jnp.where(is_lat, cos_ref[...], 1.0)
    sin = jnp.where(is_lat, sin_ref[...], 0.0)
    lane = lax.broadcasted_iota(I32, (tm, LANES), 1)
    first_half = (lane & 31) < 16

    def rope(t):
        partner = jnp.where(first_half, pltpu.roll(t, LANES - 16, 1), pltpu.roll(t, 16, 1))
        return t * cos + partner * sin

    for j in range(ATTN_WIDTH // LANES):
        q_ref[:, j * LANES:(j + 1) * LANES] = rope(p[:, j * LANES:(j + 1) * LANES]).astype(BF16)
    k_ref[...] = rope(p[:, 512:640]).astype(BF16)
    v_ref[...] = p[:, 640:768].T.astype(BF16)
    z_ref[...] = p[:, 768:1280]
    xbc_ref[...] = p[:, 1280:2048]
    dt_ref[...] = p[:, 2048:2176]


def _token_specs(tm, d, n_lat_tiles, ctx_tile0):
    return [pl.BlockSpec((tm, d), lambda i: (jnp.minimum(i, n_lat_tiles - 1), 0)),
            pl.BlockSpec((tm, d), lambda i: (jnp.maximum(i - n_lat_tiles, 0) + ctx_tile0, 0))]


def _in_proj(x_lat, x_ctx, ctx_row0, nw, mod_l, w_in_p, cos_t, sin_t, *, nt, n_lat, seq, batch):
    d = x_lat.shape[1]
    tm = _pick(math.gcd(seq, nt - n_lat), (512, 256, 128))
    n_pos_tiles = seq // tm
    kern = functools.partial(_in_proj_kernel, n_lat_tiles=n_lat // tm)
    row = lambda i: (i, 0)
    return pl.pallas_call(
        kern,
        out_shape=(jax.ShapeDtypeStruct((nt, ATTN_WIDTH), BF16),
                   jax.ShapeDtypeStruct((nt, KV_WIDTH), BF16),
                   jax.ShapeDtypeStruct((KV_WIDTH, nt), BF16),
                   jax.ShapeDtypeStruct((nt, SSM_INNER), F32),
                   jax.ShapeDtypeStruct((nt, CONV_CH), F32),
                   jax.ShapeDtypeStruct((nt, LANES), F32)),
        grid=(nt // tm,),
        in_specs=_token_specs(tm, d, n_lat // tm, ctx_row0 // tm) + [
                  pl.BlockSpec((1, d), lambda i: (0, 0)),
                  pl.BlockSpec((1, N_MOD, d), lambda i: (jnp.minimum(i * tm // seq, batch), 0, 0)),
                  pl.BlockSpec((d, IN_PAD), lambda i: (0, 0)),
                  pl.BlockSpec((tm, LANES), lambda i: (i % n_pos_tiles, 0)),
                  pl.BlockSpec((tm, LANES), lambda i: (i % n_pos_tiles, 0))],
        out_specs=(pl.BlockSpec((tm, ATTN_WIDTH), row), pl.BlockSpec((tm, KV_WIDTH), row),
                   pl.BlockSpec((KV_WIDTH, tm), lambda i: (0, i)), pl.BlockSpec((tm, SSM_INNER), row),
                   pl.BlockSpec((tm, CONV_CH), row), pl.BlockSpec((tm, LANES), row)),
        compiler_params=_params(1),
        name="in_proj",
    )(x_lat, x_ctx, nw, mod_l, w_in_p, cos_t, sin_t)


def _attn_kernel(sink_ref, q_ref, kp_ref, kc_ref, kn_ref, vp_ref, vc_ref, vn_ref, kx_ref, vx_ref,
                 bias_ref, nw_ref, o_ref):
    blk = ATT_BLOCK
    rep = N_HEADS // N_KV_HEADS
    n_ctx = kx_ref.shape[0]
    scale = HEAD_DIM ** -0.5
    q = q_ref[...] * jnp.asarray(scale, BF16)
    bias = jnp.concatenate([bias_ref[...]] * rep, axis=1)
    heads = []
    for g in range(N_KV_HEADS):
        sl = slice(g * HEAD_DIM, (g + 1) * HEAD_DIM)
        qg = jnp.concatenate([q[:, (g * rep + j) * HEAD_DIM:(g * rep + j + 1) * HEAD_DIM]
                              for j in range(rep)], axis=0)
        kg = jnp.concatenate([kx_ref[:, sl], kp_ref[:, sl], kc_ref[:, sl], kn_ref[:, sl]], axis=0)
        vg = jnp.concatenate([vx_ref[sl, :], vp_ref[sl, :], vc_ref[sl, :], vn_ref[sl, :]], axis=1)
        s = lax.dot_general(kg, qg, (((1,), (1,)), ((), ())), preferred_element_type=F32)
        s = jnp.concatenate([s[:n_ctx], s[n_ctx:] + bias], axis=0)
        sink = jnp.concatenate([jnp.full((1, blk), sink_ref[g * rep + j], F32) for j in range(rep)], axis=1)
        m = jnp.maximum(jnp.max(s, axis=0, keepdims=True), sink)
        e = jnp.exp(s - m)
        denom = jnp.sum(e, axis=0, keepdims=True) + jnp.exp(sink - m)
        og = jnp.dot(vg, e.astype(BF16), preferred_element_type=F32) / denom
        heads += [og[:, j * blk:(j + 1) * blk] for j in range(rep)]
    ssq = heads[0] * heads[0]
    for h in heads[1:]:
        ssq = ssq + h * h
    inv = lax.rsqrt(jnp.sum(ssq, axis=0, keepdims=True) * (1.0 / ATTN_WIDTH) + EPS)
    out_t = jnp.concatenate(heads, axis=0) * inv * nw_ref[...]
    o_ref[...] = out_t.T.astype(BF16)


def _window_bias():
    blk = ATT_BLOCK
    j = jnp.arange(blk)[:, None]
    i = jnp.arange(blk)[None, :]
    zero = jnp.zeros((blk, blk), F32)
    hidden = jnp.full((blk, blk), NEG_INF, F32)
    prev = jnp.where(j >= i, 0.0, NEG_INF).astype(F32)
    nxt = jnp.where(j <= i, 0.0, NEG_INF).astype(F32)
    variants = []
    for v in range(4):
        variants.append(jnp.concatenate([hidden if v & 1 else prev, zero, hidden if v & 2 else nxt], axis=0))
    variants.append(jnp.concatenate([hidden, hidden, hidden], axis=0))
    return jnp.stack(variants)


def _attention(q, k, vt, sinks, nw, *, batch, seq, n_ctx, with_ctx_queries):
    nt = q.shape[0]
    blk = ATT_BLOCK
    nb = seq // blk
    ncq = n_ctx // blk
    nq = nb + (ncq if with_ctx_queries else 0)
    ctx_blk0 = (batch * seq) // n_ctx

    def qmap(b, n, s):
        return (jnp.where(n < nb, b * nb + n, batch * nb + b * ncq + (n - nb)), 0)

    def win(off):
        return lambda b, n, s: b * nb + jnp.clip(n + off, 0, nb - 1)

    def bias_map(b, n, s):
        edge = (n == 0).astype(I32) + 2 * (n == nb - 1).astype(I32)
        return (jnp.where(n < nb, edge, 4), 0, 0)

    k_spec = lambda off: pl.BlockSpec((blk, KV_WIDTH), lambda b, n, s: (win(off)(b, n, s), 0))
    v_spec = lambda off: pl.BlockSpec((KV_WIDTH, blk), lambda b, n, s: (0, win(off)(b, n, s)))
    nw_b = jnp.broadcast_to(nw.reshape(ATTN_WIDTH, 1), (ATTN_WIDTH, blk))
    return pl.pallas_call(
        _attn_kernel,
        out_shape=jax.ShapeDtypeStruct((nt if with_ctx_queries else batch * seq, ATTN_WIDTH), BF16),
        grid_spec=pltpu.PrefetchScalarGridSpec(
            num_scalar_prefetch=1,
            grid=(batch, nq),
            in_specs=[pl.BlockSpec((blk, ATTN_WIDTH), qmap),
                      k_spec(-1), k_spec(0), k_spec(1),
                      v_spec(-1), v_spec(0), v_spec(1),
                      pl.BlockSpec((n_ctx, KV_WIDTH), lambda b, n, s: (ctx_blk0 + b, 0)),
                      pl.BlockSpec((KV_WIDTH, n_ctx), lambda b, n, s: (0, ctx_blk0 + b)),
                      pl.BlockSpec((None, 3 * blk, blk), bias_map),
                      pl.BlockSpec((ATTN_WIDTH, blk), lambda b, n, s: (0, 0))],
            out_specs=pl.BlockSpec((blk, ATTN_WIDTH), qmap)),
        compiler_params=_params(2),
        name="attention",
    )(sinks, q, k, k, k, vt, vt, vt, k, vt, _window_bias(), nw_b)


def _conv_kernel(xp_ref, xc_ref, xn_ref, w_ref, b_ref, o_ref, ext_ref, *, seq, n_lat, n_ctx):
    i = pl.program_id(0)
    tb = xc_ref.shape[0]
    row0 = i * tb
    in_lat = row0 < n_lat
    local = jnp.where(in_lat, row0 % seq, (row0 - n_lat) % n_ctx)
    length = jnp.where(in_lat, seq, n_ctx)
    first = local == 0
    last = local + tb == length
    h = SUBLANES
    ext_ref[0:h, :] = jnp.where(first, 0.0, xp_ref[...])
    ext_ref[h:h + tb, :] = xc_ref[...]
    ext_ref[h + tb:h + tb + h, :] = jnp.where(last, 0.0, xn_ref[...])
    acc = jnp.zeros((tb, CONV_CH), F32) + b_ref[...]
    for kk in range(CONV_W):
        off = h - CONV_W // 2 + kk
        acc = acc + ext_ref[off:off + tb, :] * w_ref[kk:kk + 1, :]
    o_ref[...] = acc * _sigmoid(acc)


def _conv_silu(xbc, conv_w, conv_b, *, seq, n_lat, n_ctx):
    nt = xbc.shape[0]
    tb = _pick(math.gcd(seq, n_ctx), (256, 128))
    h = SUBLANES
    per = tb // h
    n_h = nt // h
    kern = functools.partial(_conv_kernel, seq=seq, n_lat=n_lat, n_ctx=n_ctx)
    return pl.pallas_call(
        kern,
        out_shape=jax.ShapeDtypeStruct((nt, CONV_CH), F32),
        grid=(nt // tb,),
        in_specs=[pl.BlockSpec((h, CONV_CH), lambda i: (jnp.maximum(i * per - 1, 0), 0)),
                  pl.BlockSpec((tb, CONV_CH), lambda i: (i, 0)),
                  pl.BlockSpec((h, CONV_CH), lambda i: (jnp.minimum((i + 1) * per, n_h - 1), 0)),
                  pl.BlockSpec((CONV_W, CONV_CH), lambda i: (0, 0)),
                  pl.BlockSpec((1, CONV_CH), lambda i: (0, 0))],
        out_specs=pl.BlockSpec((tb, CONV_CH), lambda i: (i, 0)),
        scratch_shapes=[pltpu.VMEM((tb + 2 * h, CONV_CH), F32)],
        compiler_params=_params(1),
        name="conv_silu",
    )(xbc, xbc, xbc, conv_w, conv_b.reshape(1, CONV_CH))


def _ssd_chunk(u, dtraw, dtb, alog, state_ref, *, direction):
    q = CHUNK
    hp = SSM_INNER // SSM_HEADS
    per_g = SSM_HEADS // SSM_GROUPS
    gw = per_g * hp
    xs = u[:, :SSM_INNER]
    bm = u[:, SSM_INNER:SSM_INNER + SSM_GROUPS * SSM_STATE]
    cm = u[:, SSM_INNER + SSM_GROUPS * SSM_STATE:]

    xv = dtraw + dtb
    dt = jnp.maximum(xv, 0.0) + jnp.log1p(jnp.exp(-jnp.abs(xv)))
    dta = dt * (-jnp.exp(alog))

    ri = lax.broadcasted_iota(I32, (q, q), 0)
    ci = lax.broadcasted_iota(I32, (q, q), 1)
    tri = (ci <= ri) if direction == 0 else (ci >= ri)
    cs = _dot3(dta, tri.astype(BF16), left=True)
    cs_t = cs.T

    er = lax.broadcasted_iota(I32, (LANES, SSM_INNER), 0)
    ec = lax.broadcasted_iota(I32, (LANES, SSM_INNER), 1)
    expand = (er == direction * SSM_HEADS + jnp.right_shift(ec, hp.bit_length() - 1)).astype(BF16)
    cs_e = _dot3(cs, expand)
    dt_e = _dot3(dt, expand)
    last = q - 1 if direction == 0 else 0
    cs_last = cs_e[last:last + 1, :]

    xdt = (xs * dt_e).astype(BF16)
    xw = (xs * (jnp.exp(cs_last - cs_e) * dt_e)).astype(BF16)
    bm_t = bm.T.astype(BF16)
    cmb = cm.astype(BF16)
    bmb = bm.astype(BF16)
    state = state_ref[...]
    state_b = state.astype(BF16)

    y_diag = []
    y_off = []
    new_states = []
    for g in range(SSM_GROUPS):
        gs = slice(g * SSM_STATE, (g + 1) * SSM_STATE)
        cb = lax.dot_general(cmb[:, gs], bmb[:, gs], (((1,), (1,)), ((), ())), preferred_element_type=F32)
        y_off.append(jnp.dot(cmb[:, gs], state_b[:, g * gw:(g + 1) * gw], preferred_element_type=F32))
        new_states.append(jnp.dot(bm_t[gs, :], xw[:, g * gw:(g + 1) * gw], preferred_element_type=F32))
        for j in range(per_g):
            hh = g * per_g + j
            col = direction * SSM_HEADS + hh
            seg = cs[:, col:col + 1] - cs_t[col:col + 1, :]
            decay = jnp.exp(jnp.where(tri, seg, NEG_INF))
            scores = (cb * decay).astype(BF16)
            y_diag.append(jnp.dot(scores, xdt[:, hh * hp:(hh + 1) * hp], preferred_element_type=F32))
    y = jnp.concatenate(y_diag, axis=1) + jnp.exp(cs_e) * jnp.concatenate(y_off, axis=1)
    state_ref[...] = jnp.exp(cs_last) * state + jnp.concatenate(new_states, axis=1)
    return y


SSD_CHUNKS_PER_STEP = 2


def _ssd_fwd_kernel(u_ref, dt_ref, dtb_ref, alog_ref, skip_ref, y_ref, state_ref):
    @pl.when(pl.program_id(1) == 0)
    def _():
        state_ref[...] = jnp.zeros_like(state_ref)

    for j in range(u_ref.shape[0] // CHUNK):
        rows = slice(j * CHUNK, (j + 1) * CHUNK)
        u = u_ref[rows, :]
        y = _ssd_chunk(u, dt_ref[rows, :], dtb_ref[...], alog_ref[...], state_ref, direction=0)
        y_ref[rows, :] = y + skip_ref[...] * u[:, :SSM_INNER]


def _ssd_bwd_kernel(u_ref, dt_ref, dtb_ref, alog_ref, y0_ref, z_ref, nw_ref, o_ref, state_ref):
    @pl.when(pl.program_id(1) == 0)
    def _():
        state_ref[...] = jnp.zeros_like(state_ref)

    n = u_ref.shape[0] // CHUNK
    for j in range(n - 1, -1, -1):
        rows = slice(j * CHUNK, (j + 1) * CHUNK)
        y = y0_ref[rows, :] + _ssd_chunk(u_ref[rows, :], dt_ref[rows, :], dtb_ref[...], alog_ref[...],
                                         state_ref, direction=1)
        z = z_ref[rows, :]
        gt = y * (z * _sigmoid(z))
        gw = SSM_INNER // SSM_GROUPS
        outs = []
        for g in range(SSM_GROUPS):
            gg = gt[:, g * gw:(g + 1) * gw]
            ms = jnp.mean(gg * gg, axis=-1, keepdims=True)
            outs.append(gg * lax.rsqrt(ms + EPS))
        o_ref[rows, :] = (jnp.concatenate(outs, axis=1) * nw_ref[...]).astype(BF16)


def _ssd(u, dt_raw, z, dtb, alog, skip, ssm_nw, *, batch, seq, n_ctx):
    nt = u.shape[0]
    cps = SSD_CHUNKS_PER_STEP
    q = cps * CHUNK
    assert seq % q == 0 and n_ctx % q == 0
    ncl = seq // q
    ncc = n_ctx // q
    steps = ncc + ncl
    ctx0 = (batch * seq) // q

    def fmap(b, t):
        return (jnp.where(t < ncc, ctx0 + b * ncc + t, b * ncl + (t - ncc)), 0)

    def rmap(b, t):
        return (jnp.where(t < ncc, ctx0 + b * ncc + (ncc - 1 - t), b * ncl + (ncl - 1 - (t - ncc))), 0)

    const = lambda b, t: (0, 0)
    state = pltpu.VMEM((SSM_STATE, SSM_INNER), F32)
    y0 = pl.pallas_call(
        _ssd_fwd_kernel,
        out_shape=jax.ShapeDtypeStruct((nt, SSM_INNER), F32),
        grid=(batch, steps),
        in_specs=[pl.BlockSpec((q, CONV_CH), fmap), pl.BlockSpec((q, LANES), fmap),
                  pl.BlockSpec((1, LANES), const), pl.BlockSpec((1, LANES), const),
                  pl.BlockSpec((1, SSM_INNER), const)],
        out_specs=pl.BlockSpec((q, SSM_INNER), fmap),
        scratch_shapes=[state],
        compiler_params=_params(2),
        name="ssd_forward",
    )(u, dt_raw, dtb, alog, skip)
    return pl.pallas_call(
        _ssd_bwd_kernel,
        out_shape=jax.ShapeDtypeStruct((nt, SSM_INNER), BF16),
        grid=(batch, steps),
        in_specs=[pl.BlockSpec((q, CONV_CH), rmap), pl.BlockSpec((q, LANES), rmap),
                  pl.BlockSpec((1, LANES), const), pl.BlockSpec((1, LANES), const),
                  pl.BlockSpec((q, SSM_INNER), rmap), pl.BlockSpec((q, SSM_INNER), rmap),
                  pl.BlockSpec((1, SSM_INNER), const)],
        out_specs=pl.BlockSpec((q, SSM_INNER), rmap),
        scratch_shapes=[state],
        compiler_params=_params(2),
        name="ssd_backward",
    )(u, dt_raw, dtb, alog, y0, z, ssm_nw)


MOE_TILE = 256


def _out_proj_kernel(xl_ref, xc_ref, a_ref, s_ref, wa_ref, ws_ref, mod_ref, nw_ref, wr_ref, br_ref,
                     xo_ref, h_ref, idxt_ref, gate_ref, cnt_ref, *, n_lat_tiles):
    mix = (jnp.dot(a_ref[...], wa_ref[...], preferred_element_type=F32)
           + jnp.dot(s_ref[...], ws_ref[...], preferred_element_type=F32))
    x_in = jnp.where(pl.program_id(0) < n_lat_tiles, xl_ref[...], xc_ref[...])
    x = x_in + mod_ref[0, 2:3, :] * mix
    xo_ref[...] = x
    h = _rms_mod(x, nw_ref[...], mod_ref[0, 3:4, :], mod_ref[0, 4:5, :])
    h_hi = h.astype(BF16)
    h_ref[...] = h_hi
    h_lo = (h - h_hi.astype(F32)).astype(BF16)
    hh = jnp.dot(h_hi, wr_ref[...], preferred_element_type=F32)
    lh = jnp.dot(h_lo, wr_ref[:, :LANES], preferred_element_type=F32)
    logits = hh[:, :LANES] + hh[:, LANES:] + lh + br_ref[...]
    tm = logits.shape[0]
    lane = lax.broadcasted_iota(I32, (tm, LANES), 1)
    lane_f = lane.astype(F32)
    work = jnp.where(lane < N_EXPERTS, logits, -jnp.inf)
    idx_out = jnp.zeros((tm, LANES), F32)
    val_out = jnp.full((tm, LANES), -jnp.inf, F32)
    onehot = jnp.zeros((tm, LANES), F32)
    for kk in range(TOP_K):
        m = jnp.max(work, axis=-1, keepdims=True)
        sel = jnp.min(jnp.where(work == m, lane_f, float(LANES)), axis=-1, keepdims=True)
        idx_out = jnp.where(lane == kk, sel, idx_out)
        val_out = jnp.where(lane == kk, m, val_out)
        picked = lane_f == sel
        onehot = onehot + jnp.where(picked, 1.0, 0.0)
        work = jnp.where(picked, -jnp.inf, work)
    top = jnp.max(val_out, axis=-1, keepdims=True)
    e = jnp.exp(val_out - top)
    gate_ref[...] = e / jnp.sum(e, axis=-1, keepdims=True)
    idxt_ref[...] = idx_out.T[0:SUBLANES, :].astype(I32)
    ones = jnp.ones((SUBLANES, MOE_TILE), BF16)
    oh_b = onehot.astype(BF16)
    for j in range(tm // MOE_TILE):
        cnt_ref[j] = jnp.dot(ones, oh_b[j * MOE_TILE:(j + 1) * MOE_TILE, :],
                             preferred_element_type=F32).astype(I32)


def _out_proj(x_lat, x_ctx, ctx_row0, attn, ssm, w_out_a, w_out_s, mod_l, nfw, wr_hl, br_p,
              *, n_rows, n_lat, seq, batch):
    d = x_lat.shape[1]
    tm = _pick(math.gcd(seq, n_rows - n_lat) if n_rows > n_lat else seq, (512, 256))
    sub = tm // MOE_TILE
    row = lambda i: (i, 0)
    const = lambda i: (0, 0)
    kern = functools.partial(_out_proj_kernel, n_lat_tiles=n_lat // tm)
    return pl.pallas_call(
        kern,
        out_shape=(jax.ShapeDtypeStruct((n_rows, d), F32), jax.ShapeDtypeStruct((n_rows, d), BF16),
                   jax.ShapeDtypeStruct((SUBLANES, n_rows), I32), jax.ShapeDtypeStruct((n_rows, LANES), F32),
                   jax.ShapeDtypeStruct((n_rows // MOE_TILE, SUBLANES, LANES), I32)),
        grid=(n_rows // tm,),
        in_specs=_token_specs(tm, d, n_lat // tm, ctx_row0 // tm) + [
                  pl.BlockSpec((tm, ATTN_WIDTH), row), pl.BlockSpec((tm, SSM_INNER), row),
                  pl.BlockSpec((ATTN_WIDTH, d), const), pl.BlockSpec((SSM_INNER, d), const),
                  pl.BlockSpec((1, N_MOD, d), lambda i: (jnp.minimum(i * tm // seq, batch), 0, 0)),
                  pl.BlockSpec((1, d), const), pl.BlockSpec((d, 2 * LANES), const),
                  pl.BlockSpec((1, LANES), const)],
        out_specs=(pl.BlockSpec((tm, d), row), pl.BlockSpec((tm, d), row),
                   pl.BlockSpec((SUBLANES, tm), lambda i: (0, i)), pl.BlockSpec((tm, LANES), row),
                   pl.BlockSpec((sub, SUBLANES, LANES), lambda i: (i, 0, 0))),
        compiler_params=_params(1),
        name="out_proj_router",
    )(x_lat, x_ctx, attn, ssm, w_out_a, w_out_s, mod_l, nfw, wr_hl, br_p)


TAB_CNT, TAB_OFF, TAB_BASE = 0, N_EXPERTS, 2 * N_EXPERTS
ROW_TILE = D_MODEL // LANES
assert ROW_TILE == SUBLANES


def _to_row_tiles(ref, base, val):
    n = val.shape[0]
    for s in range(ROW_TILE):
        ref[pl.ds(base + s, n, stride=ROW_TILE), :] = val[:, s * LANES:(s + 1) * LANES]


def _from_row_tiles(ref, base, n):
    return jnp.concatenate([ref[pl.ds(base + s, n, stride=ROW_TILE), :] for s in range(ROW_TILE)], axis=1)


def _tile_rows(start, size):
    return pl.ds(pl.multiple_of(start * ROW_TILE, ROW_TILE), size * ROW_TILE)


def _for_each_run(tab_ref, lanes, make_copy, fn, *, enabled=None, unrolled=False):
    def per_expert(e, carry=0):
        cnt = tab_ref[0, lanes[0] + e]
        if enabled is not None:
            cnt = jnp.where(enabled, cnt, 0)

        @pl.when(cnt > 0)
        def _():
            fn(make_copy(tab_ref[0, lanes[1] + e], tab_ref[0, lanes[2] + e], cnt))
        return carry

    if unrolled:
        for e in range(N_EXPERTS):
            per_expert(e)
    else:
        lax.fori_loop(0, N_EXPERTS, per_expert, 0)


def _start(copy):
    copy.start()


def _wait(copy):
    copy.wait()


def _dispatch_kernel(tab_ref, tabp_ref, pad_ref, idxt_ref, h_ref, buf_ref, pos_ref,
                     scr_ref, zero_ref, sem, zsem, *, n_tiles):
    i = pl.program_id(0)
    slot = i % 2
    t = h_ref.shape[0]
    rows = TOP_K * t
    run_lanes = (TAB_CNT, TAB_OFF, TAB_BASE)

    def copy_out(s):
        return lambda local, glob, size: pltpu.make_async_copy(
            scr_ref.at[_tile_rows(s * rows + local, size), :], buf_ref.at[_tile_rows(glob, size), :], sem.at[s])

    @pl.when(i == 0)
    def _():
        zero_ref[...] = jnp.zeros_like(zero_ref)
        zero_copy = lambda local, glob, size: pltpu.make_async_copy(
            zero_ref.at[_tile_rows(0, size), :], buf_ref.at[_tile_rows(glob, size), :], zsem)
        pad_lanes = (0, 0, N_EXPERTS)
        _for_each_run(pad_ref, pad_lanes, zero_copy, _start)
        _for_each_run(pad_ref, pad_lanes, zero_copy, _wait)
        tail_start = pad_ref[0, 2 * N_EXPERTS]
        n_tail = pad_ref[0, 2 * N_EXPERTS + 1]
        zrows = zero_ref.shape[0] // ROW_TILE

        def tail(fn):
            def body(j, carry):
                fn(zero_copy(0, tail_start + j * zrows, zrows))
                return carry
            return body

        lax.fori_loop(0, n_tail, tail(_start), 0)
        lax.fori_loop(0, n_tail, tail(_wait), 0)

    e_iota = lax.broadcasted_iota(I32, (N_EXPERTS, t), 0)
    upper = (lax.broadcasted_iota(I32, (t, t), 0) < lax.broadcasted_iota(I32, (t, t), 1)).astype(BF16)
    onehots, counts, before = [], [], []
    for kk in range(TOP_K):
        oh = jnp.where(e_iota == idxt_ref[kk:kk + 1, :], 1.0, 0.0)
        onehots.append(oh)
        counts.append(jnp.sum(oh, axis=1, keepdims=True))
        before.append(jnp.dot(oh.astype(BF16), upper, preferred_element_type=F32))
    total = counts[0] + counts[1] + counts[2] + counts[3]
    below = (lax.broadcasted_iota(I32, (N_EXPERTS, N_EXPERTS), 1)
             < lax.broadcasted_iota(I32, (N_EXPERTS, N_EXPERTS), 0)).astype(BF16)
    start = _dot3(jnp.broadcast_to(total, (N_EXPERTS, LANES)), below, left=True)[:, 0:1]
    pos_rows = []
    for kk in range(TOP_K):
        pos_rows.append(jnp.sum(onehots[kk] * (before[kk] + start), axis=0, keepdims=True))
        start = start + counts[kk]
    r_iota = lax.broadcasted_iota(I32, (rows, t), 0)
    perm = jnp.zeros((rows, t), F32)
    for kk in range(TOP_K):
        perm = jnp.where(r_iota == pos_rows[kk].astype(I32), 1.0, perm)
    _to_row_tiles(scr_ref, slot * rows * ROW_TILE,
                  jnp.dot(perm.astype(BF16), h_ref[...], preferred_element_type=F32))
    pos_t = jnp.concatenate(pos_rows + [jnp.zeros((LANES - TOP_K, t), F32)], axis=0)
    pos_ref[...] = pos_t.T.astype(I32)

    _for_each_run(tab_ref, run_lanes, copy_out(slot), _start, unrolled=True)
    _for_each_run(tabp_ref, run_lanes, copy_out(1 - slot), _wait, enabled=i > 0, unrolled=True)

    @pl.when(i == n_tiles - 1)
    def _():
        _for_each_run(tab_ref, run_lanes, copy_out(slot), _wait)


def _dispatch(h2, idxt, table, padtab, *, n_rows, tme):
    nt, d = h2.shape
    t = MOE_TILE
    n_tiles = nt // t
    smem = lambda imap: pl.BlockSpec((None, 1, LANES), imap, memory_space=pltpu.SMEM)
    kern = functools.partial(_dispatch_kernel, n_tiles=n_tiles)
    return pl.pallas_call(
        kern,
        out_shape=(jax.ShapeDtypeStruct((n_rows * ROW_TILE, LANES), F32),
                   jax.ShapeDtypeStruct((nt, LANES), I32)),
        grid=(n_tiles,),
        in_specs=[smem(lambda i: (i, 0, 0)), smem(lambda i: (jnp.maximum(i - 1, 0), 0, 0)),
                  smem(lambda i: (0, 0, 0)),
                  pl.BlockSpec((SUBLANES, t), lambda i: (0, i)),
                  pl.BlockSpec((t, d), lambda i: (i, 0))],
        out_specs=(pl.BlockSpec(memory_space=pl.ANY), pl.BlockSpec((t, LANES), lambda i: (i, 0))),
        scratch_shapes=[pltpu.VMEM((2 * TOP_K * t * ROW_TILE, LANES), F32),
                        pltpu.VMEM((tme * ROW_TILE, LANES), F32),
                        pltpu.SemaphoreType.DMA((2,)), pltpu.SemaphoreType.DMA],
        compiler_params=_params(1),
        name="moe_dispatch",
    )(table, table, padtab, idxt, h2)


def _expert_kernel(be_ref, nu_ref, nx_ref, x_ref, wgu_hbm, bgu_ref, wd_hbm, bd_ref, o_ref,
                   wgu_f, wd_f, wgu_b, wd_b, sem, *, layer):
    i = pl.program_id(0)
    tme = x_ref.shape[0] // ROW_TILE

    def fetch(e):
        return (pltpu.make_async_copy(wgu_hbm.at[layer, e], wgu_f, sem.at[0]),
                pltpu.make_async_copy(wd_hbm.at[layer, e], wd_f, sem.at[1]))

    @pl.when(i < nu_ref[0])
    def _():
        e = be_ref[i]

        @pl.when(i == 0)
        def _():
            for c in fetch(e):
                c.start()

        @pl.when((i == 0) | (e != be_ref[jnp.maximum(i - 1, 0)]))
        def _():
            for c in fetch(e):
                c.wait()
            wgu_b[...] = wgu_f[...].astype(BF16)
            wd_b[...] = wd_f[...].astype(BF16)
            nxt = nx_ref[i]

            @pl.when(nxt >= 0)
            def _():
                for c in fetch(nxt):
                    c.start()

        x = _from_row_tiles(x_ref, 0, tme)
        gu = jnp.dot(x.astype(BF16), wgu_b[...], preferred_element_type=F32) + bgu_ref[...]
        glu = jnp.minimum(gu[:, :D_FF], SWIGLU_LIMIT)
        lin = jnp.clip(gu[:, D_FF:], -SWIGLU_LIMIT, SWIGLU_LIMIT)
        act = glu * _sigmoid(SWIGLU_ALPHA * glu) * (lin + 1.0)
        _to_row_tiles(o_ref, 0, jnp.dot(act.astype(BF16), wd_b[...], preferred_element_type=F32) + bd_ref[...])

    @pl.when(i >= nu_ref[0])
    def _():
        o_ref[...] = jnp.zeros_like(o_ref)


def _experts(buf, block_e, n_used, next_e, wgu, bgu, wd, bd, *, layer, tme):
    d = D_MODEL
    nblk = buf.shape[0] // (tme * ROW_TILE)
    xmap = lambda i, be, nu, nx: (jnp.maximum(jnp.minimum(i, nu[0] - 1), 0), 0)
    emap = lambda i, be, nu, nx: (layer, be[i], 0, 0)
    kern = functools.partial(_expert_kernel, layer=layer)
    return pl.pallas_call(
        kern,
        out_shape=jax.ShapeDtypeStruct(buf.shape, F32),
        grid_spec=pltpu.PrefetchScalarGridSpec(
            num_scalar_prefetch=3,
            grid=(nblk,),
            in_specs=[pl.BlockSpec((tme * ROW_TILE, LANES), xmap),
                      pl.BlockSpec(memory_space=pl.ANY),
                      pl.BlockSpec((None, None, 1, 2 * D_FF), emap),
                      pl.BlockSpec(memory_space=pl.ANY),
                      pl.BlockSpec((None, None, 1, d), emap)],
            out_specs=pl.BlockSpec((tme * ROW_TILE, LANES), lambda i, be, nu, nx: (i, 0)),
            scratch_shapes=[pltpu.VMEM((d, 2 * D_FF), F32), pltpu.VMEM((D_FF, d), F32),
                            pltpu.VMEM((d, 2 * D_FF), BF16), pltpu.VMEM((D_FF, d), BF16),
                            pltpu.SemaphoreType.DMA((2,))]),
        compiler_params=_params(1),
        name="expert_ffn",
    )(block_e, n_used, next_e, buf, wgu, bgu, wd, bd)


def _combine_kernel(tab_ref, tabn_ref, pos_ref, gate_ref, x_ref, mod_ref, fw_ref, ob_ref, o_ref,
                    scr_ref, sem, *, n_tiles, final_norm):
    i = pl.program_id(0)
    slot = i % 2
    t = x_ref.shape[0]
    rows = TOP_K * t
    run_lanes = (TAB_CNT, TAB_OFF, TAB_BASE)

    def copy_in(s):
        return lambda local, glob, size: pltpu.make_async_copy(
            ob_ref.at[_tile_rows(glob, size), :], scr_ref.at[_tile_rows(s * rows + local, size), :], sem.at[s])

    @pl.when(i == 0)
    def _():
        _for_each_run(tab_ref, run_lanes, copy_in(slot), _start)

    _for_each_run(tabn_ref, run_lanes, copy_in(1 - slot), _start, enabled=i + 1 < n_tiles, unrolled=True)

    lane = lax.broadcasted_iota(I32, (t, rows), 1)
    pw = jnp.zeros((t, rows), F32)
    for kk in range(TOP_K):
        pw = jnp.where(lane == pos_ref[:, kk:kk + 1], gate_ref[:, kk:kk + 1], pw)
    pw_hi = pw.astype(BF16)
    pw_lo = (pw - pw_hi.astype(F32)).astype(BF16)

    _for_each_run(tab_ref, run_lanes, copy_in(slot), _wait, unrolled=True)
    y = _from_row_tiles(scr_ref, slot * rows * ROW_TILE, rows).astype(BF16)
    f = jnp.dot(pw_hi, y, preferred_element_type=F32) + jnp.dot(pw_lo, y, preferred_element_type=F32)
    x = x_ref[...] + mod_ref[0, 5:6, :] * f
    if final_norm:
        ms = jnp.mean(x * x, axis=-1, keepdims=True)
        x = x * lax.rsqrt(ms + EPS) * fw_ref[...]
    o_ref[...] = x


def _combine(xt, out_buf, pos, gates, table, mod_l, fw, *, seq, batch, final_norm):
    nt, d = xt.shape
    t = MOE_TILE
    n_tiles = nt // t
    row = lambda i: (i, 0)
    smem = lambda imap: pl.BlockSpec((None, 1, LANES), imap, memory_space=pltpu.SMEM)
    kern = functools.partial(_combine_kernel, n_tiles=n_tiles, final_norm=final_norm)
    return pl.pallas_call(
        kern,
        out_shape=jax.ShapeDtypeStruct((nt, d), F32),
        grid=(n_tiles,),
        in_specs=[smem(lambda i: (i, 0, 0)), smem(lambda i: (jnp.minimum(i + 1, n_tiles - 1), 0, 0)),
                  pl.BlockSpec((t, LANES), row), pl.BlockSpec((t, LANES), row), pl.BlockSpec((t, d), row),
                  pl.BlockSpec((1, N_MOD, d), lambda i: (jnp.minimum(i * t // seq, batch), 0, 0)),
                  pl.BlockSpec((1, d), lambda i: (0, 0)),
                  pl.BlockSpec(memory_space=pl.ANY)],
        out_specs=pl.BlockSpec((t, d), row),
        scratch_shapes=[pltpu.VMEM((2 * TOP_K * t * ROW_TILE, LANES), F32), pltpu.SemaphoreType.DMA((2,))],
        compiler_params=_params(1),
        name="moe_combine",
    )(table, table, pos, gates, xt, mod_l, fw, out_buf)


def _route_tables(counts, *, tme, n_blocks):
    cnt = counts[:, 0, :N_EXPERTS]
    total = jnp.sum(cnt, axis=0)
    padded = (total + tme - 1) // tme * tme
    pad_end = jnp.cumsum(padded)
    pad_start = pad_end - padded
    base = pad_start[None, :] + jnp.cumsum(cnt, axis=0) - cnt
    off = jnp.cumsum(cnt, axis=1) - cnt
    table = jnp.concatenate([cnt, off, base, jnp.zeros_like(cnt)], axis=1).astype(I32)[:, None, :]
    tail = jnp.stack([pad_end[-1], n_blocks - pad_end[-1] // tme])
    padtab = jnp.concatenate([padded - total, pad_start + total, tail,
                              jnp.zeros((LANES - 2 * N_EXPERTS - 2,), I32)]).astype(I32)[None, None, :]
    n_used = (pad_end[-1] // tme).astype(I32).reshape(1)
    block_row0 = jnp.arange(n_blocks, dtype=I32) * tme
    block_e = jnp.minimum(jnp.sum((pad_end[None, :] <= block_row0[:, None]).astype(I32), axis=1),
                          N_EXPERTS - 1).astype(I32)
    ids = jnp.arange(N_EXPERTS, dtype=I32)
    later = (ids[None, :] > ids[:, None]) & (total[None, :] > 0)
    next_used = jnp.min(jnp.where(later, ids[None, :], N_EXPERTS), axis=1)
    next_used = jnp.where(next_used == N_EXPERTS, -1, next_used).astype(I32)
    return table, padtab, block_e, n_used, next_used[block_e]


def _rope_tables(seq):
    rows = seq // GRID_W
    row_pos = jnp.repeat(jnp.arange(rows, dtype=I32), GRID_W).astype(F32)
    col_pos = jnp.tile(jnp.arange(GRID_W, dtype=I32), rows).astype(F32)
    n_freq = HEAD_DIM // 4
    inv_freq = ROPE_BASE ** (-jnp.arange(n_freq, dtype=F32) / n_freq)
    lane = jnp.arange(LANES)
    f = lane % n_freq
    use_col = (lane % HEAD_DIM) >= HEAD_DIM // 2
    ang = jnp.where(use_col[None, :], col_pos[:, None], row_pos[:, None]) * inv_freq[f][None, :]
    first_half = (lane % 32) < 16
    return jnp.cos(ang), jnp.where(first_half[None, :], -jnp.sin(ang), jnp.sin(ang))


def kernel(x, c, ctx, c_ctx, w_ada, b_ada, norm_mix_w, norm_ffn_w, w_in, conv_w, conv_b, dt_bias, a_log, d_skip, ssm_norm_w, attn_sinks, attn_norm_w, w_out, w_router, b_router, w_gate_up, b_gate_up, w_down, b_down, final_norm_w):
    batch, seq, d = x.shape
    n_ctx = ctx.shape[1]
    depth = w_ada.shape[0]
    n_lat = batch * seq
    nt = n_lat + batch * n_ctx
    assert d == D_MODEL and seq % ATT_BLOCK == 0 and n_ctx % ATT_BLOCK == 0 and n_lat % n_ctx == 0

    r_mod = -(-(batch + 1) // SUBLANES) * SUBLANES
    c_all = jnp.zeros((r_mod, d), F32).at[:batch].set(c).at[batch].set(c_ctx)
    mod = _modulation(c_all, w_ada, b_ada)
    cos_t, sin_t = _rope_tables(seq)

    tme = 512
    assert nt % MOE_TILE == 0 and n_lat % MOE_TILE == 0
    hp = SSM_INNER // SSM_HEADS
    bgu = b_gate_up.reshape(depth, N_EXPERTS, 1, 2 * D_FF)
    bdn = b_down.reshape(depth, N_EXPERTS, 1, d)

    x_lat, x_ctx, ctx_row0 = x.reshape(n_lat, d), ctx.reshape(batch * n_ctx, d), 0
    for l in range(depth):
        last = l == depth - 1
        w_in_p = jnp.pad(w_in[l], ((0, 0), (0, IN_PAD - w_in.shape[2]))).astype(BF16)
        q, k, v, z, xbc, dt_raw = _in_proj(x_lat, x_ctx, ctx_row0, norm_mix_w[l].reshape(1, d), mod[l], w_in_p,
                                           cos_t, sin_t, nt=nt, n_lat=n_lat, seq=seq, batch=batch)
        attn = _attention(q, k, v, attn_sinks[l], attn_norm_w[l].reshape(1, ATTN_WIDTH),
                          batch=batch, seq=seq, n_ctx=n_ctx, with_ctx_queries=not last)
        u = _conv_silu(xbc, conv_w[l], conv_b[l], seq=seq, n_lat=n_lat, n_ctx=n_ctx)
        pad16 = lambda t: jnp.pad(t.reshape(1, N_DIRS * SSM_HEADS), ((0, 0), (0, LANES - N_DIRS * SSM_HEADS)))
        ssm = _ssd(u, dt_raw, z, pad16(dt_bias[l]), pad16(a_log[l]),
                   jnp.repeat(d_skip[l], hp).reshape(1, SSM_INNER), ssm_norm_w[l].reshape(1, SSM_INNER),
                   batch=batch, seq=seq, n_ctx=n_ctx)
        n_rows = n_lat if last else nt
        n_blocks = -(-(n_rows * TOP_K) // tme) + N_EXPERTS
        w_o = w_out[l].astype(BF16)
        wr_p = jnp.pad(w_router[l], ((0, 0), (0, LANES - N_EXPERTS)))
        wr_hi = wr_p.astype(BF16)
        wr_hl = jnp.concatenate([wr_hi, (wr_p - wr_hi.astype(F32)).astype(BF16)], axis=1)
        br_p = jnp.pad(b_router[l].reshape(1, N_EXPERTS), ((0, 0), (0, LANES - N_EXPERTS)))
        xt, h2, idxt, gates, counts = _out_proj(x_lat, x_ctx, ctx_row0, attn, ssm, w_o[:ATTN_WIDTH],
                                                w_o[ATTN_WIDTH:], mod[l], norm_ffn_w[l].reshape(1, d),
                                                wr_hl, br_p,
                                                n_rows=n_rows, n_lat=n_lat, seq=seq, batch=batch)
        table, padtab, block_e, n_used, next_e = _route_tables(counts, tme=tme, n_blocks=n_blocks)
        buf, pos = _dispatch(h2, idxt, table, padtab, n_rows=n_blocks * tme, tme=tme)
        out_buf = _experts(buf, block_e, n_used, next_e, w_gate_up, bgu, w_down, bdn, layer=l, tme=tme)
        xt = _combine(xt, out_buf, pos, gates, table, mod[l], final_norm_w.reshape(1, d),
                      seq=seq, batch=batch, final_norm=last)
        x_lat, x_ctx, ctx_row0 = xt, xt, n_lat
    return xt.reshape(batch, seq, d)
```

```python
import functools
import math

import jax
import jax.numpy as jnp
from jax import lax
from jax.experimental import pallas as pl
from jax.experimental.pallas import tpu as pltpu

F32 = jnp.float32
BF16 = jnp.bfloat16
I32 = jnp.int32

D_MODEL = 1024
GRID_W = 64
N_MOD = 6
EPS = 1e-6
NEG_INF = -1e30

HEAD_DIM = 64
ATTN_WIDTH = 512
N_HEADS = 8
N_KV_HEADS = 2
KV_WIDTH = 128
ATT_BLOCK = 128
ROPE_BASE = 10000.0

SSM_INNER = 512
SSM_HEADS = 8
SSM_GROUPS = 2
SSM_STATE = 64
CONV_W = 5
CONV_CH = 768
CHUNK = 128
N_DIRS = 2

N_EXPERTS = 32
TOP_K = 4
D_FF = 1024
SWIGLU_LIMIT = 7.0
SWIGLU_ALPHA = 1.702

LANES = 128
SUBLANES = 8
IN_PAD = 2176
VMEM_LIMIT = 56 * 1024 * 1024
HIGHEST = lax.Precision.HIGHEST


def _params(n_axes, vmem=VMEM_LIMIT):
    return pltpu.CompilerParams(dimension_semantics=("arbitrary",) * n_axes, vmem_limit_bytes=vmem)


def _pick(n, prefs):
    for t in prefs:
        if n % t == 0:
            return t
    raise ValueError(f"no tile in {prefs} divides {n}")


def _sigmoid(x):
    return 1.0 / (1.0 + jnp.exp(-x))


def _dot3(x, m_bf16, left=False):
    hi = x.astype(BF16)
    r1 = x - hi.astype(F32)
    mid = r1.astype(BF16)
    lo = (r1 - mid.astype(F32)).astype(BF16)
    mm = (lambda p: jnp.dot(m_bf16, p, preferred_element_type=F32)) if left else (
        lambda p: jnp.dot(p, m_bf16, preferred_element_type=F32))
    return mm(hi) + mm(mid) + mm(lo)


def _mod_kernel(c_ref, w_ref, b_ref, o_ref):
    c = c_ref[...]
    s = c * _sigmoid(c)
    o_ref[...] = jnp.dot(s, w_ref[...], preferred_element_type=F32, precision=HIGHEST) + b_ref[...]


def _modulation(c_all, w_ada, b_ada):
    depth, d, n = w_ada.shape
    r = c_all.shape[0]
    tn = _pick(n, (1536, 1024, 512, 128))
    out = pl.pallas_call(
        _mod_kernel,
        out_shape=jax.ShapeDtypeStruct((depth, r, n), F32),
        grid=(depth, n // tn),
        in_specs=[pl.BlockSpec((r, d), lambda l, j: (0, 0)),
                  pl.BlockSpec((None, d, tn), lambda l, j: (l, 0, j)),
                  pl.BlockSpec((None, 1, tn), lambda l, j: (l, 0, j))],
        out_specs=pl.BlockSpec((None, r, tn), lambda l, j: (l, 0, j)),
        compiler_params=_params(2),
        name="adaln_mod",
    )(c_all, w_ada, b_ada.reshape(depth, 1, n))
    return out.reshape(depth, r, N_MOD, d)


def _rms_mod(x, nw, shift, scale):
    ms = jnp.mean(x * x, axis=-1, keepdims=True)
    y = x * lax.rsqrt(ms + EPS) * nw
    return y * (1.0 + scale) + shift


def _in_proj_kernel(xl_ref, xc_ref, nw_ref, mod_ref, w_ref, cos_ref, sin_ref,
                    q_ref, k_ref, v_ref, z_ref, xbc_ref, dt_ref, *, n_lat_tiles):
    i = pl.program_id(0)
    is_lat = i < n_lat_tiles
    x = jnp.where(is_lat, xl_ref[...], xc_ref[...])
    h = _rms_mod(x, nw_ref[...], mod_ref[0, 0:1, :], mod_ref[0, 1:2, :])
    p = jnp.dot(h.astype(BF16), w_ref[...], preferred_element_type=F32)
    tm = p.shape[0]
    cos = jnp.where(is_lat, cos_ref[...], 1.0)
    sin = jnp.where(is_lat, sin_ref[...], 0.0)
    lane = lax.broadcasted_iota(I32, (tm, LANES), 1)
    first_half = (lane & 31) < 16

    def rope(t):
        partner = jnp.where(first_half, pltpu.roll(t, LANES - 16, 1), pltpu.roll(t, 16, 1))
        return t * cos + partner * sin

    for j in range(ATTN_WIDTH // LANES):
        q_ref[:, j * LANES:(j + 1) * LANES] = rope(p[:, j * LANES:(j + 1) * LANES]).astype(BF16)
    k_ref[...] = rope(p[:, 512:640]).astype(BF16)
    v_ref[...] = p[:, 640:768].T.astype(BF16)
    z_ref[...] = p[:, 768:1280]
    xbc_ref[...] = p[:, 1280:2048]
    dt_ref[...] = p[:, 2048:2176]


def _token_specs(tm, d, n_lat_tiles, ctx_tile0):
    return [pl.BlockSpec((tm, d), lambda i: (jnp.minimum(i, n_lat_tiles - 1), 0)),
            pl.BlockSpec((tm, d), lambda i: (jnp.maximum(i - n_lat_tiles, 0) + ctx_tile0, 0))]


def _in_proj(x_lat, x_ctx, ctx_row0, nw, mod_l, w_in_p, cos_t, sin_t, *, nt, n_lat, seq, batch):
    d = x_lat.shape[1]
    tm = _pick(math.gcd(seq, nt - n_lat), (512, 256, 128))
    n_pos_tiles = seq // tm
    kern = functools.partial(_in_proj_kernel, n_lat_tiles=n_lat // tm)
    row = lambda i: (i, 0)
    return pl.pallas_call(
        kern,
        out_shape=(jax.ShapeDtypeStruct((nt, ATTN_WIDTH), BF16),
                   jax.ShapeDtypeStruct((nt, KV_WIDTH), BF16),
                   jax.ShapeDtypeStruct((KV_WIDTH, nt), BF16),
                   jax.ShapeDtypeStruct((nt, SSM_INNER), F32),
                   jax.ShapeDtypeStruct((nt, CONV_CH), F32),
                   jax.ShapeDtypeStruct((nt, LANES), F32)),
        grid=(nt // tm,),
        in_specs=_token_specs(tm, d, n_lat // tm, ctx_row0 // tm) + [
                  pl.BlockSpec((1, d), lambda i: (0, 0)),
                  pl.BlockSpec((1, N_MOD, d), lambda i: (jnp.minimum(i * tm // seq, batch), 0, 0)),
                  pl.BlockSpec((d, IN_PAD), lambda i: (0, 0)),
                  pl.BlockSpec((tm, LANES), lambda i: (i % n_pos_tiles, 0)),
                  pl.BlockSpec((tm, LANES), lambda i: (i % n_pos_tiles, 0))],
        out_specs=(pl.BlockSpec((tm, ATTN_WIDTH), row), pl.BlockSpec((tm, KV_WIDTH), row),
                   pl.BlockSpec((KV_WIDTH, tm), lambda i: (0, i)), pl.BlockSpec((tm, SSM_INNER), row),
                   pl.BlockSpec((tm, CONV_CH), row), pl.BlockSpec((tm, LANES), row)),
        compiler_params=_params(1),
        name="in_proj",
    )(x_lat, x_ctx, nw, mod_l, w_in_p, cos_t, sin_t)


def _attn_kernel(sink_ref, q_ref, kp_ref, kc_ref, kn_ref, vp_ref, vc_ref, vn_ref, kx_ref, vx_ref,
                 bias_ref, nw_ref, o_ref):
    blk = ATT_BLOCK
    rep = N_HEADS // N_KV_HEADS
    n_ctx = kx_ref.shape[0]
    scale = HEAD_DIM ** -0.5
    q = q_ref[...] * jnp.asarray(scale, BF16)
    bias = jnp.concatenate([bias_ref[...]] * rep, axis=1)
    heads = []
    for g in range(N_KV_HEADS):
        sl = slice(g * HEAD_DIM, (g + 1) * HEAD_DIM)
        qg = jnp.concatenate([q[:, (g * rep + j) * HEAD_DIM:(g * rep + j + 1) * HEAD_DIM]
                              for j in range(rep)], axis=0)
        kg = jnp.concatenate([kx_ref[:, sl], kp_ref[:, sl], kc_ref[:, sl], kn_ref[:, sl]], axis=0)
        vg = jnp.concatenate([vx_ref[sl, :], vp_ref[sl, :], vc_ref[sl, :], vn_ref[sl, :]], axis=1)
        s = lax.dot_general(kg, qg, (((1,), (1,)), ((), ())), preferred_element_type=F32)
        s = jnp.concatenate([s[:n_ctx], s[n_ctx:] + bias], axis=0)
        sink = jnp.concatenate([jnp.full((1, blk), sink_ref[g * rep + j], F32) for j in range(rep)], axis=1)
        m = jnp.maximum(jnp.max(s, axis=0, keepdims=True), sink)
        e = jnp.exp(s - m)
        denom = jnp.sum(e, axis=0, keepdims=True) + jnp.exp(sink - m)
        og = jnp.dot(vg, e.astype(BF16), preferred_element_type=F32) / denom
        heads += [og[:, j * blk:(j + 1) * blk] for j in range(rep)]
    ssq = heads[0] * heads[0]
    for h in heads[1:]:
        ssq = ssq + h * h
    inv = lax.rsqrt(jnp.sum(ssq, axis=0, keepdims=True) * (1.0 / ATTN_WIDTH) + EPS)
    out_t = jnp.concatenate(heads, axis=0) * inv * nw_ref[...]
    o_ref[...] = out_t.T.astype(BF16)


def _window_bias():
    blk = ATT_BLOCK
    j = jnp.arange(blk)[:, None]
    i = jnp.arange(blk)[None, :]
    zero = jnp.zeros((blk, blk), F32)
    hidden = jnp.full((blk, blk), NEG_INF, F32)
    prev = jnp.where(j >= i, 0.0, NEG_INF).astype(F32)
    nxt = jnp.where(j <= i, 0.0, NEG_INF).astype(F32)
    variants = []
    for v in range(4):
        variants.append(jnp.concatenate([hidden if v & 1 else prev, zero, hidden if v & 2 else nxt], axis=0))
    variants.append(jnp.concatenate([hidden, hidden, hidden], axis=0))
    return jnp.stack(variants)


def _attention(q, k, vt, sinks, nw, *, batch, seq, n_ctx, with_ctx_queries):
    nt = q.shape[0]
    blk = ATT_BLOCK
    nb = seq // blk
    ncq = n_ctx // blk
    nq = nb + (ncq if with_ctx_queries else 0)
    ctx_blk0 = (batch * seq) // n_ctx

    def qmap(b, n, s):
        return (jnp.where(n < nb, b * nb + n, batch * nb + b * ncq + (n - nb)), 0)

    def win(off):
        return lambda b, n, s: b * nb + jnp.clip(n + off, 0, nb - 1)

    def bias_map(b, n, s):
        edge = (n == 0).astype(I32) + 2 * (n == nb - 1).astype(I32)
        return (jnp.where(n < nb, edge, 4), 0, 0)

    k_spec = lambda off: pl.BlockSpec((blk, KV_WIDTH), lambda b, n, s: (win(off)(b, n, s), 0))
    v_spec = lambda off: pl.BlockSpec((KV_WIDTH, blk), lambda b, n, s: (0, win(off)(b, n, s)))
    nw_b = jnp.broadcast_to(nw.reshape(ATTN_WIDTH, 1), (ATTN_WIDTH, blk))
    return pl.pallas_call(
        _attn_kernel,
        out_shape=jax.ShapeDtypeStruct((nt if with_ctx_queries else batch * seq, ATTN_WIDTH), BF16),
        grid_spec=pltpu.PrefetchScalarGridSpec(
            num_scalar_prefetch=1,
            grid=(batch, nq),
            in_specs=[pl.BlockSpec((blk, ATTN_WIDTH), qmap),
                      k_spec(-1), k_spec(0), k_spec(1),
                      v_spec(-1), v_spec(0), v_spec(1),
                      pl.BlockSpec((n_ctx, KV_WIDTH), lambda b, n, s: (ctx_blk0 + b, 0)),
                      pl.BlockSpec((KV_WIDTH, n_ctx), lambda b, n, s: (0, ctx_blk0 + b)),
                      pl.BlockSpec((None, 3 * blk, blk), bias_map),
                      pl.BlockSpec((ATTN_WIDTH, blk), lambda b, n, s: (0, 0))],
            out_specs=pl.BlockSpec((blk, ATTN_WIDTH), qmap)),
        compiler_params=_params(2),
        name="attention",
    )(sinks, q, k, k, k, vt, vt, vt, k, vt, _window_bias(), nw_b)


def _conv_kernel(xp_ref, xc_ref, xn_ref, w_ref, b_ref, o_ref, ext_ref, *, seq, n_lat, n_ctx):
    i = pl.program_id(0)
    tb = xc_ref.shape[0]
    row0 = i * tb
    in_lat = row0 < n_lat
    local = jnp.where(in_lat, row0 % seq, (row0 - n_lat) % n_ctx)
    length = jnp.where(in_lat, seq, n_ctx)
    first = local == 0
    last = local + tb == length
    h = SUBLANES
    ext_ref[0:h, :] = jnp.where(first, 0.0, xp_ref[...])
    ext_ref[h:h + tb, :] = xc_ref[...]
    ext_ref[h + tb:h + tb + h, :] = jnp.where(last, 0.0, xn_ref[...])
    acc = jnp.zeros((tb, CONV_CH), F32) + b_ref[...]
    for kk in range(CONV_W):
        off = h - CONV_W // 2 + kk
        acc = acc + ext_ref[off:off + tb, :] * w_ref[kk:kk + 1, :]
    o_ref[...] = acc * _sigmoid(acc)


def _conv_silu(xbc, conv_w, conv_b, *, seq, n_lat, n_ctx):
    nt = xbc.shape[0]
    tb = _pick(math.gcd(seq, n_ctx), (256, 128))
    h = SUBLANES
    per = tb // h
    n_h = nt // h
    kern = functools.partial(_conv_kernel, seq=seq, n_lat=n_lat, n_ctx=n_ctx)
    return pl.pallas_call(
        kern,
        out_shape=jax.ShapeDtypeStruct((nt, CONV_CH), F32),
        grid=(nt // tb,),
        in_specs=[pl.BlockSpec((h, CONV_CH), lambda i: (jnp.maximum(i * per - 1, 0), 0)),
                  pl.BlockSpec((tb, CONV_CH), lambda i: (i, 0)),
                  pl.BlockSpec((h, CONV_CH), lambda i: (jnp.minimum((i + 1) * per, n_h - 1), 0)),
                  pl.BlockSpec((CONV_W, CONV_CH), lambda i: (0, 0)),
                  pl.BlockSpec((1, CONV_CH), lambda i: (0, 0))],
        out_specs=pl.BlockSpec((tb, CONV_CH), lambda i: (i, 0)),
        scratch_shapes=[pltpu.VMEM((tb + 2 * h, CONV_CH), F32)],
        compiler_params=_params(1),
        name="conv_silu",
    )(xbc, xbc, xbc, conv_w, conv_b.reshape(1, CONV_CH))


def _ssd_chunk(u, dtraw, dtb, alog, state_ref, *, direction):
    q = CHUNK
    hp = SSM_INNER // SSM_HEADS
    per_g = SSM_HEADS // SSM_GROUPS
    gw = per_g * hp
    xs = u[:, :SSM_INNER]
    bm = u[:, SSM_INNER:SSM_INNER + SSM_GROUPS * SSM_STATE]
    cm = u[:, SSM_INNER + SSM_GROUPS * SSM_STATE:]

    xv = dtraw + dtb
    dt = jnp.maximum(xv, 0.0) + jnp.log1p(jnp.exp(-jnp.abs(xv)))
    dta = dt * (-jnp.exp(alog))

    ri = lax.broadcasted_iota(I32, (q, q), 0)
    ci = lax.broadcasted_iota(I32, (q, q), 1)
    tri = (ci <= ri) if direction == 0 else (ci >= ri)
    cs = _dot3(dta, tri.astype(BF16), left=True)
    cs_t = cs.T

    er = lax.broadcasted_iota(I32, (LANES, SSM_INNER), 0)
    ec = lax.broadcasted_iota(I32, (LANES, SSM_INNER), 1)
    expand = (er == direction * SSM_HEADS + jnp.right_shift(ec, hp.bit_length() - 1)).astype(BF16)
    cs_e = _dot3(cs, expand)
    dt_e = _dot3(dt, expand)
    last = q - 1 if direction == 0 else 0
    cs_last = cs_e[last:last + 1, :]

    xdt = (xs * dt_e).astype(BF16)
    xw = (xs * (jnp.exp(cs_last - cs_e) * dt_e)).astype(BF16)
    bm_t = bm.T.astype(BF16)
    cmb = cm.astype(BF16)
    bmb = bm.astype(BF16)
    state = state_ref[...]
    state_b = state.astype(BF16)

    y_diag = []
    y_off = []
    new_states = []
    for g in range(SSM_GROUPS):
        gs = slice(g * SSM_STATE, (g + 1) * SSM_STATE)
        cb = lax.dot_general(cmb[:, gs], bmb[:, gs], (((1,), (1,)), ((), ())), preferred_element_type=F32)
        y_off.append(jnp.dot(cmb[:, gs], state_b[:, g * gw:(g + 1) * gw], preferred_element_type=F32))
        new_states.append(jnp.dot(bm_t[gs, :], xw[:, g * gw:(g + 1) * gw], preferred_element_type=F32))
        for j in range(per_g):
            hh = g * per_g + j
            col = direction * SSM_HEADS + hh
            seg = cs[:, col:col + 1] - cs_t[col:col + 1, :]
            decay = jnp.exp(jnp.where(tri, seg, NEG_INF))
            scores = (cb * decay).astype(BF16)
            y_diag.append(jnp.dot(scores, xdt[:, hh * hp:(hh + 1) * hp], preferred_element_type=F32))
    y = jnp.concatenate(y_diag, axis=1) + jnp.exp(cs_e) * jnp.concatenate(y_off, axis=1)
    state_ref[...] = jnp.exp(cs_last) * state + jnp.concatenate(new_states, axis=1)
    return y


SSD_CHUNKS_PER_STEP = 2


def _ssd_fwd_kernel(u_ref, dt_ref, dtb_ref, alog_ref, skip_ref, y_ref, state_ref):
    @pl.when(pl.program_id(1) == 0)
    def _():
        state_ref[...] = jnp.zeros_like(state_ref)

    for j in range(u_ref.shape[0] // CHUNK):
        rows = slice(j * CHUNK, (j + 1) * CHUNK)
        u = u_ref[rows, :]
        y = _ssd_chunk(u, dt_ref[rows, :], dtb_ref[...], alog_ref[...], state_ref, direction=0)
        y_ref[rows, :] = y + skip_ref[...] * u[:, :SSM_INNER]


def _ssd_bwd_kernel(u_ref, dt_ref, dtb_ref, alog_ref, y0_ref, z_ref, nw_ref, o_ref, state_ref):
    @pl.when(pl.program_id(1) == 0)
    def _():
        state_ref[...] = jnp.zeros_like(state_ref)

    n = u_ref.shape[0] // CHUNK
    for j in range(n - 1, -1, -1):
        rows = slice(j * CHUNK, (j + 1) * CHUNK)
        y = y0_ref[rows, :] + _ssd_chunk(u_ref[rows, :], dt_ref[rows, :], dtb_ref[...], alog_ref[...],
                                         state_ref, direction=1)
        z = z_ref[rows, :]
        gt = y * (z * _sigmoid(z))
        gw = SSM_INNER // SSM_GROUPS
        outs = []
        for g in range(SSM_GROUPS):
            gg = gt[:, g * gw:(g + 1) * gw]
            ms = jnp.mean(gg * gg, axis=-1, keepdims=True)
            outs.append(gg * lax.rsqrt(ms + EPS))
        o_ref[rows, :] = (jnp.concatenate(outs, axis=1) * nw_ref[...]).astype(BF16)


def _ssd(u, dt_raw, z, dtb, alog, skip, ssm_nw, *, batch, seq, n_ctx):
    nt = u.shape[0]
    cps = SSD_CHUNKS_PER_STEP
    q = cps * CHUNK
    assert seq % q == 0 and n_ctx % q == 0
    ncl = seq // q
    ncc = n_ctx // q
    steps = ncc + ncl
    ctx0 = (batch * seq) // q

    def fmap(b, t):
        return (jnp.where(t < ncc, ctx0 + b * ncc + t, b * ncl + (t - ncc)), 0)

    def rmap(b, t):
        return (jnp.where(t < ncc, ctx0 + b * ncc + (ncc - 1 - t), b * ncl + (ncl - 1 - (t - ncc))), 0)

    const = lambda b, t: (0, 0)
    state = pltpu.VMEM((SSM_STATE, SSM_INNER), F32)
    y0 = pl.pallas_call(
        _ssd_fwd_kernel,
        out_shape=jax.ShapeDtypeStruct((nt, SSM_INNER), F32),
        grid=(batch, steps),
        in_specs=[pl.BlockSpec((q, CONV_CH), fmap), pl.BlockSpec((q, LANES), fmap),
                  pl.BlockSpec((1, LANES), const), pl.BlockSpec((1, LANES), const),
                  pl.BlockSpec((1, SSM_INNER), const)],
        out_specs=pl.BlockSpec((q, SSM_INNER), fmap),
        scratch_shapes=[state],
        compiler_params=_params(2),
        name="ssd_forward",
    )(u, dt_raw, dtb, alog, skip)
    return pl.pallas_call(
        _ssd_bwd_kernel,
        out_shape=jax.ShapeDtypeStruct((nt, SSM_INNER), BF16),
        grid=(batch, steps),
        in_specs=[pl.BlockSpec((q, CONV_CH), rmap), pl.BlockSpec((q, LANES), rmap),
                  pl.BlockSpec((1, LANES), const), pl.BlockSpec((1, LANES), const),
                  pl.BlockSpec((q, SSM_INNER), rmap), pl.BlockSpec((q, SSM_INNER), rmap),
                  pl.BlockSpec((1, SSM_INNER), const)],
        out_specs=pl.BlockSpec((q, SSM_INNER), rmap),
        scratch_shapes=[state],
        compiler_params=_params(2),
        name="ssd_backward",
    )(u, dt_raw, dtb, alog, y0, z, ssm_nw)


MOE_TILE = 256


def _out_proj_kernel(xl_ref, xc_ref, a_ref, s_ref, wa_ref, ws_ref, mod_ref, nw_ref, wr_ref, br_ref,
                     xo_ref, h_ref, idxt_ref, gate_ref, cnt_ref, *, n_lat_tiles):
    mix = (jnp.dot(a_ref[...], wa_ref[...], preferred_element_type=F32)
           + jnp.dot(s_ref[...], ws_ref[...], preferred_element_type=F32))
    x_in = jnp.where(pl.program_id(0) < n_lat_tiles, xl_ref[...], xc_ref[...])
    x = x_in + mod_ref[0, 2:3, :] * mix
    xo_ref[...] = x
    h = _rms_mod(x, nw_ref[...], mod_ref[0, 3:4, :], mod_ref[0, 4:5, :])
    h_hi = h.astype(BF16)
    h_ref[...] = h_hi
    h_lo = (h - h_hi.astype(F32)).astype(BF16)
    hh = jnp.dot(h_hi, wr_ref[...], preferred_element_type=F32)
    lh = jnp.dot(h_lo, wr_ref[:, :LANES], preferred_element_type=F32)
    logits = hh[:, :LANES] + hh[:, LANES:] + lh + br_ref[...]
    tm = logits.shape[0]
    lane = lax.broadcasted_iota(I32, (tm, LANES), 1)
    lane_f = lane.astype(F32)
    work = jnp.where(lane < N_EXPERTS, logits, -jnp.inf)
    idx_out = jnp.zeros((tm, LANES), F32)
    val_out = jnp.full((tm, LANES), -jnp.inf, F32)
    onehot = jnp.zeros((tm, LANES), F32)
    for kk in range(TOP_K):
        m = jnp.max(work, axis=-1, keepdims=True)
        sel = jnp.min(jnp.where(work == m, lane_f, float(LANES)), axis=-1, keepdims=True)
        idx_out = jnp.where(lane == kk, sel, idx_out)
        val_out = jnp.where(lane == kk, m, val_out)
        picked = lane_f == sel
        onehot = onehot + jnp.where(picked, 1.0, 0.0)
        work = jnp.where(picked, -jnp.inf, work)
    top = jnp.max(val_out, axis=-1, keepdims=True)
    e = jnp.exp(val_out - top)
    gate_ref[...] = e / jnp.sum(e, axis=-1, keepdims=True)
    idxt_ref[...] = idx_out.T[0:SUBLANES, :].astype(I32)
    ones = jnp.ones((SUBLANES, MOE_TILE), BF16)
    oh_b = onehot.astype(BF16)
    for j in range(tm // MOE_TILE):
        cnt_ref[j] = jnp.dot(ones, oh_b[j * MOE_TILE:(j + 1) * MOE_TILE, :],
                             preferred_element_type=F32).astype(I32)


def _out_proj(x_lat, x_ctx, ctx_row0, attn, ssm, w_out_a, w_out_s, mod_l, nfw, wr_hl, br_p,
              *, n_rows, n_lat, seq, batch):
    d = x_lat.shape[1]
    tm = _pick(math.gcd(seq, n_rows - n_lat) if n_rows > n_lat else seq, (512, 256))
    sub = tm // MOE_TILE
    row = lambda i: (i, 0)
    const = lambda i: (0, 0)
    kern = functools.partial(_out_proj_kernel, n_lat_tiles=n_lat // tm)
    return pl.pallas_call(
        kern,
        out_shape=(jax.ShapeDtypeStruct((n_rows, d), F32), jax.ShapeDtypeStruct((n_rows, d), BF16),
                   jax.ShapeDtypeStruct((SUBLANES, n_rows), I32), jax.ShapeDtypeStruct((n_rows, LANES), F32),
                   jax.ShapeDtypeStruct((n_rows // MOE_TILE, SUBLANES, LANES), I32)),
        grid=(n_rows // tm,),
        in_specs=_token_specs(tm, d, n_lat // tm, ctx_row0 // tm) + [
                  pl.BlockSpec((tm, ATTN_WIDTH), row), pl.BlockSpec((tm, SSM_INNER), row),
                  pl.BlockSpec((ATTN_WIDTH, d), const), pl.BlockSpec((SSM_INNER, d), const),
                  pl.BlockSpec((1, N_MOD, d), lambda i: (jnp.minimum(i * tm // seq, batch), 0, 0)),
                  pl.BlockSpec((1, d), const), pl.BlockSpec((d, 2 * LANES), const),
                  pl.BlockSpec((1, LANES), const)],
        out_specs=(pl.BlockSpec((tm, d), row), pl.BlockSpec((tm, d), row),
                   pl.BlockSpec((SUBLANES, tm), lambda i: (0, i)), pl.BlockSpec((tm, LANES), row),
                   pl.BlockSpec((sub, SUBLANES, LANES), lambda i: (i, 0, 0))),
        compiler_params=_params(1),
        name="out_proj_router",
    )(x_lat, x_ctx, attn, ssm, w_out_a, w_out_s, mod_l, nfw, wr_hl, br_p)


TAB_CNT, TAB_OFF, TAB_BASE = 0, N_EXPERTS, 2 * N_EXPERTS
ROW_TILE = D_MODEL // LANES
assert ROW_TILE == SUBLANES


def _to_row_tiles(ref, base, val):
    n = val.shape[0]
    for s in range(ROW_TILE):
        ref[pl.ds(base + s, n, stride=ROW_TILE), :] = val[:, s * LANES:(s + 1) * LANES]


def _from_row_tiles(ref, base, n):
    return jnp.concatenate([ref[pl.ds(base + s, n, stride=ROW_TILE), :] for s in range(ROW_TILE)], axis=1)


def _tile_rows(start, size):
    return pl.ds(pl.multiple_of(start * ROW_TILE, ROW_TILE), size * ROW_TILE)


def _for_each_run(tab_ref, lanes, make_copy, fn, *, enabled=None, unrolled=False):
    def per_expert(e, carry=0):
        cnt = tab_ref[0, lanes[0] + e]
        if enabled is not None:
            cnt = jnp.where(enabled, cnt, 0)

        @pl.when(cnt > 0)
        def _():
            fn(make_copy(tab_ref[0, lanes[1] + e], tab_ref[0, lanes[2] + e], cnt))
        return carry

    if unrolled:
        for e in range(N_EXPERTS):
            per_expert(e)
    else:
        lax.fori_loop(0, N_EXPERTS, per_expert, 0)


def _start(copy):
    copy.start()


def _wait(copy):
    copy.wait()


def _dispatch_kernel(tab_ref, tabp_ref, pad_ref, idxt_ref, h_ref, buf_ref, pos_ref,
                     scr_ref, zero_ref, sem, zsem, *, n_tiles):
    i = pl.program_id(0)
    slot = i % 2
    t = h_ref.shape[0]
    rows = TOP_K * t
    run_lanes = (TAB_CNT, TAB_OFF, TAB_BASE)

    def copy_out(s):
        return lambda local, glob, size: pltpu.make_async_copy(
            scr_ref.at[_tile_rows(s * rows + local, size), :], buf_ref.at[_tile_rows(glob, size), :], sem.at[s])

    @pl.when(i == 0)
    def _():
        zero_ref[...] = jnp.zeros_like(zero_ref)
        zero_copy = lambda local, glob, size: pltpu.make_async_copy(
            zero_ref.at[_tile_rows(0, size), :], buf_ref.at[_tile_rows(glob, size), :], zsem)
        pad_lanes = (0, 0, N_EXPERTS)
        _for_each_run(pad_ref, pad_lanes, zero_copy, _start)
        _for_each_run(pad_ref, pad_lanes, zero_copy, _wait)
        tail_start = pad_ref[0, 2 * N_EXPERTS]
        n_tail = pad_ref[0, 2 * N_EXPERTS + 1]
        zrows = zero_ref.shape[0] // ROW_TILE

        def tail(fn):
            def body(j, carry):
                fn(zero_copy(0, tail_start + j * zrows, zrows))
                return carry
            return body

        lax.fori_loop(0, n_tail, tail(_start), 0)
        lax.fori_loop(0, n_tail, tail(_wait), 0)

    e_iota = lax.broadcasted_iota(I32, (N_EXPERTS, t), 0)
    upper = (lax.broadcasted_iota(I32, (t, t), 0) < lax.broadcasted_iota(I32, (t, t), 1)).astype(BF16)
    onehots, counts, before = [], [], []
    for kk in range(TOP_K):
        oh = jnp.where(e_iota == idxt_ref[kk:kk + 1, :], 1.0, 0.0)
        onehots.append(oh)
        counts.append(jnp.sum(oh, axis=1, keepdims=True))
        before.append(jnp.dot(oh.astype(BF16), upper, preferred_element_type=F32))
    total = counts[0] + counts[1] + counts[2] + counts[3]
    below = (lax.broadcasted_iota(I32, (N_EXPERTS, N_EXPERTS), 1)
             < lax.broadcasted_iota(I32, (N_EXPERTS, N_EXPERTS), 0)).astype(BF16)
    start = _dot3(jnp.broadcast_to(total, (N_EXPERTS, LANES)), below, left=True)[:, 0:1]
    pos_rows = []
    for kk in range(TOP_K):
        pos_rows.append(jnp.sum(onehots[kk] * (before[kk] + start), axis=0, keepdims=True))
        start = start + counts[kk]
    r_iota = lax.broadcasted_iota(I32, (rows, t), 0)
    perm = jnp.zeros((rows, t), F32)
    for kk in range(TOP_K):
        perm = jnp.where(r_iota == pos_rows[kk].astype(I32), 1.0, perm)
    _to_row_tiles(scr_ref, slot * rows * ROW_TILE,
                  jnp.dot(perm.astype(BF16), h_ref[...], preferred_element_type=F32))
    pos_t = jnp.concatenate(pos_rows + [jnp.zeros((LANES - TOP_K, t), F32)], axis=0)
    pos_ref[...] = pos_t.T.astype(I32)

    _for_each_run(tab_ref, run_lanes, copy_out(slot), _start, unrolled=True)
    _for_each_run(tabp_ref, run_lanes, copy_out(1 - slot), _wait, enabled=i > 0, unrolled=True)

    @pl.when(i == n_tiles - 1)
    def _():
        _for_each_run(tab_ref, run_lanes, copy_out(slot), _wait)


def _dispatch(h2, idxt, table, padtab, *, n_rows, tme):
    nt, d = h2.shape
    t = MOE_TILE
    n_tiles = nt // t
    smem = lambda imap: pl.BlockSpec((None, 1, LANES), imap, memory_space=pltpu.SMEM)
    kern = functools.partial(_dispatch_kernel, n_tiles=n_tiles)
    return pl.pallas_call(
        kern,
        out_shape=(jax.ShapeDtypeStruct((n_rows * ROW_TILE, LANES), F32),
                   jax.ShapeDtypeStruct((nt, LANES), I32)),
        grid=(n_tiles,),
        in_specs=[smem(lambda i: (i, 0, 0)), smem(lambda i: (jnp.maximum(i - 1, 0), 0, 0)),
                  smem(lambda i: (0, 0, 0)),
                  pl.BlockSpec((SUBLANES, t), lambda i: (0, i)),
                  pl.BlockSpec((t, d), lambda i: (i, 0))],
        out_specs=(pl.BlockSpec(memory_space=pl.ANY), pl.BlockSpec((t, LANES), lambda i: (i, 0))),
        scratch_shapes=[pltpu.VMEM((2 * TOP_K * t * ROW_TILE, LANES), F32),
                        pltpu.VMEM((tme * ROW_TILE, LANES), F32),
                        pltpu.SemaphoreType.DMA((2,)), pltpu.SemaphoreType.DMA],
        compiler_params=_params(1),
        name="moe_dispatch",
    )(table, table, padtab, idxt, h2)


def _expert_kernel(be_ref, nu_ref, nx_ref, x_hbm, wgu_hbm, bgu_ref, wd_hbm, bd_ref, o_hbm,
                   xbuf, obuf, wgu_f, wd_f, wgu_b, wd_b, xsem, osem, wsem, *, layer):
    i = pl.program_id(0)
    n_used = nu_ref[0]
    tme = xbuf.shape[1]
    slot = i % 2

    def x_copies(blk, s):
        return [pltpu.make_async_copy(x_hbm.at[pl.ds(blk * tme, tme), j, :],
                                      xbuf.at[s, :, j * LANES:(j + 1) * LANES], xsem.at[s])
                for j in range(ROW_TILE)]

    def o_copies(blk, s):
        return [pltpu.make_async_copy(obuf.at[s, :, j * LANES:(j + 1) * LANES],
                                      o_hbm.at[pl.ds(blk * tme, tme), j, :], osem.at[s])
                for j in range(ROW_TILE)]

    def w_copies(e):
        return [pltpu.make_async_copy(wgu_hbm.at[layer, e], wgu_f, wsem.at[0]),
                pltpu.make_async_copy(wd_hbm.at[layer, e], wd_f, wsem.at[1])]

    @pl.when(i < n_used)
    def _():
        e = be_ref[i]

        @pl.when(i == 0)
        def _():
            for c in x_copies(i, slot) + w_copies(e):
                c.start()

        @pl.when(i + 1 < n_used)
        def _():
            for c in x_copies(i + 1, 1 - slot):
                c.start()

        @pl.when((i == 0) | (e != be_ref[jnp.maximum(i - 1, 0)]))
        def _():
            for c in w_copies(e):
                c.wait()
            wgu_b[...] = wgu_f[...].astype(BF16)
            wd_b[...] = wd_f[...].astype(BF16)
            nxt = nx_ref[i]

            @pl.when(nxt >= 0)
            def _():
                for c in w_copies(nxt):
                    c.start()

        @pl.when(i >= 2)
        def _():
            for c in o_copies(i - 2, slot):
                c.wait()

        for c in x_copies(i, slot):
            c.wait()
        gu = jnp.dot(xbuf[slot].astype(BF16), wgu_b[...], preferred_element_type=F32) + bgu_ref[...]
        glu = jnp.minimum(gu[:, :D_FF], SWIGLU_LIMIT)
        lin = jnp.clip(gu[:, D_FF:], -SWIGLU_LIMIT, SWIGLU_LIMIT)
        act = glu * _sigmoid(SWIGLU_ALPHA * glu) * (lin + 1.0)
        obuf[slot] = jnp.dot(act.astype(BF16), wd_b[...], preferred_element_type=F32) + bd_ref[...]
        for c in o_copies(i, slot):
            c.start()

        @pl.when(i == n_used - 1)
        def _():
            for c in o_copies(i, slot):
                c.wait()

            @pl.when(i >= 1)
            def _():
                for c in o_copies(i - 1, 1 - slot):
                    c.wait()

    @pl.when(i >= n_used)
    def _():
        obuf[slot] = jnp.zeros((tme, D_MODEL), F32)
        for c in o_copies(i, slot):
            c.start()
        for c in o_copies(i, slot):
            c.wait()


def _experts(buf, block_e, n_used, next_e, wgu, bgu, wd, bd, *, layer, tme):
    d = D_MODEL
    n_rows = buf.shape[0] // ROW_TILE
    emap = lambda i, be, nu, nx: (layer, be[i], 0, 0)
    kern = functools.partial(_expert_kernel, layer=layer)
    out = pl.pallas_call(
        kern,
        out_shape=jax.ShapeDtypeStruct((n_rows, ROW_TILE, LANES), F32),
        grid_spec=pltpu.PrefetchScalarGridSpec(
            num_scalar_prefetch=3,
            grid=(n_rows // tme,),
            in_specs=[pl.BlockSpec(memory_space=pl.ANY),
                      pl.BlockSpec(memory_space=pl.ANY),
                      pl.BlockSpec((None, None, 1, 2 * D_FF), emap),
                      pl.BlockSpec(memory_space=pl.ANY),
                      pl.BlockSpec((None, None, 1, d), emap)],
            out_specs=pl.BlockSpec(memory_space=pl.ANY),
            scratch_shapes=[pltpu.VMEM((2, tme, d), F32), pltpu.VMEM((2, tme, d), F32),
                            pltpu.VMEM((d, 2 * D_FF), F32), pltpu.VMEM((D_FF, d), F32),
                            pltpu.VMEM((d, 2 * D_FF), BF16), pltpu.VMEM((D_FF, d), BF16),
                            pltpu.SemaphoreType.DMA((2,)), pltpu.SemaphoreType.DMA((2,)),
                            pltpu.SemaphoreType.DMA((2,))]),
        compiler_params=_params(1),
        name="expert_ffn",
    )(block_e, n_used, next_e, buf.reshape(n_rows, ROW_TILE, LANES), wgu, bgu, wd, bd)
    return out.reshape(n_rows * ROW_TILE, LANES)


def _combine_kernel(tab_ref, tabn_ref, pos_ref, gate_ref, x_ref, mod_ref, fw_ref, ob_ref, o_ref,
                    scr_ref, sem, *, n_tiles, final_norm):
    i = pl.program_id(0)
    slot = i % 2
    t = x_ref.shape[0]
    rows = TOP_K * t
    run_lanes = (TAB_CNT, TAB_OFF, TAB_BASE)

    def copy_in(s):
        return lambda local, glob, size: pltpu.make_async_copy(
            ob_ref.at[_tile_rows(glob, size), :], scr_ref.at[_tile_rows(s * rows + local, size), :], sem.at[s])

    @pl.when(i == 0)
    def _():
        _for_each_run(tab_ref, run_lanes, copy_in(slot), _start)

    _for_each_run(tabn_ref, run_lanes, copy_in(1 - slot), _start, enabled=i + 1 < n_tiles, unrolled=True)

    lane = lax.broadcasted_iota(I32, (t, rows), 1)
    pw = jnp.zeros((t, rows), F32)
    for kk in range(TOP_K):
        pw = jnp.where(lane == pos_ref[:, kk:kk + 1], gate_ref[:, kk:kk + 1], pw)
    pw_hi = pw.astype(BF16)
    pw_lo = (pw - pw_hi.astype(F32)).astype(BF16)

    _for_each_run(tab_ref, run_lanes, copy_in(slot), _wait, unrolled=True)
    y = _from_row_tiles(scr_ref, slot * rows * ROW_TILE, rows).astype(BF16)
    f = jnp.dot(pw_hi, y, preferred_element_type=F32) + jnp.dot(pw_lo, y, preferred_element_type=F32)
    x = x_ref[...] + mod_ref[0, 5:6, :] * f
    if final_norm:
        ms = jnp.mean(x * x, axis=-1, keepdims=True)
        x = x * lax.rsqrt(ms + EPS) * fw_ref[...]
    o_ref[...] = x


def _combine(xt, out_buf, pos, gates, table, mod_l, fw, *, seq, batch, final_norm):
    nt, d = xt.shape
    t = MOE_TILE
    n_tiles = nt // t
    row = lambda i: (i, 0)
    smem = lambda imap: pl.BlockSpec((None, 1, LANES), imap, memory_space=pltpu.SMEM)
    kern = functools.partial(_combine_kernel, n_tiles=n_tiles, final_norm=final_norm)
    return pl.pallas_call(
        kern,
        out_shape=jax.ShapeDtypeStruct((nt, d), F32),
        grid=(n_tiles,),
        in_specs=[smem(lambda i: (i, 0, 0)), smem(lambda i: (jnp.minimum(i + 1, n_tiles - 1), 0, 0)),
                  pl.BlockSpec((t, LANES), row), pl.BlockSpec((t, LANES), row), pl.BlockSpec((t, d), row),
                  pl.BlockSpec((1, N_MOD, d), lambda i: (jnp.minimum(i * t // seq, batch), 0, 0)),
                  pl.BlockSpec((1, d), lambda i: (0, 0)),
                  pl.BlockSpec(memory_space=pl.ANY)],
        out_specs=pl.BlockSpec((t, d), row),
        scratch_shapes=[pltpu.VMEM((2 * TOP_K * t * ROW_TILE, LANES), F32), pltpu.SemaphoreType.DMA((2,))],
        compiler_params=_params(1),
        name="moe_combine",
    )(table, table, pos, gates, xt, mod_l, fw, out_buf)


def _route_tables(counts, *, tme, n_blocks):
    cnt = counts[:, 0, :N_EXPERTS]
    total = jnp.sum(cnt, axis=0)
    padded = (total + tme - 1) // tme * tme
    pad_end = jnp.cumsum(padded)
    pad_start = pad_end - padded
    base = pad_start[None, :] + jnp.cumsum(cnt, axis=0) - cnt
    off = jnp.cumsum(cnt, axis=1) - cnt
    table = jnp.concatenate([cnt, off, base, jnp.zeros_like(cnt)], axis=1).astype(I32)[:, None, :]
    tail = jnp.stack([pad_end[-1], n_blocks - pad_end[-1] // tme])
    padtab = jnp.concatenate([padded - total, pad_start + total, tail,
                              jnp.zeros((LANES - 2 * N_EXPERTS - 2,), I32)]).astype(I32)[None, None, :]
    n_used = (pad_end[-1] // tme).astype(I32).reshape(1)
    block_row0 = jnp.arange(n_blocks, dtype=I32) * tme
    block_e = jnp.minimum(jnp.sum((pad_end[None, :] <= block_row0[:, None]).astype(I32), axis=1),
                          N_EXPERTS - 1).astype(I32)
    ids = jnp.arange(N_EXPERTS, dtype=I32)
    later = (ids[None, :] > ids[:, None]) & (total[None, :] > 0)
    next_used = jnp.min(jnp.where(later, ids[None, :], N_EXPERTS), axis=1)
    next_used = jnp.where(next_used == N_EXPERTS, -1, next_used).astype(I32)
    return table, padtab, block_e, n_used, next_used[block_e]


def _rope_tables(seq):
    rows = seq // GRID_W
    row_pos = jnp.repeat(jnp.arange(rows, dtype=I32), GRID_W).astype(F32)
    col_pos = jnp.tile(jnp.arange(GRID_W, dtype=I32), rows).astype(F32)
    n_freq = HEAD_DIM // 4
    inv_freq = ROPE_BASE ** (-jnp.arange(n_freq, dtype=F32) / n_freq)
    lane = jnp.arange(LANES)
    f = lane % n_freq
    use_col = (lane % HEAD_DIM) >= HEAD_DIM // 2
    ang = jnp.where(use_col[None, :], col_pos[:, None], row_pos[:, None]) * inv_freq[f][None, :]
    first_half = (lane % 32) < 16
    return jnp.cos(ang), jnp.where(first_half[None, :], -jnp.sin(ang), jnp.sin(ang))


def kernel(x, c, ctx, c_ctx, w_ada, b_ada, norm_mix_w, norm_ffn_w, w_in, conv_w, conv_b, dt_bias, a_log, d_skip, ssm_norm_w, attn_sinks, attn_norm_w, w_out, w_router, b_router, w_gate_up, b_gate_up, w_down, b_down, final_norm_w):
    batch, seq, d = x.shape
    n_ctx = ctx.shape[1]
    depth = w_ada.shape[0]
    n_lat = batch * seq
    nt = n_lat + batch * n_ctx
    assert d == D_MODEL and seq % ATT_BLOCK == 0 and n_ctx % ATT_BLOCK == 0 and n_lat % n_ctx == 0

    r_mod = -(-(batch + 1) // SUBLANES) * SUBLANES
    c_all = jnp.zeros((r_mod, d), F32).at[:batch].set(c).at[batch].set(c_ctx)
    mod = _modulation(c_all, w_ada, b_ada)
    cos_t, sin_t = _rope_tables(seq)

    tme = 512
    assert nt % MOE_TILE == 0 and n_lat % MOE_TILE == 0
    hp = SSM_INNER // SSM_HEADS
    bgu = b_gate_up.reshape(depth, N_EXPERTS, 1, 2 * D_FF)
    bdn = b_down.reshape(depth, N_EXPERTS, 1, d)

    x_lat, x_ctx, ctx_row0 = x.reshape(n_lat, d), ctx.reshape(batch * n_ctx, d), 0
    for l in range(depth):
        last = l == depth - 1
        w_in_p = jnp.pad(w_in[l], ((0, 0), (0, IN_PAD - w_in.shape[2]))).astype(BF16)
        q, k, v, z, xbc, dt_raw = _in_proj(x_lat, x_ctx, ctx_row0, norm_mix_w[l].reshape(1, d), mod[l], w_in_p,
                                           cos_t, sin_t, nt=nt, n_lat=n_lat, seq=seq, batch=batch)
        attn = _attention(q, k, v, attn_sinks[l], attn_norm_w[l].reshape(1, ATTN_WIDTH),
                          batch=batch, seq=seq, n_ctx=n_ctx, with_ctx_queries=not last)
        u = _conv_silu(xbc, conv_w[l], conv_b[l], seq=seq, n_lat=n_lat, n_ctx=n_ctx)
        pad16 = lambda t: jnp.pad(t.reshape(1, N_DIRS * SSM_HEADS), ((0, 0), (0, LANES - N_DIRS * SSM_HEADS)))
        ssm = _ssd(u, dt_raw, z, pad16(dt_bias[l]), pad16(a_log[l]),
                   jnp.repeat(d_skip[l], hp).reshape(1, SSM_INNER), ssm_norm_w[l].reshape(1, SSM_INNER),
                   batch=batch, seq=seq, n_ctx=n_ctx)
        n_rows = n_lat if last else nt
        n_blocks = -(-(n_rows * TOP_K) // tme) + N_EXPERTS
        w_o = w_out[l].astype(BF16)
        wr_p = jnp.pad(w_router[l], ((0, 0), (0, LANES - N_EXPERTS)))
        wr_hi = wr_p.astype(BF16)
        wr_hl = jnp.concatenate([wr_hi, (wr_p - wr_hi.astype(F32)).astype(BF16)], axis=1)
        br_p = jnp.pad(b_router[l].reshape(1, N_EXPERTS), ((0, 0), (0, LANES - N_EXPERTS)))
        xt, h2, idxt, gates, counts = _out_proj(x_lat, x_ctx, ctx_row0, attn, ssm, w_o[:ATTN_WIDTH],
                                                w_o[ATTN_WIDTH:], mod[l], norm_ffn_w[l].reshape(1, d),
                                                wr_hl, br_p,
                                                n_rows=n_rows, n_lat=n_lat, seq=seq, batch=batch)
        table, padtab, block_e, n_used, next_e = _route_tables(counts, tme=tme, n_blocks=n_blocks)
        buf, pos = _dispatch(h2, idxt, table, padtab, n_rows=n_blocks * tme, tme=tme)
        out_buf = _experts(buf, block_e, n_used, next_e, w_gate_up, bgu, w_down, bdn, layer=l, tme=tme)
        xt = _combine(xt, out_buf, pos, gates, table, mod[l], final_norm_w.reshape(1, d),
                      seq=seq, batch=batch, final_norm=last)
        x_lat, x_ctx, ctx_row0 = xt, xt, n_lat
    return xt.reshape(batch, seq, d)
```

```python
import functools
import math

import jax
import jax.numpy as jnp
from jax import lax
from jax.experimental import pallas as pl
from jax.experimental.pallas import tpu as pltpu

F32 = jnp.float32
BF16 = jnp.bfloat16
I32 = jnp.int32

D_MODEL = 1024
GRID_W = 64
N_MOD = 6
EPS = 1e-6
NEG_INF = -1e30

HEAD_DIM = 64
ATTN_WIDTH = 512
N_HEADS = 8
N_KV_HEADS = 2
KV_WIDTH = 128
ATT_BLOCK = 128
ROPE_BASE = 10000.0

SSM_INNER = 512
SSM_HEADS = 8
SSM_GROUPS = 2
SSM_STATE = 64
CONV_W = 5
CONV_CH = 768
CHUNK = 128
N_DIRS = 2

N_EXPERTS = 32
TOP_K = 4
D_FF = 1024
SWIGLU_LIMIT = 7.0
SWIGLU_ALPHA = 1.702

LANES = 128
SUBLANES = 8
IN_PAD = 2176
VMEM_LIMIT = 56 * 1024 * 1024
HIGHEST = lax.Precision.HIGHEST


def _params(n_axes, vmem=VMEM_LIMIT):
    return pltpu.CompilerParams(dimension_semantics=("arbitrary",) * n_axes, vmem_limit_bytes=vmem)


def _pick(n, prefs):
    for t in prefs:
        if n % t == 0:
            return t
    raise ValueError(f"no tile in {prefs} divides {n}")


def _sigmoid(x):
    return 1.0 / (1.0 + jnp.exp(-x))


def _dot3(x, m_bf16, left=False):
    hi = x.astype(BF16)
    r1 = x - hi.astype(F32)
    mid = r1.astype(BF16)
    lo = (r1 - mid.astype(F32)).astype(BF16)
    mm = (lambda p: jnp.dot(m_bf16, p, preferred_element_type=F32)) if left else (
        lambda p: jnp.dot(p, m_bf16, preferred_element_type=F32))
    return mm(hi) + mm(mid) + mm(lo)


def _mod_kernel(c_ref, w_ref, b_ref, o_ref):
    c = c_ref[...]
    s = c * _sigmoid(c)
    o_ref[...] = jnp.dot(s, w_ref[...], preferred_element_type=F32, precision=HIGHEST) + b_ref[...]


def _modulation(c_all, w_ada, b_ada):
    depth, d, n = w_ada.shape
    r = c_all.shape[0]
    tn = _pick(n, (1536, 1024, 512, 128))
    out = pl.pallas_call(
        _mod_kernel,
        out_shape=jax.ShapeDtypeStruct((depth, r, n), F32),
        grid=(depth, n // tn),
        in_specs=[pl.BlockSpec((r, d), lambda l, j: (0, 0)),
                  pl.BlockSpec((None, d, tn), lambda l, j: (l, 0, j)),
                  pl.BlockSpec((None, 1, tn), lambda l, j: (l, 0, j))],
        out_specs=pl.BlockSpec((None, r, tn), lambda l, j: (l, 0, j)),
        compiler_params=_params(2),
        name="adaln_mod",
    )(c_all, w_ada, b_ada.reshape(depth, 1, n))
    return out.reshape(depth, r, N_MOD, d)


def _rms_mod(x, nw, shift, scale):
    ms = jnp.mean(x * x, axis=-1, keepdims=True)
    y = x * lax.rsqrt(ms + EPS) * nw
    return y * (1.0 + scale) + shift


def _in_proj_kernel(xl_ref, xc_ref, nw_ref, mod_ref, w_ref, cos_ref, sin_ref,
                    q_ref, k_ref, v_ref, z_ref, xbc_ref, dt_ref, *, n_lat_tiles):
    i = pl.program_id(0)
    is_lat = i < n_lat_tiles
    x = jnp.where(is_lat, xl_ref[...], xc_ref[...])
    h = _rms_mod(x, nw_ref[...], mod_ref[0, 0:1, :], mod_ref[0, 1:2, :])
    p = jnp.dot(h.astype(BF16), w_ref[...], preferred_element_type=F32)
    tm = p.shape[0]
    cos = jnp.where(is_lat, cos_ref[...], 1.0)
    sin = jnp.where(is_lat, sin_ref[...], 0.0)
    lane = lax.broadcasted_iota(I32, (tm, LANES), 1)
    first_half = (lane & 31) < 16

    def rope(t):
        partner = jnp.where(first_half, pltpu.roll(t, LANES - 16, 1), pltpu.roll(t, 16, 1))
        return t * cos + partner * sin

    for j in range(ATTN_WIDTH // LANES):
        q_ref[:, j * LANES:(j + 1) * LANES] = rope(p[:, j * LANES:(j + 1) * LANES]).astype(BF16)
    k_ref[...] = rope(p[:, 512:640]).astype(BF16)
    v_ref[...] = p[:, 640:768].T.astype(BF16)
    z_ref[...] = p[:, 768:1280]
    xbc_ref[...] = p[:, 1280:2048]
    dt_ref[...] = p[:, 2048:2176]


def _token_specs(tm, d, n_lat_tiles, ctx_tile0):
    return [pl.BlockSpec((tm, d), lambda i: (jnp.minimum(i, n_lat_tiles - 1), 0)),
            pl.BlockSpec((tm, d), lambda i: (jnp.maximum(i - n_lat_tiles, 0) + ctx_tile0, 0))]


def _in_proj(x_lat, x_ctx, ctx_row0, nw, mod_l, w_in_p, cos_t, sin_t, *, nt, n_lat, seq, batch):
    d = x_lat.shape[1]
    tm = _pick(math.gcd(seq, nt - n_lat), (512, 256, 128))
    n_pos_tiles = seq // tm
    kern = functools.partial(_in_proj_kernel, n_lat_tiles=n_lat // tm)
    row = lambda i: (i, 0)
    return pl.pallas_call(
        kern,
        out_shape=(jax.ShapeDtypeStruct((nt, ATTN_WIDTH), BF16),
                   jax.ShapeDtypeStruct((nt, KV_WIDTH), BF16),
                   jax.ShapeDtypeStruct((KV_WIDTH, nt), BF16),
                   jax.ShapeDtypeStruct((nt, SSM_INNER), F32),
                   jax.ShapeDtypeStruct((nt, CONV_CH), F32),
                   jax.ShapeDtypeStruct((nt, LANES), F32)),
        grid=(nt // tm,),
        in_specs=_token_specs(tm, d, n_lat // tm, ctx_row0 // tm) + [
                  pl.BlockSpec((1, d), lambda i: (0, 0)),
                  pl.BlockSpec((1, N_MOD, d), lambda i: (jnp.minimum(i * tm // seq, batch), 0, 0)),
                  pl.BlockSpec((d, IN_PAD), lambda i: (0, 0)),
                  pl.BlockSpec((tm, LANES), lambda i: (i % n_pos_tiles, 0)),
                  pl.BlockSpec((tm, LANES), lambda i: (i % n_pos_tiles, 0))],
        out_specs=(pl.BlockSpec((tm, ATTN_WIDTH), row), pl.BlockSpec((tm, KV_WIDTH), row),
                   pl.BlockSpec((KV_WIDTH, tm), lambda i: (0, i)), pl.BlockSpec((tm, SSM_INNER), row),
                   pl.BlockSpec((tm, CONV_CH), row), pl.BlockSpec((tm, LANES), row)),
        compiler_params=_params(1),
        name="in_proj",
    )(x_lat, x_ctx, nw, mod_l, w_in_p, cos_t, sin_t)


def _attn_kernel(sink_ref, q_ref, kp_ref, kc_ref, kn_ref, vp_ref, vc_ref, vn_ref, kx_ref, vx_ref,
                 ba_ref, bb_ref, nw_ref, o_ref):
    blk = ATT_BLOCK
    rep = N_HEADS // N_KV_HEADS
    n_ctx = kx_ref.shape[0]
    scale = HEAD_DIM ** -0.5
    windows = (((kp_ref[...], kc_ref[0:blk, :], kc_ref[blk:, :]),
                (vp_ref[...], vc_ref[:, 0:blk], vc_ref[:, blk:]), ba_ref),
               ((kc_ref[0:blk, :], kc_ref[blk:, :], kn_ref[...]),
                (vc_ref[:, 0:blk], vc_ref[:, blk:], vn_ref[...]), bb_ref))
    for w, (kwin, vwin, bias_ref) in enumerate(windows):
        q = q_ref[w * blk:(w + 1) * blk, :] * jnp.asarray(scale, BF16)
        bias = jnp.concatenate([bias_ref[...]] * rep, axis=1)
        heads = []
        for g in range(N_KV_HEADS):
            sl = slice(g * HEAD_DIM, (g + 1) * HEAD_DIM)
            qg = jnp.concatenate([q[:, (g * rep + j) * HEAD_DIM:(g * rep + j + 1) * HEAD_DIM]
                                  for j in range(rep)], axis=0)
            kg = jnp.concatenate([kx_ref[:, sl]] + [kb[:, sl] for kb in kwin], axis=0)
            vg = jnp.concatenate([vx_ref[sl, :]] + [vb[sl, :] for vb in vwin], axis=1)
            s = lax.dot_general(kg, qg, (((1,), (1,)), ((), ())), preferred_element_type=F32)
            s = jnp.concatenate([s[:n_ctx], s[n_ctx:] + bias], axis=0)
            sink = jnp.concatenate([jnp.full((1, blk), sink_ref[g * rep + j], F32) for j in range(rep)], axis=1)
            m = jnp.maximum(jnp.max(s, axis=0, keepdims=True), sink)
            e = jnp.exp(s - m)
            denom = jnp.sum(e, axis=0, keepdims=True) + jnp.exp(sink - m)
            og = jnp.dot(vg, e.astype(BF16), preferred_element_type=F32) / denom
            heads += [og[:, j * blk:(j + 1) * blk] for j in range(rep)]
        ssq = heads[0] * heads[0]
        for h in heads[1:]:
            ssq = ssq + h * h
        inv = lax.rsqrt(jnp.sum(ssq, axis=0, keepdims=True) * (1.0 / ATTN_WIDTH) + EPS)
        out_t = jnp.concatenate(heads, axis=0) * inv * nw_ref[...]
        o_ref[w * blk:(w + 1) * blk, :] = out_t.T.astype(BF16)


def _window_bias():
    blk = ATT_BLOCK
    j = jnp.arange(blk)[:, None]
    i = jnp.arange(blk)[None, :]
    zero = jnp.zeros((blk, blk), F32)
    hidden = jnp.full((blk, blk), NEG_INF, F32)
    prev = jnp.where(j >= i, 0.0, NEG_INF).astype(F32)
    nxt = jnp.where(j <= i, 0.0, NEG_INF).astype(F32)
    variants = []
    for v in range(4):
        variants.append(jnp.concatenate([hidden if v & 1 else prev, zero, hidden if v & 2 else nxt], axis=0))
    variants.append(jnp.concatenate([hidden, hidden, hidden], axis=0))
    return jnp.stack(variants)


def _attention(q, k, vt, sinks, nw, *, batch, seq, n_ctx, with_ctx_queries):
    nt = q.shape[0]
    blk = ATT_BLOCK
    assert seq % (2 * blk) == 0 and n_ctx % (2 * blk) == 0
    nb = seq // blk
    np_ = nb // 2
    ncp = n_ctx // (2 * blk)
    nq = np_ + (ncp if with_ctx_queries else 0)
    ctx_blk0 = (batch * seq) // n_ctx

    def qmap(b, n, s):
        return (jnp.where(n < np_, b * np_ + n, batch * np_ + b * ncp + (n - np_)), 0)

    own = lambda b, n: b * np_ + jnp.clip(n, 0, np_ - 1)
    edge = lambda off: (lambda b, n: b * nb + jnp.clip(2 * n + off, 0, nb - 1))

    def bias_map(which):
        def index(b, n, s):
            hidden_edge = (n == 0).astype(I32) if which == 0 else 2 * (n == np_ - 1).astype(I32)
            return (jnp.where(n < np_, hidden_edge, 4), 0, 0)
        return index

    nw_b = jnp.broadcast_to(nw.reshape(ATTN_WIDTH, 1), (ATTN_WIDTH, blk))
    bias = _window_bias()
    return pl.pallas_call(
        _attn_kernel,
        out_shape=jax.ShapeDtypeStruct((nt if with_ctx_queries else batch * seq, ATTN_WIDTH), BF16),
        grid_spec=pltpu.PrefetchScalarGridSpec(
            num_scalar_prefetch=1,
            grid=(batch, nq),
            in_specs=[pl.BlockSpec((2 * blk, ATTN_WIDTH), qmap),
                      pl.BlockSpec((blk, KV_WIDTH), lambda b, n, s: (edge(-1)(b, n), 0)),
                      pl.BlockSpec((2 * blk, KV_WIDTH), lambda b, n, s: (own(b, n), 0)),
                      pl.BlockSpec((blk, KV_WIDTH), lambda b, n, s: (edge(2)(b, n), 0)),
                      pl.BlockSpec((KV_WIDTH, blk), lambda b, n, s: (0, edge(-1)(b, n))),
                      pl.BlockSpec((KV_WIDTH, 2 * blk), lambda b, n, s: (0, own(b, n))),
                      pl.BlockSpec((KV_WIDTH, blk), lambda b, n, s: (0, edge(2)(b, n))),
                      pl.BlockSpec((n_ctx, KV_WIDTH), lambda b, n, s: (ctx_blk0 + b, 0)),
                      pl.BlockSpec((KV_WIDTH, n_ctx), lambda b, n, s: (0, ctx_blk0 + b)),
                      pl.BlockSpec((None, 3 * blk, blk), bias_map(0)),
                      pl.BlockSpec((None, 3 * blk, blk), bias_map(1)),
                      pl.BlockSpec((ATTN_WIDTH, blk), lambda b, n, s: (0, 0))],
            out_specs=pl.BlockSpec((2 * blk, ATTN_WIDTH), qmap)),
        compiler_params=_params(2),
        name="attention",
    )(sinks, q, k, k, k, vt, vt, vt, k, vt, bias, bias, nw_b)


def _conv_kernel(xp_ref, xc_ref, xn_ref, w_ref, b_ref, o_ref, ext_ref, *, seq, n_lat, n_ctx):
    i = pl.program_id(0)
    tb = xc_ref.shape[0]
    row0 = i * tb
    in_lat = row0 < n_lat
    local = jnp.where(in_lat, row0 % seq, (row0 - n_lat) % n_ctx)
    length = jnp.where(in_lat, seq, n_ctx)
    first = local == 0
    last = local + tb == length
    h = SUBLANES
    ext_ref[0:h, :] = jnp.where(first, 0.0, xp_ref[...])
    ext_ref[h:h + tb, :] = xc_ref[...]
    ext_ref[h + tb:h + tb + h, :] = jnp.where(last, 0.0, xn_ref[...])
    acc = jnp.zeros((tb, CONV_CH), F32) + b_ref[...]
    for kk in range(CONV_W):
        off = h - CONV_W // 2 + kk
        acc = acc + ext_ref[off:off + tb, :] * w_ref[kk:kk + 1, :]
    o_ref[...] = acc * _sigmoid(acc)


def _conv_silu(xbc, conv_w, conv_b, *, seq, n_lat, n_ctx):
    nt = xbc.shape[0]
    tb = _pick(math.gcd(seq, n_ctx), (256, 128))
    h = SUBLANES
    per = tb // h
    n_h = nt // h
    kern = functools.partial(_conv_kernel, seq=seq, n_lat=n_lat, n_ctx=n_ctx)
    return pl.pallas_call(
        kern,
        out_shape=jax.ShapeDtypeStruct((nt, CONV_CH), F32),
        grid=(nt // tb,),
        in_specs=[pl.BlockSpec((h, CONV_CH), lambda i: (jnp.maximum(i * per - 1, 0), 0)),
                  pl.BlockSpec((tb, CONV_CH), lambda i: (i, 0)),
                  pl.BlockSpec((h, CONV_CH), lambda i: (jnp.minimum((i + 1) * per, n_h - 1), 0)),
                  pl.BlockSpec((CONV_W, CONV_CH), lambda i: (0, 0)),
                  pl.BlockSpec((1, CONV_CH), lambda i: (0, 0))],
        out_specs=pl.BlockSpec((tb, CONV_CH), lambda i: (i, 0)),
        scratch_shapes=[pltpu.VMEM((tb + 2 * h, CONV_CH), F32)],
        compiler_params=_params(1),
        name="conv_silu",
    )(xbc, xbc, xbc, conv_w, conv_b.reshape(1, CONV_CH))


def _ssd_chunk(u, dtraw, dtb, alog, state_ref, *, direction):
    q = CHUNK
    hp = SSM_INNER // SSM_HEADS
    per_g = SSM_HEADS // SSM_GROUPS
    gw = per_g * hp
    xs = u[:, :SSM_INNER]
    bm = u[:, SSM_INNER:SSM_INNER + SSM_GROUPS * SSM_STATE]
    cm = u[:, SSM_INNER + SSM_GROUPS * SSM_STATE:]

    xv = dtraw + dtb
    dt = jnp.maximum(xv, 0.0) + jnp.log1p(jnp.exp(-jnp.abs(xv)))
    dta = dt * (-jnp.exp(alog))

    ri = lax.broadcasted_iota(I32, (q, q), 0)
    ci = lax.broadcasted_iota(I32, (q, q), 1)
    tri = (ci <= ri) if direction == 0 else (ci >= ri)
    cs = _dot3(dta, tri.astype(BF16), left=True)
    cs_t = cs.T

    er = lax.broadcasted_iota(I32, (LANES, SSM_INNER), 0)
    ec = lax.broadcasted_iota(I32, (LANES, SSM_INNER), 1)
    expand = (er == direction * SSM_HEADS + jnp.right_shift(ec, hp.bit_length() - 1)).astype(BF16)
    cs_e = _dot3(cs, expand)
    dt_e = _dot3(dt, expand)
    last = q - 1 if direction == 0 else 0
    cs_last = cs_e[last:last + 1, :]

    xdt = (xs * dt_e).astype(BF16)
    xw = (xs * (jnp.exp(cs_last - cs_e) * dt_e)).astype(BF16)
    bm_t = bm.T.astype(BF16)
    cmb = cm.astype(BF16)
    bmb = bm.astype(BF16)
    state = state_ref[...]
    state_b = state.astype(BF16)

    y_diag = []
    y_off = []
    new_states = []
    for g in range(SSM_GROUPS):
        gs = slice(g * SSM_STATE, (g + 1) * SSM_STATE)
        cb = lax.dot_general(cmb[:, gs], bmb[:, gs], (((1,), (1,)), ((), ())), preferred_element_type=F32)
        y_off.append(jnp.dot(cmb[:, gs], state_b[:, g * gw:(g + 1) * gw], preferred_element_type=F32))
        new_states.append(jnp.dot(bm_t[gs, :], xw[:, g * gw:(g + 1) * gw], preferred_element_type=F32))
        for j in range(per_g):
            hh = g * per_g + j
            col = direction * SSM_HEADS + hh
            seg = cs[:, col:col + 1] - cs_t[col:col + 1, :]
            decay = jnp.exp(jnp.where(tri, seg, NEG_INF))
            scores = (cb * decay).astype(BF16)
            y_diag.append(jnp.dot(scores, xdt[:, hh * hp:(hh + 1) * hp], preferred_element_type=F32))
    y = jnp.concatenate(y_diag, axis=1) + jnp.exp(cs_e) * jnp.concatenate(y_off, axis=1)
    state_ref[...] = jnp.exp(cs_last) * state + jnp.concatenate(new_states, axis=1)
    return y


SSD_CHUNKS_PER_STEP = 2


def _ssd_fwd_kernel(u_ref, dt_ref, dtb_ref, alog_ref, skip_ref, y_ref, state_ref):
    @pl.when(pl.program_id(1) == 0)
    def _():
        state_ref[...] = jnp.zeros_like(state_ref)

    for j in range(u_ref.shape[0] // CHUNK):
        rows = slice(j * CHUNK, (j + 1) * CHUNK)
        u = u_ref[rows, :]
        y = _ssd_chunk(u, dt_ref[rows, :], dtb_ref[...], alog_ref[...], state_ref, direction=0)
        y_ref[rows, :] = y + skip_ref[...] * u[:, :SSM_INNER]


def _ssd_bwd_kernel(u_ref, dt_ref, dtb_ref, alog_ref, y0_ref, z_ref, nw_ref, o_ref, state_ref):
    @pl.when(pl.program_id(1) == 0)
    def _():
        state_ref[...] = jnp.zeros_like(state_ref)

    n = u_ref.shape[0] // CHUNK
    for j in range(n - 1, -1, -1):
        rows = slice(j * CHUNK, (j + 1) * CHUNK)
        y = y0_ref[rows, :] + _ssd_chunk(u_ref[rows, :], dt_ref[rows, :], dtb_ref[...], alog_ref[...],
                                         state_ref, direction=1)
        z = z_ref[rows, :]
        gt = y * (z * _sigmoid(z))
        gw = SSM_INNER // SSM_GROUPS
        outs = []
        for g in range(SSM_GROUPS):
            gg = gt[:, g * gw:(g + 1) * gw]
            ms = jnp.mean(gg * gg, axis=-1, keepdims=True)
            outs.append(gg * lax.rsqrt(ms + EPS))
        o_ref[rows, :] = (jnp.concatenate(outs, axis=1) * nw_ref[...]).astype(BF16)


def _ssd(u, dt_raw, z, dtb, alog, skip, ssm_nw, *, batch, seq, n_ctx):
    nt = u.shape[0]
    cps = SSD_CHUNKS_PER_STEP
    q = cps * CHUNK
    assert seq % q == 0 and n_ctx % q == 0
    ncl = seq // q
    ncc = n_ctx // q
    steps = ncc + ncl
    ctx0 = (batch * seq) // q

    def fmap(b, t):
        return (jnp.where(t < ncc, ctx0 + b * ncc + t, b * ncl + (t - ncc)), 0)

    def rmap(b, t):
        return (jnp.where(t < ncc, ctx0 + b * ncc + (ncc - 1 - t), b * ncl + (ncl - 1 - (t - ncc))), 0)

    const = lambda b, t: (0, 0)
    state = pltpu.VMEM((SSM_STATE, SSM_INNER), F32)
    y0 = pl.pallas_call(
        _ssd_fwd_kernel,
        out_shape=jax.ShapeDtypeStruct((nt, SSM_INNER), F32),
        grid=(batch, steps),
        in_specs=[pl.BlockSpec((q, CONV_CH), fmap), pl.BlockSpec((q, LANES), fmap),
                  pl.BlockSpec((1, LANES), const), pl.BlockSpec((1, LANES), const),
                  pl.BlockSpec((1, SSM_INNER), const)],
        out_specs=pl.BlockSpec((q, SSM_INNER), fmap),
        scratch_shapes=[state],
        compiler_params=_params(2),
        name="ssd_forward",
    )(u, dt_raw, dtb, alog, skip)
    return pl.pallas_call(
        _ssd_bwd_kernel,
        out_shape=jax.ShapeDtypeStruct((nt, SSM_INNER), BF16),
        grid=(batch, steps),
        in_specs=[pl.BlockSpec((q, CONV_CH), rmap), pl.BlockSpec((q, LANES), rmap),
                  pl.BlockSpec((1, LANES), const), pl.BlockSpec((1, LANES), const),
                  pl.BlockSpec((q, SSM_INNER), rmap), pl.BlockSpec((q, SSM_INNER), rmap),
                  pl.BlockSpec((1, SSM_INNER), const)],
        out_specs=pl.BlockSpec((q, SSM_INNER), rmap),
        scratch_shapes=[state],
        compiler_params=_params(2),
        name="ssd_backward",
    )(u, dt_raw, dtb, alog, y0, z, ssm_nw)


MOE_TILE = 256


def _out_proj_kernel(xl_ref, xc_ref, a_ref, s_ref, wa_ref, ws_ref, mod_ref, nw_ref, wr_ref, br_ref,
                     xo_ref, h_ref, idxt_ref, gate_ref, cnt_ref, *, n_lat_tiles):
    mix = (jnp.dot(a_ref[...], wa_ref[...], preferred_element_type=F32)
           + jnp.dot(s_ref[...], ws_ref[...], preferred_element_type=F32))
    x_in = jnp.where(pl.program_id(0) < n_lat_tiles, xl_ref[...], xc_ref[...])
    x = x_in + mod_ref[0, 2:3, :] * mix
    xo_ref[...] = x
    h = _rms_mod(x, nw_ref[...], mod_ref[0, 3:4, :], mod_ref[0, 4:5, :])
    h_hi = h.astype(BF16)
    h_ref[...] = h_hi
    h_lo = (h - h_hi.astype(F32)).astype(BF16)
    hh = jnp.dot(h_hi, wr_ref[...], preferred_element_type=F32)
    lh = jnp.dot(h_lo, wr_ref[:, :LANES], preferred_element_type=F32)
    logits = hh[:, :LANES] + hh[:, LANES:] + lh + br_ref[...]
    tm = logits.shape[0]
    lane = lax.broadcasted_iota(I32, (tm, LANES), 1)
    lane_f = lane.astype(F32)
    work = jnp.where(lane < N_EXPERTS, logits, -jnp.inf)
    idx_out = jnp.zeros((tm, LANES), F32)
    val_out = jnp.full((tm, LANES), -jnp.inf, F32)
    onehot = jnp.zeros((tm, LANES), F32)
    for kk in range(TOP_K):
        m = jnp.max(work, axis=-1, keepdims=True)
        sel = jnp.min(jnp.where(work == m, lane_f, float(LANES)), axis=-1, keepdims=True)
        idx_out = jnp.where(lane == kk, sel, idx_out)
        val_out = jnp.where(lane == kk, m, val_out)
        picked = lane_f == sel
        onehot = onehot + jnp.where(picked, 1.0, 0.0)
        work = jnp.where(picked, -jnp.inf, work)
    top = jnp.max(val_out, axis=-1, keepdims=True)
    e = jnp.exp(val_out - top)
    gate_ref[...] = e / jnp.sum(e, axis=-1, keepdims=True)
    idxt_ref[...] = idx_out.T[0:SUBLANES, :].astype(I32)
    ones = jnp.ones((SUBLANES, MOE_TILE), BF16)
    oh_b = onehot.astype(BF16)
    for j in range(tm // MOE_TILE):
        cnt_ref[j] = jnp.dot(ones, oh_b[j * MOE_TILE:(j + 1) * MOE_TILE, :],
                             preferred_element_type=F32).astype(I32)


def _out_proj(x_lat, x_ctx, ctx_row0, attn, ssm, w_out_a, w_out_s, mod_l, nfw, wr_hl, br_p,
              *, n_rows, n_lat, seq, batch):
    d = x_lat.shape[1]
    tm = _pick(math.gcd(seq, n_rows - n_lat) if n_rows > n_lat else seq, (512, 256))
    sub = tm // MOE_TILE
    row = lambda i: (i, 0)
    const = lambda i: (0, 0)
    kern = functools.partial(_out_proj_kernel, n_lat_tiles=n_lat // tm)
    return pl.pallas_call(
        kern,
        out_shape=(jax.ShapeDtypeStruct((n_rows, d), F32), jax.ShapeDtypeStruct((n_rows, d), BF16),
                   jax.ShapeDtypeStruct((SUBLANES, n_rows), I32), jax.ShapeDtypeStruct((n_rows, LANES), F32),
                   jax.ShapeDtypeStruct((n_rows // MOE_TILE, SUBLANES, LANES), I32)),
        grid=(n_rows // tm,),
        in_specs=_token_specs(tm, d, n_lat // tm, ctx_row0 // tm) + [
                  pl.BlockSpec((tm, ATTN_WIDTH), row), pl.BlockSpec((tm, SSM_INNER), row),
                  pl.BlockSpec((ATTN_WIDTH, d), const), pl.BlockSpec((SSM_INNER, d), const),
                  pl.BlockSpec((1, N_MOD, d), lambda i: (jnp.minimum(i * tm // seq, batch), 0, 0)),
                  pl.BlockSpec((1, d), const), pl.BlockSpec((d, 2 * LANES), const),
                  pl.BlockSpec((1, LANES), const)],
        out_specs=(pl.BlockSpec((tm, d), row), pl.BlockSpec((tm, d), row),
                   pl.BlockSpec((SUBLANES, tm), lambda i: (0, i)), pl.BlockSpec((tm, LANES), row),
                   pl.BlockSpec((sub, SUBLANES, LANES), lambda i: (i, 0, 0))),
        compiler_params=_params(1),
        name="out_proj_router",
    )(x_lat, x_ctx, attn, ssm, w_out_a, w_out_s, mod_l, nfw, wr_hl, br_p)


TAB_CNT, TAB_OFF, TAB_BASE = 0, N_EXPERTS, 2 * N_EXPERTS
ROW_TILE = D_MODEL // LANES
assert ROW_TILE == SUBLANES


def _to_row_tiles(ref, base, val):
    n = val.shape[0]
    for s in range(ROW_TILE):
        ref[pl.ds(base + s, n, stride=ROW_TILE), :] = val[:, s * LANES:(s + 1) * LANES]


def _from_row_tiles(ref, base, n):
    return jnp.concatenate([ref[pl.ds(base + s, n, stride=ROW_TILE), :] for s in range(ROW_TILE)], axis=1)


def _tile_rows(start, size):
    return pl.ds(pl.multiple_of(start * ROW_TILE, ROW_TILE), size * ROW_TILE)


def _for_each_run(tab_ref, lanes, make_copy, fn, *, enabled=None, unrolled=False):
    def per_expert(e, carry=0):
        cnt = tab_ref[0, lanes[0] + e]
        if enabled is not None:
            cnt = jnp.where(enabled, cnt, 0)

        @pl.when(cnt > 0)
        def _():
            fn(make_copy(tab_ref[0, lanes[1] + e], tab_ref[0, lanes[2] + e], cnt))
        return carry

    if unrolled:
        for e in range(N_EXPERTS):
            per_expert(e)
    else:
        lax.fori_loop(0, N_EXPERTS, per_expert, 0)


def _start(copy):
    copy.start()


def _wait(copy):
    copy.wait()


def _dispatch_kernel(tab_ref, tabp_ref, pad_ref, idxt_ref, h_ref, buf_ref, pos_ref,
                     scr_ref, zero_ref, sem, zsem, *, n_steps, group):
    i = pl.program_id(0)
    slot = i % 2
    t = h_ref.shape[0] // group
    rows = TOP_K * t
    run_lanes = (TAB_CNT, TAB_OFF, TAB_BASE)

    def copy_out(s):
        return lambda local, glob, size: pltpu.make_async_copy(
            scr_ref.at[_tile_rows(s * rows + local, size), :], buf_ref.at[_tile_rows(glob, size), :], sem.at[s])

    @pl.when(i == 0)
    def _():
        zero_ref[...] = jnp.zeros_like(zero_ref)
        zero_copy = lambda local, glob, size: pltpu.make_async_copy(
            zero_ref.at[_tile_rows(0, size), :], buf_ref.at[_tile_rows(glob, size), :], zsem)
        pad_lanes = (0, 0, N_EXPERTS)
        _for_each_run(pad_ref, pad_lanes, zero_copy, _start)
        _for_each_run(pad_ref, pad_lanes, zero_copy, _wait)
        tail_start = pad_ref[0, 2 * N_EXPERTS]
        n_tail = pad_ref[0, 2 * N_EXPERTS + 1]
        zrows = zero_ref.shape[0] // ROW_TILE

        def tail(fn):
            def body(j, carry):
                fn(zero_copy(0, tail_start + j * zrows, zrows))
                return carry
            return body

        lax.fori_loop(0, n_tail, tail(_start), 0)
        lax.fori_loop(0, n_tail, tail(_wait), 0)

    e_iota = lax.broadcasted_iota(I32, (N_EXPERTS, t), 0)
    upper = (lax.broadcasted_iota(I32, (t, t), 0) < lax.broadcasted_iota(I32, (t, t), 1)).astype(BF16)
    below = (lax.broadcasted_iota(I32, (N_EXPERTS, N_EXPERTS), 1)
             < lax.broadcasted_iota(I32, (N_EXPERTS, N_EXPERTS), 0)).astype(BF16)
    r_iota = lax.broadcasted_iota(I32, (rows, t), 0)
    for g in range(group):
        cols = slice(g * t, (g + 1) * t)
        onehots, counts, before = [], [], []
        for kk in range(TOP_K):
            oh = jnp.where(e_iota == idxt_ref[kk:kk + 1, cols], 1.0, 0.0)
            onehots.append(oh)
            counts.append(jnp.sum(oh, axis=1, keepdims=True))
            before.append(jnp.dot(oh.astype(BF16), upper, preferred_element_type=F32))
        total = counts[0] + counts[1] + counts[2] + counts[3]
        start = _dot3(jnp.broadcast_to(total, (N_EXPERTS, LANES)), below, left=True)[:, 0:1]
        pos_rows = []
        for kk in range(TOP_K):
            pos_rows.append(jnp.sum(onehots[kk] * (before[kk] + start), axis=0, keepdims=True))
            start = start + counts[kk]
        perm = jnp.zeros((rows, t), F32)
        for kk in range(TOP_K):
            perm = jnp.where(r_iota == pos_rows[kk].astype(I32), 1.0, perm)
        _to_row_tiles(scr_ref, (slot * group + g) * rows * ROW_TILE,
                      jnp.dot(perm.astype(BF16), h_ref[cols, :], preferred_element_type=F32))
        pos_t = jnp.concatenate(pos_rows + [jnp.zeros((LANES - TOP_K, t), F32)], axis=0)
        pos_ref[cols, :] = pos_t.T.astype(I32)
        _for_each_run(tab_ref.at[g], run_lanes, copy_out(slot * group + g), _start, unrolled=True)

    for g in range(group):
        _for_each_run(tabp_ref.at[g], run_lanes, copy_out((1 - slot) * group + g), _wait,
                      enabled=i > 0, unrolled=True)

    @pl.when(i == n_steps - 1)
    def _():
        for g in range(group):
            _for_each_run(tab_ref.at[g], run_lanes, copy_out(slot * group + g), _wait)


def _moe_group(n_tiles):
    return 2 if n_tiles % 2 == 0 else 1


def _dispatch(h2, idxt, table, padtab, *, n_rows, tme):
    nt, d = h2.shape
    t = MOE_TILE
    group = _moe_group(nt // t)
    n_steps = nt // (t * group)
    smem = lambda shape, imap: pl.BlockSpec(shape, imap, memory_space=pltpu.SMEM)
    kern = functools.partial(_dispatch_kernel, n_steps=n_steps, group=group)
    return pl.pallas_call(
        kern,
        out_shape=(jax.ShapeDtypeStruct((n_rows * ROW_TILE, LANES), F32),
                   jax.ShapeDtypeStruct((nt, LANES), I32)),
        grid=(n_steps,),
        in_specs=[smem((group, 1, LANES), lambda i: (i, 0, 0)),
                  smem((group, 1, LANES), lambda i: (jnp.maximum(i - 1, 0), 0, 0)),
                  smem((None, 1, LANES), lambda i: (0, 0, 0)),
                  pl.BlockSpec((SUBLANES, group * t), lambda i: (0, i)),
                  pl.BlockSpec((group * t, d), lambda i: (i, 0))],
        out_specs=(pl.BlockSpec(memory_space=pl.ANY), pl.BlockSpec((group * t, LANES), lambda i: (i, 0))),
        scratch_shapes=[pltpu.VMEM((2 * group * TOP_K * t * ROW_TILE, LANES), F32),
                        pltpu.VMEM((tme * ROW_TILE, LANES), F32),
                        pltpu.SemaphoreType.DMA((2 * group,)), pltpu.SemaphoreType.DMA],
        compiler_params=_params(1),
        name="moe_dispatch",
    )(table, table, padtab, idxt, h2)


def _expert_kernel(be_ref, nu_ref, nx_ref, x_ref, wgu_hbm, bgu_ref, wd_hbm, bd_ref, o_ref,
                   wgu_f, wd_f, wgu_b, wd_b, sem, *, layer):
    i = pl.program_id(0)
    tme = x_ref.shape[0] // ROW_TILE

    def fetch(e):
        return (pltpu.make_async_copy(wgu_hbm.at[layer, e], wgu_f, sem.at[0]),
                pltpu.make_async_copy(wd_hbm.at[layer, e], wd_f, sem.at[1]))

    @pl.when(i < nu_ref[0])
    def _():
        e = be_ref[i]

        @pl.when(i == 0)
        def _():
            for c in fetch(e):
                c.start()

        @pl.when((i == 0) | (e != be_ref[jnp.maximum(i - 1, 0)]))
        def _():
            for c in fetch(e):
                c.wait()
            wgu_b[...] = wgu_f[...].astype(BF16)
            wd_b[...] = wd_f[...].astype(BF16)
            nxt = nx_ref[i]

            @pl.when(nxt >= 0)
            def _():
                for c in fetch(nxt):
                    c.start()

        x = _from_row_tiles(x_ref, 0, tme)
        gu = jnp.dot(x.astype(BF16), wgu_b[...], preferred_element_type=F32) + bgu_ref[...]
        glu = jnp.minimum(gu[:, :D_FF], SWIGLU_LIMIT)
        lin = jnp.clip(gu[:, D_FF:], -SWIGLU_LIMIT, SWIGLU_LIMIT)
        act = glu * _sigmoid(SWIGLU_ALPHA * glu) * (lin + 1.0)
        _to_row_tiles(o_ref, 0, jnp.dot(act.astype(BF16), wd_b[...], preferred_element_type=F32) + bd_ref[...])

    @pl.when(i >= nu_ref[0])
    def _():
        o_ref[...] = jnp.zeros_like(o_ref)


def _experts(buf, block_e, n_used, next_e, wgu, bgu, wd, bd, *, layer, tme):
    d = D_MODEL
    nblk = buf.shape[0] // (tme * ROW_TILE)
    xmap = lambda i, be, nu, nx: (jnp.maximum(jnp.minimum(i, nu[0] - 1), 0), 0)
    emap = lambda i, be, nu, nx: (layer, be[i], 0, 0)
    kern = functools.partial(_expert_kernel, layer=layer)
    return pl.pallas_call(
        kern,
        out_shape=jax.ShapeDtypeStruct(buf.shape, F32),
        grid_spec=pltpu.PrefetchScalarGridSpec(
            num_scalar_prefetch=3,
            grid=(nblk,),
            in_specs=[pl.BlockSpec((tme * ROW_TILE, LANES), xmap),
                      pl.BlockSpec(memory_space=pl.ANY),
                      pl.BlockSpec((None, None, 1, 2 * D_FF), emap),
                      pl.BlockSpec(memory_space=pl.ANY),
                      pl.BlockSpec((None, None, 1, d), emap)],
            out_specs=pl.BlockSpec((tme * ROW_TILE, LANES), lambda i, be, nu, nx: (i, 0)),
            scratch_shapes=[pltpu.VMEM((d, 2 * D_FF), F32), pltpu.VMEM((D_FF, d), F32),
                            pltpu.VMEM((d, 2 * D_FF), BF16), pltpu.VMEM((D_FF, d), BF16),
                            pltpu.SemaphoreType.DMA((2,))]),
        compiler_params=_params(1),
        name="expert_ffn",
    )(block_e, n_used, next_e, buf, wgu, bgu, wd, bd)


def _combine_kernel(tab_ref, tabn_ref, pos_ref, gate_ref, x_ref, mod_ref, fw_ref, ob_ref, o_ref,
                    scr_ref, sem, *, n_steps, group, final_norm):
    i = pl.program_id(0)
    slot = i % 2
    t = x_ref.shape[0] // group
    rows = TOP_K * t
    run_lanes = (TAB_CNT, TAB_OFF, TAB_BASE)

    def copy_in(s):
        return lambda local, glob, size: pltpu.make_async_copy(
            ob_ref.at[_tile_rows(glob, size), :], scr_ref.at[_tile_rows(s * rows + local, size), :], sem.at[s])

    @pl.when(i == 0)
    def _():
        for g in range(group):
            _for_each_run(tab_ref.at[g], run_lanes, copy_in(slot * group + g), _start)

    for g in range(group):
        _for_each_run(tabn_ref.at[g], run_lanes, copy_in((1 - slot) * group + g), _start,
                      enabled=i + 1 < n_steps, unrolled=True)

    lane = lax.broadcasted_iota(I32, (t, rows), 1)
    for g in range(group):
        tok = slice(g * t, (g + 1) * t)
        pw = jnp.zeros((t, rows), F32)
        for kk in range(TOP_K):
            pw = jnp.where(lane == pos_ref[tok, kk:kk + 1], gate_ref[tok, kk:kk + 1], pw)
        pw_hi = pw.astype(BF16)
        pw_lo = (pw - pw_hi.astype(F32)).astype(BF16)

        _for_each_run(tab_ref.at[g], run_lanes, copy_in(slot * group + g), _wait, unrolled=True)
        y = _from_row_tiles(scr_ref, (slot * group + g) * rows * ROW_TILE, rows).astype(BF16)
        f = jnp.dot(pw_hi, y, preferred_element_type=F32) + jnp.dot(pw_lo, y, preferred_element_type=F32)
        x = x_ref[tok, :] + mod_ref[0, 5:6, :] * f
        if final_norm:
            ms = jnp.mean(x * x, axis=-1, keepdims=True)
            x = x * lax.rsqrt(ms + EPS) * fw_ref[...]
        o_ref[tok, :] = x


def _combine(xt, out_buf, pos, gates, table, mod_l, fw, *, seq, batch, final_norm):
    nt, d = xt.shape
    t = MOE_TILE
    group = _moe_group(nt // t)
    n_steps = nt // (t * group)
    gt = group * t
    row = lambda i: (i, 0)
    smem = lambda imap: pl.BlockSpec((group, 1, LANES), imap, memory_space=pltpu.SMEM)
    kern = functools.partial(_combine_kernel, n_steps=n_steps, group=group, final_norm=final_norm)
    return pl.pallas_call(
        kern,
        out_shape=jax.ShapeDtypeStruct((nt, d), F32),
        grid=(n_steps,),
        in_specs=[smem(lambda i: (i, 0, 0)), smem(lambda i: (jnp.minimum(i + 1, n_steps - 1), 0, 0)),
                  pl.BlockSpec((gt, LANES), row), pl.BlockSpec((gt, LANES), row), pl.BlockSpec((gt, d), row),
                  pl.BlockSpec((1, N_MOD, d), lambda i: (jnp.minimum(i * gt // seq, batch), 0, 0)),
                  pl.BlockSpec((1, d), lambda i: (0, 0)),
                  pl.BlockSpec(memory_space=pl.ANY)],
        out_specs=pl.BlockSpec((gt, d), row),
        scratch_shapes=[pltpu.VMEM((2 * group * TOP_K * t * ROW_TILE, LANES), F32),
                        pltpu.SemaphoreType.DMA((2 * group,))],
        compiler_params=_params(1),
        name="moe_combine",
    )(table, table, pos, gates, xt, mod_l, fw, out_buf)


def _route_tables(counts, *, tme, n_blocks):
    cnt = counts[:, 0, :N_EXPERTS]
    total = jnp.sum(cnt, axis=0)
    padded = (total + tme - 1) // tme * tme
    pad_end = jnp.cumsum(padded)
    pad_start = pad_end - padded
    base = pad_start[None, :] + jnp.cumsum(cnt, axis=0) - cnt
    off = jnp.cumsum(cnt, axis=1) - cnt
    table = jnp.concatenate([cnt, off, base, jnp.zeros_like(cnt)], axis=1).astype(I32)[:, None, :]
    tail = jnp.stack([pad_end[-1], n_blocks - pad_end[-1] // tme])
    padtab = jnp.concatenate([padded - total, pad_start + total, tail,
                              jnp.zeros((LANES - 2 * N_EXPERTS - 2,), I32)]).astype(I32)[None, None, :]
    n_used = (pad_end[-1] // tme).astype(I32).reshape(1)
    block_row0 = jnp.arange(n_blocks, dtype=I32) * tme
    block_e = jnp.minimum(jnp.sum((pad_end[None, :] <= block_row0[:, None]).astype(I32), axis=1),
                          N_EXPERTS - 1).astype(I32)
    ids = jnp.arange(N_EXPERTS, dtype=I32)
    later = (ids[None, :] > ids[:, None]) & (total[None, :] > 0)
    next_used = jnp.min(jnp.where(later, ids[None, :], N_EXPERTS), axis=1)
    next_used = jnp.where(next_used == N_EXPERTS, -1, next_used).astype(I32)
    return table, padtab, block_e, n_used, next_used[block_e]


def _rope_tables(seq):
    rows = seq // GRID_W
    row_pos = jnp.repeat(jnp.arange(rows, dtype=I32), GRID_W).astype(F32)
    col_pos = jnp.tile(jnp.arange(GRID_W, dtype=I32), rows).astype(F32)
    n_freq = HEAD_DIM // 4
    inv_freq = ROPE_BASE ** (-jnp.arange(n_freq, dtype=F32) / n_freq)
    lane = jnp.arange(LANES)
    f = lane % n_freq
    use_col = (lane % HEAD_DIM) >= HEAD_DIM // 2
    ang = jnp.where(use_col[None, :], col_pos[:, None], row_pos[:, None]) * inv_freq[f][None, :]
    first_half = (lane % 32) < 16
    return jnp.cos(ang), jnp.where(first_half[None, :], -jnp.sin(ang), jnp.sin(ang))


def kernel(x, c, ctx, c_ctx, w_ada, b_ada, norm_mix_w, norm_ffn_w, w_in, conv_w, conv_b, dt_bias, a_log, d_skip, ssm_norm_w, attn_sinks, attn_norm_w, w_out, w_router, b_router, w_gate_up, b_gate_up, w_down, b_down, final_norm_w):
    batch, seq, d = x.shape
    n_ctx = ctx.shape[1]
    depth = w_ada.shape[0]
    n_lat = batch * seq
    nt = n_lat + batch * n_ctx
    assert d == D_MODEL and seq % ATT_BLOCK == 0 and n_ctx % ATT_BLOCK == 0 and n_lat % n_ctx == 0

    r_mod = -(-(batch + 1) // SUBLANES) * SUBLANES
    c_all = jnp.zeros((r_mod, d), F32).at[:batch].set(c).at[batch].set(c_ctx)
    mod = _modulation(c_all, w_ada, b_ada)
    cos_t, sin_t = _rope_tables(seq)

    tme = 512
    assert nt % MOE_TILE == 0 and n_lat % MOE_TILE == 0
    hp = SSM_INNER // SSM_HEADS
    bgu = b_gate_up.reshape(depth, N_EXPERTS, 1, 2 * D_FF)
    bdn = b_down.reshape(depth, N_EXPERTS, 1, d)

    x_lat, x_ctx, ctx_row0 = x.reshape(n_lat, d), ctx.reshape(batch * n_ctx, d), 0
    for l in range(depth):
        last = l == depth - 1
        w_in_p = jnp.pad(w_in[l], ((0, 0), (0, IN_PAD - w_in.shape[2]))).astype(BF16)
        q, k, v, z, xbc, dt_raw = _in_proj(x_lat, x_ctx, ctx_row0, norm_mix_w[l].reshape(1, d), mod[l], w_in_p,
                                           cos_t, sin_t, nt=nt, n_lat=n_lat, seq=seq, batch=batch)
        attn = _attention(q, k, v, attn_sinks[l], attn_norm_w[l].reshape(1, ATTN_WIDTH),
                          batch=batch, seq=seq, n_ctx=n_ctx, with_ctx_queries=not last)
        u = _conv_silu(xbc, conv_w[l], conv_b[l], seq=seq, n_lat=n_lat, n_ctx=n_ctx)
        pad16 = lambda t: jnp.pad(t.reshape(1, N_DIRS * SSM_HEADS), ((0, 0), (0, LANES - N_DIRS * SSM_HEADS)))
        ssm = _ssd(u, dt_raw, z, pad16(dt_bias[l]), pad16(a_log[l]),
                   jnp.repeat(d_skip[l], hp).reshape(1, SSM_INNER), ssm_norm_w[l].reshape(1, SSM_INNER),
                   batch=batch, seq=seq, n_ctx=n_ctx)
        n_rows = n_lat if last else nt
        n_blocks = -(-(n_rows * TOP_K) // tme) + N_EXPERTS
        w_o = w_out[l].astype(BF16)
        wr_p = jnp.pad(w_router[l], ((0, 0), (0, LANES - N_EXPERTS)))
        wr_hi = wr_p.astype(BF16)
        wr_hl = jnp.concatenate([wr_hi, (wr_p - wr_hi.astype(F32)).astype(BF16)], axis=1)
        br_p = jnp.pad(b_router[l].reshape(1, N_EXPERTS), ((0, 0), (0, LANES - N_EXPERTS)))
        xt, h2, idxt, gates, counts = _out_proj(x_lat, x_ctx, ctx_row0, attn, ssm, w_o[:ATTN_WIDTH],
                                                w_o[ATTN_WIDTH:], mod[l], norm_ffn_w[l].reshape(1, d),
                                                wr_hl, br_p,
                                                n_rows=n_rows, n_lat=n_lat, seq=seq, batch=batch)
        table, padtab, block_e, n_used, next_e = _route_tables(counts, tme=tme, n_blocks=n_blocks)
        buf, pos = _dispatch(h2, idxt, table, padtab, n_rows=n_blocks * tme, tme=tme)
        out_buf = _experts(buf, block_e, n_used, next_e, w_gate_up, bgu, w_down, bdn, layer=l, tme=tme)
        xt = _combine(xt, out_buf, pos, gates, table, mod[l], final_norm_w.reshape(1, d),
                      seq=seq, batch=batch, final_norm=last)
        x_lat, x_ctx, ctx_row0 = xt, xt, n_lat
    return xt.reshape(batch, seq, d)
```

```python
import functools
import math

import jax
import jax.numpy as jnp
from jax import lax
from jax.experimental import pallas as pl
from jax.experimental.pallas import tpu as pltpu

F32 = jnp.float32
BF16 = jnp.bfloat16
I32 = jnp.int32

D_MODEL = 1024
GRID_W = 64
N_MOD = 6
EPS = 1e-6
NEG_INF = -1e30

HEAD_DIM = 64
ATTN_WIDTH = 512
N_HEADS = 8
N_KV_HEADS = 2
KV_WIDTH = 128
ATT_BLOCK = 128
ROPE_BASE = 10000.0

SSM_INNER = 512
SSM_HEADS = 8
SSM_GROUPS = 2
SSM_STATE = 64
CONV_W = 5
CONV_CH = 768
CHUNK = 128
N_DIRS = 2

N_EXPERTS = 32
TOP_K = 4
D_FF = 1024
SWIGLU_LIMIT = 7.0
SWIGLU_ALPHA = 1.702

LANES = 128
SUBLANES = 8
IN_PAD = 2176
VMEM_LIMIT = 56 * 1024 * 1024
HIGHEST = lax.Precision.HIGHEST


def _params(n_axes, vmem=VMEM_LIMIT):
    return pltpu.CompilerParams(dimension_semantics=("arbitrary",) * n_axes, vmem_limit_bytes=vmem)


def _pick(n, prefs):
    for t in prefs:
        if n % t == 0:
            return t
    raise ValueError(f"no tile in {prefs} divides {n}")


def _sigmoid(x):
    return 1.0 / (1.0 + jnp.exp(-x))


def _dot3(x, m_bf16, left=False):
    hi = x.astype(BF16)
    r1 = x - hi.astype(F32)
    mid = r1.astype(BF16)
    lo = (r1 - mid.astype(F32)).astype(BF16)
    mm = (lambda p: jnp.dot(m_bf16, p, preferred_element_type=F32)) if left else (
        lambda p: jnp.dot(p, m_bf16, preferred_element_type=F32))
    return mm(hi) + mm(mid) + mm(lo)


def _mod_kernel(c_ref, w_ref, b_ref, o_ref):
    c = c_ref[...]
    s = c * _sigmoid(c)
    o_ref[...] = jnp.dot(s, w_ref[...], preferred_element_type=F32, precision=HIGHEST) + b_ref[...]


def _modulation(c_all, w_ada, b_ada):
    depth, d, n = w_ada.shape
    r = c_all.shape[0]
    tn = _pick(n, (1536, 1024, 512, 128))
    out = pl.pallas_call(
        _mod_kernel,
        out_shape=jax.ShapeDtypeStruct((depth, r, n), F32),
        grid=(depth, n // tn),
        in_specs=[pl.BlockSpec((r, d), lambda l, j: (0, 0)),
                  pl.BlockSpec((None, d, tn), lambda l, j: (l, 0, j)),
                  pl.BlockSpec((None, 1, tn), lambda l, j: (l, 0, j))],
        out_specs=pl.BlockSpec((None, r, tn), lambda l, j: (l, 0, j)),
        compiler_params=_params(2),
        name="adaln_mod",
    )(c_all, w_ada, b_ada.reshape(depth, 1, n))
    return out.reshape(depth, r, N_MOD, d)


def _rms_mod(x, nw, shift, scale):
    ms = jnp.mean(x * x, axis=-1, keepdims=True)
    y = x * lax.rsqrt(ms + EPS) * nw
    return y * (1.0 + scale) + shift


def _in_proj_kernel(xl_ref, xc_ref, nw_ref, mod_ref, w_ref, cos_ref, sin_ref,
                    q_ref, k_ref, v_ref, z_ref, xbc_ref, dt_ref, *, n_lat_tiles):
    i = pl.program_id(0)
    is_lat = i < n_lat_tiles
    x = jnp.where(is_lat, xl_ref[...], xc_ref[...])
    h = _rms_mod(x, nw_ref[...], mod_ref[0, 0:1, :], mod_ref[0, 1:2, :])
    p = jnp.dot(h.astype(BF16), w_ref[...], preferred_element_type=F32)
    tm = p.shape[0]
    cos = jnp.where(is_lat, cos_ref[...], 1.0)
    sin = jnp.where(is_lat, sin_ref[...], 0.0)
    lane = lax.broadcasted_iota(I32, (tm, LANES), 1)
    first_half = (lane & 31) < 16

    def rope(t):
        partner = jnp.where(first_half, pltpu.roll(t, LANES - 16, 1), pltpu.roll(t, 16, 1))
        return t * cos + partner * sin

    for j in range(ATTN_WIDTH // LANES):
        q_ref[:, j * LANES:(j + 1) * LANES] = rope(p[:, j * LANES:(j + 1) * LANES]).astype(BF16)
    k_ref[...] = rope(p[:, 512:640]).astype(BF16)
    v_ref[...] = p[:, 640:768].T.astype(BF16)
    z_ref[...] = p[:, 768:1280]
    xbc_ref[...] = p[:, 1280:2048]
    dt_ref[...] = p[:, 2048:2176]


def _token_specs(tm, d, n_lat_tiles, ctx_tile0):
    return [pl.BlockSpec((tm, d), lambda i: (jnp.minimum(i, n_lat_tiles - 1), 0)),
            pl.BlockSpec((tm, d), lambda i: (jnp.maximum(i - n_lat_tiles, 0) + ctx_tile0, 0))]


def _in_proj(x_lat, x_ctx, ctx_row0, nw, mod_l, w_in_p, cos_t, sin_t, *, nt, n_lat, seq, batch):
    d = x_lat.shape[1]
    tm = _pick(math.gcd(seq, nt - n_lat), (512, 256, 128))
    n_pos_tiles = seq // tm
    kern = functools.partial(_in_proj_kernel, n_lat_tiles=n_lat // tm)
    row = lambda i: (i, 0)
    return pl.pallas_call(
        kern,
        out_shape=(jax.ShapeDtypeStruct((nt, ATTN_WIDTH), BF16),
                   jax.ShapeDtypeStruct((nt, KV_WIDTH), BF16),
                   jax.ShapeDtypeStruct((KV_WIDTH, nt), BF16),
                   jax.ShapeDtypeStruct((nt, SSM_INNER), F32),
                   jax.ShapeDtypeStruct((nt, CONV_CH), F32),
                   jax.ShapeDtypeStruct((nt, LANES), F32)),
        grid=(nt // tm,),
        in_specs=_token_specs(tm, d, n_lat // tm, ctx_row0 // tm) + [
                  pl.BlockSpec((1, d), lambda i: (0, 0)),
                  pl.BlockSpec((1, N_MOD, d), lambda i: (jnp.minimum(i * tm // seq, batch), 0, 0)),
                  pl.BlockSpec((d, IN_PAD), lambda i: (0, 0)),
                  pl.BlockSpec((tm, LANES), lambda i: (i % n_pos_tiles, 0)),
                  pl.BlockSpec((tm, LANES), lambda i: (i % n_pos_tiles, 0))],
        out_specs=(pl.BlockSpec((tm, ATTN_WIDTH), row), pl.BlockSpec((tm, KV_WIDTH), row),
                   pl.BlockSpec((KV_WIDTH, tm), lambda i: (0, i)), pl.BlockSpec((tm, SSM_INNER), row),
                   pl.BlockSpec((tm, CONV_CH), row), pl.BlockSpec((tm, LANES), row)),
        compiler_params=_params(1),
        name="in_proj",
    )(x_lat, x_ctx, nw, mod_l, w_in_p, cos_t, sin_t)


def _attn_kernel(sink_ref, q_ref, kp_ref, kc_ref, kn_ref, vp_ref, vc_ref, vn_ref, kx_ref, vx_ref,
                 ba_ref, bb_ref, nw_ref, o_ref):
    blk = ATT_BLOCK
    rep = N_HEADS // N_KV_HEADS
    n_ctx = kx_ref.shape[0]
    scale = HEAD_DIM ** -0.5
    windows = (((kp_ref[...], kc_ref[0:blk, :], kc_ref[blk:, :]),
                (vp_ref[...], vc_ref[:, 0:blk], vc_ref[:, blk:]), ba_ref),
               ((kc_ref[0:blk, :], kc_ref[blk:, :], kn_ref[...]),
                (vc_ref[:, 0:blk], vc_ref[:, blk:], vn_ref[...]), bb_ref))
    for w, (kwin, vwin, bias_ref) in enumerate(windows):
        q = q_ref[w * blk:(w + 1) * blk, :] * jnp.asarray(scale, BF16)
        bias = jnp.concatenate([bias_ref[...]] * rep, axis=1)
        heads = []
        for g in range(N_KV_HEADS):
            sl = slice(g * HEAD_DIM, (g + 1) * HEAD_DIM)
            qg = jnp.concatenate([q[:, (g * rep + j) * HEAD_DIM:(g * rep + j + 1) * HEAD_DIM]
                                  for j in range(rep)], axis=0)
            kg = jnp.concatenate([kx_ref[:, sl]] + [kb[:, sl] for kb in kwin], axis=0)
            vg = jnp.concatenate([vx_ref[sl, :]] + [vb[sl, :] for vb in vwin], axis=1)
            s = lax.dot_general(kg, qg, (((1,), (1,)), ((), ())), preferred_element_type=F32)
            s = jnp.concatenate([s[:n_ctx], s[n_ctx:] + bias], axis=0)
            sink = jnp.concatenate([jnp.full((1, blk), sink_ref[g * rep + j], F32) for j in range(rep)], axis=1)
            m = jnp.maximum(jnp.max(s, axis=0, keepdims=True), sink)
            e = jnp.exp(s - m)
            denom = jnp.sum(e, axis=0, keepdims=True) + jnp.exp(sink - m)
            og = jnp.dot(vg, e.astype(BF16), preferred_element_type=F32) / denom
            heads += [og[:, j * blk:(j + 1) * blk] for j in range(rep)]
        ssq = heads[0] * heads[0]
        for h in heads[1:]:
            ssq = ssq + h * h
        inv = lax.rsqrt(jnp.sum(ssq, axis=0, keepdims=True) * (1.0 / ATTN_WIDTH) + EPS)
        out_t = jnp.concatenate(heads, axis=0) * inv * nw_ref[...]
        o_ref[w * blk:(w + 1) * blk, :] = out_t.T.astype(BF16)


def _window_bias():
    blk = ATT_BLOCK
    j = jnp.arange(blk)[:, None]
    i = jnp.arange(blk)[None, :]
    zero = jnp.zeros((blk, blk), F32)
    hidden = jnp.full((blk, blk), NEG_INF, F32)
    prev = jnp.where(j >= i, 0.0, NEG_INF).astype(F32)
    nxt = jnp.where(j <= i, 0.0, NEG_INF).astype(F32)
    variants = []
    for v in range(4):
        variants.append(jnp.concatenate([hidden if v & 1 else prev, zero, hidden if v & 2 else nxt], axis=0))
    variants.append(jnp.concatenate([hidden, hidden, hidden], axis=0))
    return jnp.stack(variants)


def _attention(q, k, vt, sinks, nw, *, batch, seq, n_ctx, with_ctx_queries):
    nt = q.shape[0]
    blk = ATT_BLOCK
    assert seq % (2 * blk) == 0 and n_ctx % (2 * blk) == 0
    nb = seq // blk
    np_ = nb // 2
    ncp = n_ctx // (2 * blk)
    nq = np_ + (ncp if with_ctx_queries else 0)
    ctx_blk0 = (batch * seq) // n_ctx

    def qmap(b, n, s):
        return (jnp.where(n < np_, b * np_ + n, batch * np_ + b * ncp + (n - np_)), 0)

    own = lambda b, n: b * np_ + jnp.clip(n, 0, np_ - 1)
    edge = lambda off: (lambda b, n: b * nb + jnp.clip(2 * n + off, 0, nb - 1))

    def bias_map(which):
        def index(b, n, s):
            hidden_edge = (n == 0).astype(I32) if which == 0 else 2 * (n == np_ - 1).astype(I32)
            return (jnp.where(n < np_, hidden_edge, 4), 0, 0)
        return index

    nw_b = jnp.broadcast_to(nw.reshape(ATTN_WIDTH, 1), (ATTN_WIDTH, blk))
    bias = _window_bias()
    return pl.pallas_call(
        _attn_kernel,
        out_shape=jax.ShapeDtypeStruct((nt if with_ctx_queries else batch * seq, ATTN_WIDTH), BF16),
        grid_spec=pltpu.PrefetchScalarGridSpec(
            num_scalar_prefetch=1,
            grid=(batch, nq),
            in_specs=[pl.BlockSpec((2 * blk, ATTN_WIDTH), qmap),
                      pl.BlockSpec((blk, KV_WIDTH), lambda b, n, s: (edge(-1)(b, n), 0)),
                      pl.BlockSpec((2 * blk, KV_WIDTH), lambda b, n, s: (own(b, n), 0)),
                      pl.BlockSpec((blk, KV_WIDTH), lambda b, n, s: (edge(2)(b, n), 0)),
                      pl.BlockSpec((KV_WIDTH, blk), lambda b, n, s: (0, edge(-1)(b, n))),
                      pl.BlockSpec((KV_WIDTH, 2 * blk), lambda b, n, s: (0, own(b, n))),
                      pl.BlockSpec((KV_WIDTH, blk), lambda b, n, s: (0, edge(2)(b, n))),
                      pl.BlockSpec((n_ctx, KV_WIDTH), lambda b, n, s: (ctx_blk0 + b, 0)),
                      pl.BlockSpec((KV_WIDTH, n_ctx), lambda b, n, s: (0, ctx_blk0 + b)),
                      pl.BlockSpec((None, 3 * blk, blk), bias_map(0)),
                      pl.BlockSpec((None, 3 * blk, blk), bias_map(1)),
                      pl.BlockSpec((ATTN_WIDTH, blk), lambda b, n, s: (0, 0))],
            out_specs=pl.BlockSpec((2 * blk, ATTN_WIDTH), qmap)),
        compiler_params=_params(2),
        name="attention",
    )(sinks, q, k, k, k, vt, vt, vt, k, vt, bias, bias, nw_b)


def _conv_kernel(xp_ref, xc_ref, xn_ref, w_ref, b_ref, o_ref, *, seq, n_lat, n_ctx):
    i = pl.program_id(0)
    tb = xc_ref.shape[0]
    row0 = i * tb
    in_lat = row0 < n_lat
    local = jnp.where(in_lat, row0 % seq, (row0 - n_lat) % n_ctx)
    length = jnp.where(in_lat, seq, n_ctx)
    first = local == 0
    last = local + tb == length
    h = SUBLANES
    n_ext = tb + 2 * h
    ext = jnp.concatenate([jnp.where(first, 0.0, xp_ref[...]), xc_ref[...],
                           jnp.where(last, 0.0, xn_ref[...])], axis=0)
    acc = jnp.zeros((tb, CONV_CH), F32) + b_ref[...]
    for kk in range(CONV_W):
        shift = (CONV_W // 2 - kk) % n_ext
        src = ext if shift == 0 else pltpu.roll(ext, shift, 0)
        acc = acc + src[h:h + tb, :] * w_ref[kk:kk + 1, :]
    o_ref[...] = acc * _sigmoid(acc)


def _conv_silu(xbc, conv_w, conv_b, *, seq, n_lat, n_ctx):
    nt = xbc.shape[0]
    tb = _pick(math.gcd(seq, n_ctx), (256, 128))
    h = SUBLANES
    per = tb // h
    n_h = nt // h
    kern = functools.partial(_conv_kernel, seq=seq, n_lat=n_lat, n_ctx=n_ctx)
    return pl.pallas_call(
        kern,
        out_shape=jax.ShapeDtypeStruct((nt, CONV_CH), F32),
        grid=(nt // tb,),
        in_specs=[pl.BlockSpec((h, CONV_CH), lambda i: (jnp.maximum(i * per - 1, 0), 0)),
                  pl.BlockSpec((tb, CONV_CH), lambda i: (i, 0)),
                  pl.BlockSpec((h, CONV_CH), lambda i: (jnp.minimum((i + 1) * per, n_h - 1), 0)),
                  pl.BlockSpec((CONV_W, CONV_CH), lambda i: (0, 0)),
                  pl.BlockSpec((1, CONV_CH), lambda i: (0, 0))],
        out_specs=pl.BlockSpec((tb, CONV_CH), lambda i: (i, 0)),
        compiler_params=_params(1),
        name="conv_silu",
    )(xbc, xbc, xbc, conv_w, conv_b.reshape(1, CONV_CH))


def _ssd_chunk(u, dtraw, dtb, alog, state_ref, *, direction):
    q = CHUNK
    hp = SSM_INNER // SSM_HEADS
    per_g = SSM_HEADS // SSM_GROUPS
    gw = per_g * hp
    xs = u[:, :SSM_INNER]
    bm = u[:, SSM_INNER:SSM_INNER + SSM_GROUPS * SSM_STATE]
    cm = u[:, SSM_INNER + SSM_GROUPS * SSM_STATE:]

    xv = dtraw + dtb
    dt = jnp.maximum(xv, 0.0) + jnp.log1p(jnp.exp(-jnp.abs(xv)))
    dta = dt * (-jnp.exp(alog))

    ri = lax.broadcasted_iota(I32, (q, q), 0)
    ci = lax.broadcasted_iota(I32, (q, q), 1)
    tri = (ci <= ri) if direction == 0 else (ci >= ri)
    cs = _dot3(dta, tri.astype(BF16), left=True)
    cs_t = cs.T

    er = lax.broadcasted_iota(I32, (LANES, SSM_INNER), 0)
    ec = lax.broadcasted_iota(I32, (LANES, SSM_INNER), 1)
    expand = (er == direction * SSM_HEADS + jnp.right_shift(ec, hp.bit_length() - 1)).astype(BF16)
    cs_e = _dot3(cs, expand)
    dt_e = _dot3(dt, expand)
    last = q - 1 if direction == 0 else 0
    cs_last = cs_e[last:last + 1, :]

    xdt = (xs * dt_e).astype(BF16)
    xw = (xs * (jnp.exp(cs_last - cs_e) * dt_e)).astype(BF16)
    bm_t = bm.T.astype(BF16)
    cmb = cm.astype(BF16)
    bmb = bm.astype(BF16)
    state = state_ref[...]
    state_b = state.astype(BF16)

    y_diag = []
    y_off = []
    new_states = []
    for g in range(SSM_GROUPS):
        gs = slice(g * SSM_STATE, (g + 1) * SSM_STATE)
        cb = lax.dot_general(cmb[:, gs], bmb[:, gs], (((1,), (1,)), ((), ())), preferred_element_type=F32)
        y_off.append(jnp.dot(cmb[:, gs], state_b[:, g * gw:(g + 1) * gw], preferred_element_type=F32))
        new_states.append(jnp.dot(bm_t[gs, :], xw[:, g * gw:(g + 1) * gw], preferred_element_type=F32))
        for j in range(per_g):
            hh = g * per_g + j
            col = direction * SSM_HEADS + hh
            seg = cs[:, col:col + 1] - cs_t[col:col + 1, :]
            decay = jnp.exp(jnp.where(tri, seg, NEG_INF))
            scores = (cb * decay).astype(BF16)
            y_diag.append(jnp.dot(scores, xdt[:, hh * hp:(hh + 1) * hp], preferred_element_type=F32))
    y = jnp.concatenate(y_diag, axis=1) + jnp.exp(cs_e) * jnp.concatenate(y_off, axis=1)
    state_ref[...] = jnp.exp(cs_last) * state + jnp.concatenate(new_states, axis=1)
    return y


SSD_CHUNKS_PER_STEP = 2


def _ssd_fwd_kernel(u_ref, dt_ref, dtb_ref, alog_ref, skip_ref, y_ref, state_ref):
    @pl.when(pl.program_id(1) == 0)
    def _():
        state_ref[...] = jnp.zeros_like(state_ref)

    for j in range(u_ref.shape[0] // CHUNK):
        rows = slice(j * CHUNK, (j + 1) * CHUNK)
        u = u_ref[rows, :]
        y = _ssd_chunk(u, dt_ref[rows, :], dtb_ref[...], alog_ref[...], state_ref, direction=0)
        y_ref[rows, :] = y + skip_ref[...] * u[:, :SSM_INNER]


def _ssd_bwd_kernel(u_ref, dt_ref, dtb_ref, alog_ref, y0_ref, z_ref, nw_ref, o_ref, state_ref):
    @pl.when(pl.program_id(1) == 0)
    def _():
        state_ref[...] = jnp.zeros_like(state_ref)

    n = u_ref.shape[0] // CHUNK
    for j in range(n - 1, -1, -1):
        rows = slice(j * CHUNK, (j + 1) * CHUNK)
        y = y0_ref[rows, :] + _ssd_chunk(u_ref[rows, :], dt_ref[rows, :], dtb_ref[...], alog_ref[...],
                                         state_ref, direction=1)
        z = z_ref[rows, :]
        gt = y * (z * _sigmoid(z))
        gw = SSM_INNER // SSM_GROUPS
        outs = []
        for g in range(SSM_GROUPS):
            gg = gt[:, g * gw:(g + 1) * gw]
            ms = jnp.mean(gg * gg, axis=-1, keepdims=True)
            outs.append(gg * lax.rsqrt(ms + EPS))
        o_ref[rows, :] = (jnp.concatenate(outs, axis=1) * nw_ref[...]).astype(BF16)


def _ssd(u, dt_raw, z, dtb, alog, skip, ssm_nw, *, batch, seq, n_ctx):
    nt = u.shape[0]
    cps = SSD_CHUNKS_PER_STEP
    q = cps * CHUNK
    assert seq % q == 0 and n_ctx % q == 0
    ncl = seq // q
    ncc = n_ctx // q
    steps = ncc + ncl
    ctx0 = (batch * seq) // q

    def fmap(b, t):
        return (jnp.where(t < ncc, ctx0 + b * ncc + t, b * ncl + (t - ncc)), 0)

    def rmap(b, t):
        return (jnp.where(t < ncc, ctx0 + b * ncc + (ncc - 1 - t), b * ncl + (ncl - 1 - (t - ncc))), 0)

    const = lambda b, t: (0, 0)
    state = pltpu.VMEM((SSM_STATE, SSM_INNER), F32)
    y0 = pl.pallas_call(
        _ssd_fwd_kernel,
        out_shape=jax.ShapeDtypeStruct((nt, SSM_INNER), F32),
        grid=(batch, steps),
        in_specs=[pl.BlockSpec((q, CONV_CH), fmap), pl.BlockSpec((q, LANES), fmap),
                  pl.BlockSpec((1, LANES), const), pl.BlockSpec((1, LANES), const),
                  pl.BlockSpec((1, SSM_INNER), const)],
        out_specs=pl.BlockSpec((q, SSM_INNER), fmap),
        scratch_shapes=[state],
        compiler_params=_params(2),
        name="ssd_forward",
    )(u, dt_raw, dtb, alog, skip)
    return pl.pallas_call(
        _ssd_bwd_kernel,
        out_shape=jax.ShapeDtypeStruct((nt, SSM_INNER), BF16),
        grid=(batch, steps),
        in_specs=[pl.BlockSpec((q, CONV_CH), rmap), pl.BlockSpec((q, LANES), rmap),
                  pl.BlockSpec((1, LANES), const), pl.BlockSpec((1, LANES), const),
                  pl.BlockSpec((q, SSM_INNER), rmap), pl.BlockSpec((q, SSM_INNER), rmap),
                  pl.BlockSpec((1, SSM_INNER), const)],
        out_specs=pl.BlockSpec((q, SSM_INNER), rmap),
        scratch_shapes=[state],
        compiler_params=_params(2),
        name="ssd_backward",
    )(u, dt_raw, dtb, alog, y0, z, ssm_nw)


MOE_TILE = 256


def _out_proj_kernel(xl_ref, xc_ref, a_ref, s_ref, wa_ref, ws_ref, mod_ref, nw_ref, wr_ref, br_ref,
                     xo_ref, h_ref, idxt_ref, gate_ref, cnt_ref, *, n_lat_tiles):
    mix = (jnp.dot(a_ref[...], wa_ref[...], preferred_element_type=F32)
           + jnp.dot(s_ref[...], ws_ref[...], preferred_element_type=F32))
    x_in = jnp.where(pl.program_id(0) < n_lat_tiles, xl_ref[...], xc_ref[...])
    x = x_in + mod_ref[0, 2:3, :] * mix
    xo_ref[...] = x
    h = _rms_mod(x, nw_ref[...], mod_ref[0, 3:4, :], mod_ref[0, 4:5, :])
    h_hi = h.astype(BF16)
    h_ref[...] = h_hi
    h_lo = (h - h_hi.astype(F32)).astype(BF16)
    hh = jnp.dot(h_hi, wr_ref[...], preferred_element_type=F32)
    lh = jnp.dot(h_lo, wr_ref[:, :LANES], preferred_element_type=F32)
    logits = hh[:, :LANES] + hh[:, LANES:] + lh + br_ref[...]
    tm = logits.shape[0]
    lane = lax.broadcasted_iota(I32, (tm, LANES), 1)
    lane_f = lane.astype(F32)
    work = jnp.where(lane < N_EXPERTS, logits, -jnp.inf)
    idx_out = jnp.zeros((tm, LANES), F32)
    val_out = jnp.full((tm, LANES), -jnp.inf, F32)
    onehot = jnp.zeros((tm, LANES), F32)
    for kk in range(TOP_K):
        m = jnp.max(work, axis=-1, keepdims=True)
        sel = jnp.min(jnp.where(work == m, lane_f, float(LANES)), axis=-1, keepdims=True)
        idx_out = jnp.where(lane == kk, sel, idx_out)
        val_out = jnp.where(lane == kk, m, val_out)
        picked = lane_f == sel
        onehot = onehot + jnp.where(picked, 1.0, 0.0)
        work = jnp.where(picked, -jnp.inf, work)
    top = jnp.max(val_out, axis=-1, keepdims=True)
    e = jnp.exp(val_out - top)
    gate_ref[...] = e / jnp.sum(e, axis=-1, keepdims=True)
    idxt_ref[...] = idx_out.T[0:SUBLANES, :].astype(I32)
    ones = jnp.ones((SUBLANES, MOE_TILE), BF16)
    oh_b = onehot.astype(BF16)
    for j in range(tm // MOE_TILE):
        cnt_ref[j] = jnp.dot(ones, oh_b[j * MOE_TILE:(j + 1) * MOE_TILE, :],
                             preferred_element_type=F32).astype(I32)


def _out_proj(x_lat, x_ctx, ctx_row0, attn, ssm, w_out_a, w_out_s, mod_l, nfw, wr_hl, br_p,
              *, n_rows, n_lat, seq, batch):
    d = x_lat.shape[1]
    tm = _pick(math.gcd(seq, n_rows - n_lat) if n_rows > n_lat else seq, (512, 256))
    sub = tm // MOE_TILE
    row = lambda i: (i, 0)
    const = lambda i: (0, 0)
    kern = functools.partial(_out_proj_kernel, n_lat_tiles=n_lat // tm)
    return pl.pallas_call(
        kern,
        out_shape=(jax.ShapeDtypeStruct((n_rows, d), F32), jax.ShapeDtypeStruct((n_rows, d), BF16),
                   jax.ShapeDtypeStruct((SUBLANES, n_rows), I32), jax.ShapeDtypeStruct((n_rows, LANES), F32),
                   jax.ShapeDtypeStruct((n_rows // MOE_TILE, SUBLANES, LANES), I32)),
        grid=(n_rows // tm,),
        in_specs=_token_specs(tm, d, n_lat // tm, ctx_row0 // tm) + [
                  pl.BlockSpec((tm, ATTN_WIDTH), row), pl.BlockSpec((tm, SSM_INNER), row),
                  pl.BlockSpec((ATTN_WIDTH, d), const), pl.BlockSpec((SSM_INNER, d), const),
                  pl.BlockSpec((1, N_MOD, d), lambda i: (jnp.minimum(i * tm // seq, batch), 0, 0)),
                  pl.BlockSpec((1, d), const), pl.BlockSpec((d, 2 * LANES), const),
                  pl.BlockSpec((1, LANES), const)],
        out_specs=(pl.BlockSpec((tm, d), row), pl.BlockSpec((tm, d), row),
                   pl.BlockSpec((SUBLANES, tm), lambda i: (0, i)), pl.BlockSpec((tm, LANES), row),
                   pl.BlockSpec((sub, SUBLANES, LANES), lambda i: (i, 0, 0))),
        compiler_params=_params(1),
        name="out_proj_router",
    )(x_lat, x_ctx, attn, ssm, w_out_a, w_out_s, mod_l, nfw, wr_hl, br_p)


TAB_CNT, TAB_OFF, TAB_BASE = 0, N_EXPERTS, 2 * N_EXPERTS
ROW_TILE = D_MODEL // LANES
assert ROW_TILE == SUBLANES


def _to_row_tiles(ref, base, val):
    n = val.shape[0]
    for s in range(ROW_TILE):
        ref[pl.ds(base + s, n, stride=ROW_TILE), :] = val[:, s * LANES:(s + 1) * LANES]


def _from_row_tiles(ref, base, n):
    return jnp.concatenate([ref[pl.ds(base + s, n, stride=ROW_TILE), :] for s in range(ROW_TILE)], axis=1)


def _tile_rows(start, size):
    return pl.ds(pl.multiple_of(start * ROW_TILE, ROW_TILE), size * ROW_TILE)


def _for_each_run(tab_ref, lanes, make_copy, fn, *, enabled=None, unrolled=False):
    def per_expert(e, carry=0):
        cnt = tab_ref[0, lanes[0] + e]
        if enabled is not None:
            cnt = jnp.where(enabled, cnt, 0)

        @pl.when(cnt > 0)
        def _():
            fn(make_copy(tab_ref[0, lanes[1] + e], tab_ref[0, lanes[2] + e], cnt))
        return carry

    if unrolled:
        for e in range(N_EXPERTS):
            per_expert(e)
    else:
        lax.fori_loop(0, N_EXPERTS, per_expert, 0)


def _start(copy):
    copy.start()


def _wait(copy):
    copy.wait()


def _dispatch_kernel(tab_ref, tabp_ref, pad_ref, idxt_ref, h_ref, buf_ref, pos_ref,
                     scr_ref, zero_ref, sem, zsem, *, n_steps, group):
    i = pl.program_id(0)
    slot = i % 2
    t = h_ref.shape[0] // group
    rows = TOP_K * t
    run_lanes = (TAB_CNT, TAB_OFF, TAB_BASE)

    def copy_out(s):
        return lambda local, glob, size: pltpu.make_async_copy(
            scr_ref.at[_tile_rows(s * rows + local, size), :], buf_ref.at[_tile_rows(glob, size), :], sem.at[s])

    @pl.when(i == 0)
    def _():
        zero_ref[...] = jnp.zeros_like(zero_ref)
        zero_copy = lambda local, glob, size: pltpu.make_async_copy(
            zero_ref.at[_tile_rows(0, size), :], buf_ref.at[_tile_rows(glob, size), :], zsem)
        pad_lanes = (0, 0, N_EXPERTS)
        _for_each_run(pad_ref, pad_lanes, zero_copy, _start)
        _for_each_run(pad_ref, pad_lanes, zero_copy, _wait)
        tail_start = pad_ref[0, 2 * N_EXPERTS]
        n_tail = pad_ref[0, 2 * N_EXPERTS + 1]
        zrows = zero_ref.shape[0] // ROW_TILE

        def tail(fn):
            def body(j, carry):
                fn(zero_copy(0, tail_start + j * zrows, zrows))
                return carry
            return body

        lax.fori_loop(0, n_tail, tail(_start), 0)
        lax.fori_loop(0, n_tail, tail(_wait), 0)

    e_iota = lax.broadcasted_iota(I32, (N_EXPERTS, t), 0)
    upper = (lax.broadcasted_iota(I32, (t, t), 0) < lax.broadcasted_iota(I32, (t, t), 1)).astype(BF16)
    below = (lax.broadcasted_iota(I32, (N_EXPERTS, N_EXPERTS), 1)
             < lax.broadcasted_iota(I32, (N_EXPERTS, N_EXPERTS), 0)).astype(BF16)
    r_iota = lax.broadcasted_iota(I32, (rows, t), 0)
    for g in range(group):
        cols = slice(g * t, (g + 1) * t)
        onehots, counts, before = [], [], []
        for kk in range(TOP_K):
            oh = jnp.where(e_iota == idxt_ref[kk:kk + 1, cols], 1.0, 0.0)
            onehots.append(oh)
            counts.append(jnp.sum(oh, axis=1, keepdims=True))
            before.append(jnp.dot(oh.astype(BF16), upper, preferred_element_type=F32))
        total = counts[0] + counts[1] + counts[2] + counts[3]
        start = _dot3(jnp.broadcast_to(total, (N_EXPERTS, LANES)), below, left=True)[:, 0:1]
        pos_rows = []
        for kk in range(TOP_K):
            pos_rows.append(jnp.sum(onehots[kk] * (before[kk] + start), axis=0, keepdims=True))
            start = start + counts[kk]
        perm = jnp.zeros((rows, t), F32)
        for kk in range(TOP_K):
            perm = jnp.where(r_iota == pos_rows[kk].astype(I32), 1.0, perm)
        _to_row_tiles(scr_ref, (slot * group + g) * rows * ROW_TILE,
                      jnp.dot(perm.astype(BF16), h_ref[cols, :], preferred_element_type=F32))
        pos_t = jnp.concatenate(pos_rows + [jnp.zeros((LANES - TOP_K, t), F32)], axis=0)
        pos_ref[cols, :] = pos_t.T.astype(I32)
        _for_each_run(tab_ref.at[g], run_lanes, copy_out(slot * group + g), _start, unrolled=True)

    for g in range(group):
        _for_each_run(tabp_ref.at[g], run_lanes, copy_out((1 - slot) * group + g), _wait,
                      enabled=i > 0, unrolled=True)

    @pl.when(i == n_steps - 1)
    def _():
        for g in range(group):
            _for_each_run(tab_ref.at[g], run_lanes, copy_out(slot * group + g), _wait)


def _moe_group(n_tiles, largest=2):
    return next(g for g in (4, 2, 1) if g <= largest and n_tiles % g == 0)


def _dispatch(h2, idxt, table, padtab, *, n_rows, tme):
    nt, d = h2.shape
    t = MOE_TILE
    group = _moe_group(nt // t, largest=4)
    n_steps = nt // (t * group)
    smem = lambda shape, imap: pl.BlockSpec(shape, imap, memory_space=pltpu.SMEM)
    kern = functools.partial(_dispatch_kernel, n_steps=n_steps, group=group)
    return pl.pallas_call(
        kern,
        out_shape=(jax.ShapeDtypeStruct((n_rows * ROW_TILE, LANES), F32),
                   jax.ShapeDtypeStruct((nt, LANES), I32)),
        grid=(n_steps,),
        in_specs=[smem((group, 1, LANES), lambda i: (i, 0, 0)),
                  smem((group, 1, LANES), lambda i: (jnp.maximum(i - 1, 0), 0, 0)),
                  smem((None, 1, LANES), lambda i: (0, 0, 0)),
                  pl.BlockSpec((SUBLANES, group * t), lambda i: (0, i)),
                  pl.BlockSpec((group * t, d), lambda i: (i, 0))],
        out_specs=(pl.BlockSpec(memory_space=pl.ANY), pl.BlockSpec((group * t, LANES), lambda i: (i, 0))),
        scratch_shapes=[pltpu.VMEM((2 * group * TOP_K * t * ROW_TILE, LANES), F32),
                        pltpu.VMEM((tme * ROW_TILE, LANES), F32),
                        pltpu.SemaphoreType.DMA((2 * group,)), pltpu.SemaphoreType.DMA],
        compiler_params=_params(1),
        name="moe_dispatch",
    )(table, table, padtab, idxt, h2)


def _expert_kernel(be_ref, nu_ref, nx_ref, x_ref, wgu_hbm, bgu_ref, wd_hbm, bd_ref, o_ref,
                   wgu_f, wd_f, wgu_b, wd_b, sem, *, layer):
    i = pl.program_id(0)
    tme = x_ref.shape[0] // ROW_TILE

    def fetch(e):
        return (pltpu.make_async_copy(wgu_hbm.at[layer, e], wgu_f, sem.at[0]),
                pltpu.make_async_copy(wd_hbm.at[layer, e], wd_f, sem.at[1]))

    @pl.when(i < nu_ref[0])
    def _():
        e = be_ref[i]

        @pl.when(i == 0)
        def _():
            for c in fetch(e):
                c.start()

        @pl.when((i == 0) | (e != be_ref[jnp.maximum(i - 1, 0)]))
        def _():
            for c in fetch(e):
                c.wait()
            wgu_b[...] = wgu_f[...].astype(BF16)
            wd_b[...] = wd_f[...].astype(BF16)
            nxt = nx_ref[i]

            @pl.when(nxt >= 0)
            def _():
                for c in fetch(nxt):
                    c.start()

        x = _from_row_tiles(x_ref, 0, tme)
        gu = jnp.dot(x.astype(BF16), wgu_b[...], preferred_element_type=F32) + bgu_ref[...]
        glu = jnp.minimum(gu[:, :D_FF], SWIGLU_LIMIT)
        lin = jnp.clip(gu[:, D_FF:], -SWIGLU_LIMIT, SWIGLU_LIMIT)
        act = glu * _sigmoid(SWIGLU_ALPHA * glu) * (lin + 1.0)
        _to_row_tiles(o_ref, 0, jnp.dot(act.astype(BF16), wd_b[...], preferred_element_type=F32) + bd_ref[...])

    @pl.when(i >= nu_ref[0])
    def _():
        o_ref[...] = jnp.zeros_like(o_ref)


def _experts(buf, block_e, n_used, next_e, wgu, bgu, wd, bd, *, layer, tme):
    d = D_MODEL
    nblk = buf.shape[0] // (tme * ROW_TILE)
    xmap = lambda i, be, nu, nx: (jnp.maximum(jnp.minimum(i, nu[0] - 1), 0), 0)
    emap = lambda i, be, nu, nx: (layer, be[i], 0, 0)
    kern = functools.partial(_expert_kernel, layer=layer)
    return pl.pallas_call(
        kern,
        out_shape=jax.ShapeDtypeStruct(buf.shape, F32),
        grid_spec=pltpu.PrefetchScalarGridSpec(
            num_scalar_prefetch=3,
            grid=(nblk,),
            in_specs=[pl.BlockSpec((tme * ROW_TILE, LANES), xmap),
                      pl.BlockSpec(memory_space=pl.ANY),
                      pl.BlockSpec((None, None, 1, 2 * D_FF), emap),
                      pl.BlockSpec(memory_space=pl.ANY),
                      pl.BlockSpec((None, None, 1, d), emap)],
            out_specs=pl.BlockSpec((tme * ROW_TILE, LANES), lambda i, be, nu, nx: (i, 0)),
            scratch_shapes=[pltpu.VMEM((d, 2 * D_FF), F32), pltpu.VMEM((D_FF, d), F32),
                            pltpu.VMEM((d, 2 * D_FF), BF16), pltpu.VMEM((D_FF, d), BF16),
                            pltpu.SemaphoreType.DMA((2,))]),
        compiler_params=_params(1),
        name="expert_ffn",
    )(block_e, n_used, next_e, buf, wgu, bgu, wd, bd)


def _combine_kernel(tab_ref, tabn_ref, pos_ref, gate_ref, x_ref, mod_ref, fw_ref, ob_ref, o_ref,
                    scr_ref, sem, *, n_steps, group, final_norm):
    i = pl.program_id(0)
    slot = i % 2
    t = x_ref.shape[0] // group
    rows = TOP_K * t
    run_lanes = (TAB_CNT, TAB_OFF, TAB_BASE)

    def copy_in(s):
        return lambda local, glob, size: pltpu.make_async_copy(
            ob_ref.at[_tile_rows(glob, size), :], scr_ref.at[_tile_rows(s * rows + local, size), :], sem.at[s])

    @pl.when(i == 0)
    def _():
        for g in range(group):
            _for_each_run(tab_ref.at[g], run_lanes, copy_in(slot * group + g), _start)

    for g in range(group):
        _for_each_run(tabn_ref.at[g], run_lanes, copy_in((1 - slot) * group + g), _start,
                      enabled=i + 1 < n_steps, unrolled=True)

    lane = lax.broadcasted_iota(I32, (t, rows), 1)
    for g in range(group):
        tok = slice(g * t, (g + 1) * t)
        pw = jnp.zeros((t, rows), F32)
        for kk in range(TOP_K):
            pw = jnp.where(lane == pos_ref[tok, kk:kk + 1], gate_ref[tok, kk:kk + 1], pw)
        pw_hi = pw.astype(BF16)
        pw_lo = (pw - pw_hi.astype(F32)).astype(BF16)

        _for_each_run(tab_ref.at[g], run_lanes, copy_in(slot * group + g), _wait, unrolled=True)
        y = _from_row_tiles(scr_ref, (slot * group + g) * rows * ROW_TILE, rows).astype(BF16)
        f = jnp.dot(pw_hi, y, preferred_element_type=F32) + jnp.dot(pw_lo, y, preferred_element_type=F32)
        x = x_ref[tok, :] + mod_ref[0, 5:6, :] * f
        if final_norm:
            ms = jnp.mean(x * x, axis=-1, keepdims=True)
            x = x * lax.rsqrt(ms + EPS) * fw_ref[...]
        o_ref[tok, :] = x


def _combine(xt, out_buf, pos, gates, table, mod_l, fw, *, seq, batch, final_norm):
    nt, d = xt.shape
    t = MOE_TILE
    group = _moe_group(nt // t)
    n_steps = nt // (t * group)
    gt = group * t
    row = lambda i: (i, 0)
    smem = lambda imap: pl.BlockSpec((group, 1, LANES), imap, memory_space=pltpu.SMEM)
    kern = functools.partial(_combine_kernel, n_steps=n_steps, group=group, final_norm=final_norm)
    return pl.pallas_call(
        kern,
        out_shape=jax.ShapeDtypeStruct((nt, d), F32),
        grid=(n_steps,),
        in_specs=[smem(lambda i: (i, 0, 0)), smem(lambda i: (jnp.minimum(i + 1, n_steps - 1), 0, 0)),
                  pl.BlockSpec((gt, LANES), row), pl.BlockSpec((gt, LANES), row), pl.BlockSpec((gt, d), row),
                  pl.BlockSpec((1, N_MOD, d), lambda i: (jnp.minimum(i * gt // seq, batch), 0, 0)),
                  pl.BlockSpec((1, d), lambda i: (0, 0)),
                  pl.BlockSpec(memory_space=pl.ANY)],
        out_specs=pl.BlockSpec((gt, d), row),
        scratch_shapes=[pltpu.VMEM((2 * group * TOP_K * t * ROW_TILE, LANES), F32),
                        pltpu.SemaphoreType.DMA((2 * group,))],
        compiler_params=_params(1),
        name="moe_combine",
    )(table, table, pos, gates, xt, mod_l, fw, out_buf)


def _route_tables(counts, *, tme, n_blocks):
    cnt = counts[:, 0, :N_EXPERTS]
    total = jnp.sum(cnt, axis=0)
    padded = (total + tme - 1) // tme * tme
    pad_end = jnp.cumsum(padded)
    pad_start = pad_end - padded
    base = pad_start[None, :] + jnp.cumsum(cnt, axis=0) - cnt
    off = jnp.cumsum(cnt, axis=1) - cnt
    table = jnp.concatenate([cnt, off, base, jnp.zeros_like(cnt)], axis=1).astype(I32)[:, None, :]
    tail = jnp.stack([pad_end[-1], n_blocks - pad_end[-1] // tme])
    padtab = jnp.concatenate([padded - total, pad_start + total, tail,
                              jnp.zeros((LANES - 2 * N_EXPERTS - 2,), I32)]).astype(I32)[None, None, :]
    n_used = (pad_end[-1] // tme).astype(I32).reshape(1)
    block_row0 = jnp.arange(n_blocks, dtype=I32) * tme
    block_e = jnp.minimum(jnp.sum((pad_end[None, :] <= block_row0[:, None]).astype(I32), axis=1),
                          N_EXPERTS - 1).astype(I32)
    ids = jnp.arange(N_EXPERTS, dtype=I32)
    later = (ids[None, :] > ids[:, None]) & (total[None, :] > 0)
    next_used = jnp.min(jnp.where(later, ids[None, :], N_EXPERTS), axis=1)
    next_used = jnp.where(next_used == N_EXPERTS, -1, next_used).astype(I32)
    return table, padtab, block_e, n_used, next_used[block_e]


def _rope_tables(seq):
    rows = seq // GRID_W
    row_pos = jnp.repeat(jnp.arange(rows, dtype=I32), GRID_W).astype(F32)
    col_pos = jnp.tile(jnp.arange(GRID_W, dtype=I32), rows).astype(F32)
    n_freq = HEAD_DIM // 4
    inv_freq = ROPE_BASE ** (-jnp.arange(n_freq, dtype=F32) / n_freq)
    lane = jnp.arange(LANES)
    f = lane % n_freq
    use_col = (lane % HEAD_DIM) >= HEAD_DIM // 2
    ang = jnp.where(use_col[None, :], col_pos[:, None], row_pos[:, None]) * inv_freq[f][None, :]
    first_half = (lane % 32) < 16
    return jnp.cos(ang), jnp.where(first_half[None, :], -jnp.sin(ang), jnp.sin(ang))


def kernel(x, c, ctx, c_ctx, w_ada, b_ada, norm_mix_w, norm_ffn_w, w_in, conv_w, conv_b, dt_bias, a_log, d_skip, ssm_norm_w, attn_sinks, attn_norm_w, w_out, w_router, b_router, w_gate_up, b_gate_up, w_down, b_down, final_norm_w):
    batch, seq, d = x.shape
    n_ctx = ctx.shape[1]
    depth = w_ada.shape[0]
    n_lat = batch * seq
    nt = n_lat + batch * n_ctx
    assert d == D_MODEL and seq % ATT_BLOCK == 0 and n_ctx % ATT_BLOCK == 0 and n_lat % n_ctx == 0

    r_mod = -(-(batch + 1) // SUBLANES) * SUBLANES
    c_all = jnp.zeros((r_mod, d), F32).at[:batch].set(c).at[batch].set(c_ctx)
    mod = _modulation(c_all, w_ada, b_ada)
    cos_t, sin_t = _rope_tables(seq)

    tme = 512
    assert nt % MOE_TILE == 0 and n_lat % MOE_TILE == 0
    hp = SSM_INNER // SSM_HEADS
    bgu = b_gate_up.reshape(depth, N_EXPERTS, 1, 2 * D_FF)
    bdn = b_down.reshape(depth, N_EXPERTS, 1, d)

    x_lat, x_ctx, ctx_row0 = x.reshape(n_lat, d), ctx.reshape(batch * n_ctx, d), 0
    for l in range(depth):
        last = l == depth - 1
        w_in_p = jnp.pad(w_in[l], ((0, 0), (0, IN_PAD - w_in.shape[2]))).astype(BF16)
        q, k, v, z, xbc, dt_raw = _in_proj(x_lat, x_ctx, ctx_row0, norm_mix_w[l].reshape(1, d), mod[l], w_in_p,
                                           cos_t, sin_t, nt=nt, n_lat=n_lat, seq=seq, batch=batch)
        attn = _attention(q, k, v, attn_sinks[l], attn_norm_w[l].reshape(1, ATTN_WIDTH),
                          batch=batch, seq=seq, n_ctx=n_ctx, with_ctx_queries=not last)
        u = _conv_silu(xbc, conv_w[l], conv_b[l], seq=seq, n_lat=n_lat, n_ctx=n_ctx)
        pad16 = lambda t: jnp.pad(t.reshape(1, N_DIRS * SSM_HEADS), ((0, 0), (0, LANES - N_DIRS * SSM_HEADS)))
        ssm = _ssd(u, dt_raw, z, pad16(dt_bias[l]), pad16(a_log[l]),
                   jnp.repeat(d_skip[l], hp).reshape(1, SSM_INNER), ssm_norm_w[l].reshape(1, SSM_INNER),
                   batch=batch, seq=seq, n_ctx=n_ctx)
        n_rows = n_lat if last else nt
        n_blocks = -(-(n_rows * TOP_K) // tme) + N_EXPERTS
        w_o = w_out[l].astype(BF16)
        wr_p = jnp.pad(w_router[l], ((0, 0), (0, LANES - N_EXPERTS)))
        wr_hi = wr_p.astype(BF16)
        wr_hl = jnp.concatenate([wr_hi, (wr_p - wr_hi.astype(F32)).astype(BF16)], axis=1)
        br_p = jnp.pad(b_router[l].reshape(1, N_EXPERTS), ((0, 0), (0, LANES - N_EXPERTS)))
        xt, h2, idxt, gates, counts = _out_proj(x_lat, x_ctx, ctx_row0, attn, ssm, w_o[:ATTN_WIDTH],
                                                w_o[ATTN_WIDTH:], mod[l], norm_ffn_w[l].reshape(1, d),
                                                wr_hl, br_p,
                                                n_rows=n_rows, n_lat=n_lat, seq=seq, batch=batch)
        table, padtab, block_e, n_used, next_e = _route_tables(counts, tme=tme, n_blocks=n_blocks)
        buf, pos = _dispatch(h2, idxt, table, padtab, n_rows=n_blocks * tme, tme=tme)
        out_buf = _experts(buf, block_e, n_used, next_e, w_gate_up, bgu, w_down, bdn, layer=l, tme=tme)
        xt = _combine(xt, out_buf, pos, gates, table, mod[l], final_norm_w.reshape(1, d),
                      seq=seq, batch=batch, final_norm=last)
        x_lat, x_ctx, ctx_row0 = xt, xt, n_lat
    return xt.reshape(batch, seq, d)
```

```python
import functools
import math

import jax
import jax.numpy as jnp
from jax import lax
from jax.experimental import pallas as pl
from jax.experimental.pallas import tpu as pltpu

F32 = jnp.float32
BF16 = jnp.bfloat16
I32 = jnp.int32

D_MODEL = 1024
GRID_W = 64
N_MOD = 6
EPS = 1e-6
NEG_INF = -1e30

HEAD_DIM = 64
ATTN_WIDTH = 512
N_HEADS = 8
N_KV_HEADS = 2
KV_WIDTH = 128
ATT_BLOCK = 128
ROPE_BASE = 10000.0

SSM_INNER = 512
SSM_HEADS = 8
SSM_GROUPS = 2
SSM_STATE = 64
CONV_W = 5
CONV_CH = 768
CHUNK = 128
N_DIRS = 2

N_EXPERTS = 32
TOP_K = 4
D_FF = 1024
SWIGLU_LIMIT = 7.0
SWIGLU_ALPHA = 1.702

LANES = 128
SUBLANES = 8
IN_Q = ATTN_WIDTH
IN_K = IN_Q + KV_WIDTH
IN_V = IN_K + KV_WIDTH
IN_Z = IN_V + SSM_INNER
IN_XBC = IN_Z + CONV_CH
IN_PAD = IN_XBC + LANES
ROPE_HALF = HEAD_DIM // 4
VMEM_LIMIT = 56 * 1024 * 1024
HIGHEST = lax.Precision.HIGHEST


def _params(n_axes, vmem=VMEM_LIMIT):
    return pltpu.CompilerParams(dimension_semantics=("arbitrary",) * n_axes, vmem_limit_bytes=vmem)


def _pick(n, prefs):
    for t in prefs:
        if n % t == 0:
            return t
    raise ValueError(f"no tile in {prefs} divides {n}")


def _sigmoid(x):
    return 1.0 / (1.0 + jnp.exp(-x))


def _dot3(x, m_bf16, left=False):
    hi = x.astype(BF16)
    r1 = x - hi.astype(F32)
    mid = r1.astype(BF16)
    lo = (r1 - mid.astype(F32)).astype(BF16)
    mm = (lambda p: jnp.dot(m_bf16, p, preferred_element_type=F32)) if left else (
        lambda p: jnp.dot(p, m_bf16, preferred_element_type=F32))
    return mm(hi) + mm(mid) + mm(lo)


def _mod_kernel(c_ref, w_ref, b_ref, o_ref):
    c = c_ref[...]
    s = c * _sigmoid(c)
    o_ref[...] = jnp.dot(s, w_ref[...], preferred_element_type=F32, precision=HIGHEST) + b_ref[...]


def _modulation(c_all, w_ada, b_ada):
    depth, d, n = w_ada.shape
    r = c_all.shape[0]
    tn = _pick(n, (1536, 1024, 512, 128))
    out = pl.pallas_call(
        _mod_kernel,
        out_shape=jax.ShapeDtypeStruct((depth, r, n), F32),
        grid=(depth, n // tn),
        in_specs=[pl.BlockSpec((r, d), lambda l, j: (0, 0)),
                  pl.BlockSpec((None, d, tn), lambda l, j: (l, 0, j)),
                  pl.BlockSpec((None, 1, tn), lambda l, j: (l, 0, j))],
        out_specs=pl.BlockSpec((None, r, tn), lambda l, j: (l, 0, j)),
        compiler_params=_params(2),
        name="adaln_mod",
    )(c_all, w_ada, b_ada.reshape(depth, 1, n))
    return out.reshape(depth, r, N_MOD, d)


def _rms_mod(x, nw, shift, scale):
    ms = jnp.mean(x * x, axis=-1, keepdims=True)
    y = x * lax.rsqrt(ms + EPS) * nw
    return y * (1.0 + scale) + shift


def _in_proj_kernel(xl_ref, xc_ref, nw_ref, mod_ref, w_ref, cos_ref, sin_ref,
                    q_ref, k_ref, v_ref, z_ref, xbc_ref, dt_ref, *, n_lat_tiles):
    i = pl.program_id(0)
    is_lat = i < n_lat_tiles
    x = jnp.where(is_lat, xl_ref[...], xc_ref[...])
    h = _rms_mod(x, nw_ref[...], mod_ref[0, 0:1, :], mod_ref[0, 1:2, :])
    p = jnp.dot(h.astype(BF16), w_ref[...], preferred_element_type=F32)
    tm = p.shape[0]
    cos = jnp.where(is_lat, cos_ref[...], 1.0)
    sin = jnp.where(is_lat, sin_ref[...], 0.0)
    lane = lax.broadcasted_iota(I32, (tm, LANES), 1)
    first_half = (lane & (2 * ROPE_HALF - 1)) < ROPE_HALF

    def rope(t):
        partner = jnp.where(first_half, pltpu.roll(t, LANES - ROPE_HALF, 1), pltpu.roll(t, ROPE_HALF, 1))
        return t * cos + partner * sin

    for j in range(ATTN_WIDTH // LANES):
        q_ref[:, j * LANES:(j + 1) * LANES] = rope(p[:, j * LANES:(j + 1) * LANES]).astype(BF16)
    k_ref[...] = rope(p[:, IN_Q:IN_K]).astype(BF16)
    v_ref[...] = p[:, IN_K:IN_V].T.astype(BF16)
    z_ref[...] = p[:, IN_V:IN_Z]
    xbc_ref[...] = p[:, IN_Z:IN_XBC]
    dt_ref[...] = p[:, IN_XBC:IN_PAD]


def _token_specs(tm, d, n_lat_tiles, ctx_tile0):
    return [pl.BlockSpec((tm, d), lambda i: (jnp.minimum(i, n_lat_tiles - 1), 0)),
            pl.BlockSpec((tm, d), lambda i: (jnp.maximum(i - n_lat_tiles, 0) + ctx_tile0, 0))]


def _in_proj(x_lat, x_ctx, ctx_row0, nw, mod_l, w_in_p, cos_t, sin_t, *, nt, n_lat, seq, batch):
    d = x_lat.shape[1]
    tm = _pick(math.gcd(seq, nt - n_lat), (512, 256, 128))
    n_pos_tiles = seq // tm
    kern = functools.partial(_in_proj_kernel, n_lat_tiles=n_lat // tm)
    row = lambda i: (i, 0)
    return pl.pallas_call(
        kern,
        out_shape=(jax.ShapeDtypeStruct((nt, ATTN_WIDTH), BF16),
                   jax.ShapeDtypeStruct((nt, KV_WIDTH), BF16),
                   jax.ShapeDtypeStruct((KV_WIDTH, nt), BF16),
                   jax.ShapeDtypeStruct((nt, SSM_INNER), F32),
                   jax.ShapeDtypeStruct((nt, CONV_CH), F32),
                   jax.ShapeDtypeStruct((nt, LANES), F32)),
        grid=(nt // tm,),
        in_specs=_token_specs(tm, d, n_lat // tm, ctx_row0 // tm) + [
                  pl.BlockSpec((1, d), lambda i: (0, 0)),
                  pl.BlockSpec((1, N_MOD, d), lambda i: (jnp.minimum(i * tm // seq, batch), 0, 0)),
                  pl.BlockSpec((d, IN_PAD), lambda i: (0, 0)),
                  pl.BlockSpec((tm, LANES), lambda i: (i % n_pos_tiles, 0)),
                  pl.BlockSpec((tm, LANES), lambda i: (i % n_pos_tiles, 0))],
        out_specs=(pl.BlockSpec((tm, ATTN_WIDTH), row), pl.BlockSpec((tm, KV_WIDTH), row),
                   pl.BlockSpec((KV_WIDTH, tm), lambda i: (0, i)), pl.BlockSpec((tm, SSM_INNER), row),
                   pl.BlockSpec((tm, CONV_CH), row), pl.BlockSpec((tm, LANES), row)),
        compiler_params=_params(1),
        name="in_proj",
    )(x_lat, x_ctx, nw, mod_l, w_in_p, cos_t, sin_t)


def _attn_kernel(sink_ref, q_ref, kp_ref, kc_ref, kn_ref, vp_ref, vc_ref, vn_ref, kx_ref, vx_ref,
                 ba_ref, bb_ref, nw_ref, o_ref):
    blk = ATT_BLOCK
    rep = N_HEADS // N_KV_HEADS
    n_ctx = kx_ref.shape[0]
    scale = HEAD_DIM ** -0.5
    windows = (((kp_ref[...], kc_ref[0:blk, :], kc_ref[blk:, :]),
                (vp_ref[...], vc_ref[:, 0:blk], vc_ref[:, blk:]), ba_ref),
               ((kc_ref[0:blk, :], kc_ref[blk:, :], kn_ref[...]),
                (vc_ref[:, 0:blk], vc_ref[:, blk:], vn_ref[...]), bb_ref))
    for w, (kwin, vwin, bias_ref) in enumerate(windows):
        q = q_ref[w * blk:(w + 1) * blk, :] * jnp.asarray(scale, BF16)
        bias = jnp.concatenate([bias_ref[...]] * rep, axis=1)
        heads = []
        for g in range(N_KV_HEADS):
            sl = slice(g * HEAD_DIM, (g + 1) * HEAD_DIM)
            qg = jnp.concatenate([q[:, (g * rep + j) * HEAD_DIM:(g * rep + j + 1) * HEAD_DIM]
                                  for j in range(rep)], axis=0)
            kg = jnp.concatenate([kx_ref[:, sl]] + [kb[:, sl] for kb in kwin], axis=0)
            vg = jnp.concatenate([vx_ref[sl, :]] + [vb[sl, :] for vb in vwin], axis=1)
            s = lax.dot_general(kg, qg, (((1,), (1,)), ((), ())), preferred_element_type=F32)
            s = jnp.concatenate([s[:n_ctx], s[n_ctx:] + bias], axis=0)
            sink = jnp.concatenate([jnp.full((1, blk), sink_ref[g * rep + j], F32) for j in range(rep)], axis=1)
            m = jnp.maximum(jnp.max(s, axis=0, keepdims=True), sink)
            e = jnp.exp(s - m)
            denom = jnp.sum(e, axis=0, keepdims=True) + jnp.exp(sink - m)
            og = jnp.dot(vg, e.astype(BF16), preferred_element_type=F32) / denom
            heads += [og[:, j * blk:(j + 1) * blk] for j in range(rep)]
        ssq = heads[0] * heads[0]
        for h in heads[1:]:
            ssq = ssq + h * h
        inv = lax.rsqrt(jnp.sum(ssq, axis=0, keepdims=True) * (1.0 / ATTN_WIDTH) + EPS)
        out_t = jnp.concatenate(heads, axis=0) * inv * nw_ref[...]
        o_ref[w * blk:(w + 1) * blk, :] = out_t.T.astype(BF16)


def _window_bias():
    blk = ATT_BLOCK
    j = jnp.arange(blk)[:, None]
    i = jnp.arange(blk)[None, :]
    zero = jnp.zeros((blk, blk), F32)
    hidden = jnp.full((blk, blk), NEG_INF, F32)
    prev = jnp.where(j >= i, 0.0, NEG_INF).astype(F32)
    nxt = jnp.where(j <= i, 0.0, NEG_INF).astype(F32)
    variants = []
    for v in range(4):
        variants.append(jnp.concatenate([hidden if v & 1 else prev, zero, hidden if v & 2 else nxt], axis=0))
    variants.append(jnp.concatenate([hidden, hidden, hidden], axis=0))
    return jnp.stack(variants)


def _attention(q, k, vt, sinks, nw, *, batch, seq, n_ctx, with_ctx_queries):
    nt = q.shape[0]
    blk = ATT_BLOCK
    assert seq % (2 * blk) == 0 and n_ctx % (2 * blk) == 0
    nb = seq // blk
    np_ = nb // 2
    ncp = n_ctx // (2 * blk)
    nq = np_ + (ncp if with_ctx_queries else 0)
    ctx_blk0 = (batch * seq) // n_ctx

    def qmap(b, n, s):
        return (jnp.where(n < np_, b * np_ + n, batch * np_ + b * ncp + (n - np_)), 0)

    own = lambda b, n: b * np_ + jnp.clip(n, 0, np_ - 1)
    edge = lambda off: (lambda b, n: b * nb + jnp.clip(2 * n + off, 0, nb - 1))

    def bias_map(which):
        def index(b, n, s):
            hidden_edge = (n == 0).astype(I32) if which == 0 else 2 * (n == np_ - 1).astype(I32)
            return (jnp.where(n < np_, hidden_edge, 4), 0, 0)
        return index

    nw_b = jnp.broadcast_to(nw.reshape(ATTN_WIDTH, 1), (ATTN_WIDTH, blk))
    bias = _window_bias()
    return pl.pallas_call(
        _attn_kernel,
        out_shape=jax.ShapeDtypeStruct((nt if with_ctx_queries else batch * seq, ATTN_WIDTH), BF16),
        grid_spec=pltpu.PrefetchScalarGridSpec(
            num_scalar_prefetch=1,
            grid=(batch, nq),
            in_specs=[pl.BlockSpec((2 * blk, ATTN_WIDTH), qmap),
                      pl.BlockSpec((blk, KV_WIDTH), lambda b, n, s: (edge(-1)(b, n), 0)),
                      pl.BlockSpec((2 * blk, KV_WIDTH), lambda b, n, s: (own(b, n), 0)),
                      pl.BlockSpec((blk, KV_WIDTH), lambda b, n, s: (edge(2)(b, n), 0)),
                      pl.BlockSpec((KV_WIDTH, blk), lambda b, n, s: (0, edge(-1)(b, n))),
                      pl.BlockSpec((KV_WIDTH, 2 * blk), lambda b, n, s: (0, own(b, n))),
                      pl.BlockSpec((KV_WIDTH, blk), lambda b, n, s: (0, edge(2)(b, n))),
                      pl.BlockSpec((n_ctx, KV_WIDTH), lambda b, n, s: (ctx_blk0 + b, 0)),
                      pl.BlockSpec((KV_WIDTH, n_ctx), lambda b, n, s: (0, ctx_blk0 + b)),
                      pl.BlockSpec((None, 3 * blk, blk), bias_map(0)),
                      pl.BlockSpec((None, 3 * blk, blk), bias_map(1)),
                      pl.BlockSpec((ATTN_WIDTH, blk), lambda b, n, s: (0, 0))],
            out_specs=pl.BlockSpec((2 * blk, ATTN_WIDTH), qmap)),
        compiler_params=_params(2),
        name="attention",
    )(sinks, q, k, k, k, vt, vt, vt, k, vt, bias, bias, nw_b)


def _conv_kernel(xp_ref, xc_ref, xn_ref, w_ref, b_ref, o_ref, *, seq, n_lat, n_ctx):
    i = pl.program_id(0)
    tb = xc_ref.shape[0]
    row0 = i * tb
    in_lat = row0 < n_lat
    local = jnp.where(in_lat, row0 % seq, (row0 - n_lat) % n_ctx)
    length = jnp.where(in_lat, seq, n_ctx)
    first = local == 0
    last = local + tb == length
    h = SUBLANES
    n_ext = tb + 2 * h
    ext = jnp.concatenate([jnp.where(first, 0.0, xp_ref[...]), xc_ref[...],
                           jnp.where(last, 0.0, xn_ref[...])], axis=0)
    acc = jnp.zeros((tb, CONV_CH), F32) + b_ref[...]
    for kk in range(CONV_W):
        shift = (CONV_W // 2 - kk) % n_ext
        src = ext if shift == 0 else pltpu.roll(ext, shift, 0)
        acc = acc + src[h:h + tb, :] * w_ref[kk:kk + 1, :]
    o_ref[...] = acc * _sigmoid(acc)


def _conv_silu(xbc, conv_w, conv_b, *, seq, n_lat, n_ctx):
    nt = xbc.shape[0]
    tb = _pick(math.gcd(seq, n_ctx), (256, 128))
    h = SUBLANES
    per = tb // h
    n_h = nt // h
    kern = functools.partial(_conv_kernel, seq=seq, n_lat=n_lat, n_ctx=n_ctx)
    return pl.pallas_call(
        kern,
        out_shape=jax.ShapeDtypeStruct((nt, CONV_CH), F32),
        grid=(nt // tb,),
        in_specs=[pl.BlockSpec((h, CONV_CH), lambda i: (jnp.maximum(i * per - 1, 0), 0)),
                  pl.BlockSpec((tb, CONV_CH), lambda i: (i, 0)),
                  pl.BlockSpec((h, CONV_CH), lambda i: (jnp.minimum((i + 1) * per, n_h - 1), 0)),
                  pl.BlockSpec((CONV_W, CONV_CH), lambda i: (0, 0)),
                  pl.BlockSpec((1, CONV_CH), lambda i: (0, 0))],
        out_specs=pl.BlockSpec((tb, CONV_CH), lambda i: (i, 0)),
        compiler_params=_params(1),
        name="conv_silu",
    )(xbc, xbc, xbc, conv_w, conv_b.reshape(1, CONV_CH))


def _ssd_chunk(u, dtraw, dtb, alog, state_ref, *, direction):
    q = CHUNK
    hp = SSM_INNER // SSM_HEADS
    per_g = SSM_HEADS // SSM_GROUPS
    gw = per_g * hp
    xs = u[:, :SSM_INNER]
    bm = u[:, SSM_INNER:SSM_INNER + SSM_GROUPS * SSM_STATE]
    cm = u[:, SSM_INNER + SSM_GROUPS * SSM_STATE:]

    xv = dtraw + dtb
    dt = jnp.maximum(xv, 0.0) + jnp.log1p(jnp.exp(-jnp.abs(xv)))
    dta = dt * (-jnp.exp(alog))

    ri = lax.broadcasted_iota(I32, (q, q), 0)
    ci = lax.broadcasted_iota(I32, (q, q), 1)
    tri = (ci <= ri) if direction == 0 else (ci >= ri)
    cs = _dot3(dta, tri.astype(BF16), left=True)
    cs_t = cs.T

    er = lax.broadcasted_iota(I32, (LANES, SSM_INNER), 0)
    ec = lax.broadcasted_iota(I32, (LANES, SSM_INNER), 1)
    expand = (er == direction * SSM_HEADS + jnp.right_shift(ec, hp.bit_length() - 1)).astype(BF16)
    cs_e = _dot3(cs, expand)
    dt_e = _dot3(dt, expand)
    last = q - 1 if direction == 0 else 0
    cs_last = cs_e[last:last + 1, :]

    xdt = (xs * dt_e).astype(BF16)
    xw = (xs * (jnp.exp(cs_last - cs_e) * dt_e)).astype(BF16)
    bm_t = bm.T.astype(BF16)
    cmb = cm.astype(BF16)
    bmb = bm.astype(BF16)
    state = state_ref[...]
    state_b = state.astype(BF16)

    y_diag = []
    y_off = []
    new_states = []
    for g in range(SSM_GROUPS):
        gs = slice(g * SSM_STATE, (g + 1) * SSM_STATE)
        cb = lax.dot_general(cmb[:, gs], bmb[:, gs], (((1,), (1,)), ((), ())), preferred_element_type=F32)
        y_off.append(jnp.dot(cmb[:, gs], state_b[:, g * gw:(g + 1) * gw], preferred_element_type=F32))
        new_states.append(jnp.dot(bm_t[gs, :], xw[:, g * gw:(g + 1) * gw], preferred_element_type=F32))
        for j in range(per_g):
            hh = g * per_g + j
            col = direction * SSM_HEADS + hh
            seg = cs[:, col:col + 1] - cs_t[col:col + 1, :]
            decay = jnp.exp(jnp.where(tri, seg, NEG_INF))
            scores = (cb * decay).astype(BF16)
            y_diag.append(jnp.dot(scores, xdt[:, hh * hp:(hh + 1) * hp], preferred_element_type=F32))
    y = jnp.concatenate(y_diag, axis=1) + jnp.exp(cs_e) * jnp.concatenate(y_off, axis=1)
    state_ref[...] = jnp.exp(cs_last) * state + jnp.concatenate(new_states, axis=1)
    return y


SSD_CHUNKS_PER_STEP = 2


def _ssd_fwd_kernel(u_ref, dt_ref, dtb_ref, alog_ref, skip_ref, y_ref, state_ref):
    @pl.when(pl.program_id(1) == 0)
    def _():
        state_ref[...] = jnp.zeros_like(state_ref)

    for j in range(u_ref.shape[0] // CHUNK):
        rows = slice(j * CHUNK, (j + 1) * CHUNK)
        u = u_ref[rows, :]
        y = _ssd_chunk(u, dt_ref[rows, :], dtb_ref[...], alog_ref[...], state_ref, direction=0)
        y_ref[rows, :] = y + skip_ref[...] * u[:, :SSM_INNER]


def _ssd_bwd_kernel(u_ref, dt_ref, dtb_ref, alog_ref, y0_ref, z_ref, nw_ref, o_ref, state_ref):
    @pl.when(pl.program_id(1) == 0)
    def _():
        state_ref[...] = jnp.zeros_like(state_ref)

    n = u_ref.shape[0] // CHUNK
    for j in range(n - 1, -1, -1):
        rows = slice(j * CHUNK, (j + 1) * CHUNK)
        y = y0_ref[rows, :] + _ssd_chunk(u_ref[rows, :], dt_ref[rows, :], dtb_ref[...], alog_ref[...],
                                         state_ref, direction=1)
        z = z_ref[rows, :]
        gt = y * (z * _sigmoid(z))
        gw = SSM_INNER // SSM_GROUPS
        outs = []
        for g in range(SSM_GROUPS):
            gg = gt[:, g * gw:(g + 1) * gw]
            ms = jnp.mean(gg * gg, axis=-1, keepdims=True)
            outs.append(gg * lax.rsqrt(ms + EPS))
        o_ref[rows, :] = (jnp.concatenate(outs, axis=1) * nw_ref[...]).astype(BF16)


def _ssd(u, dt_raw, z, dtb, alog, skip, ssm_nw, *, batch, seq, n_ctx):
    nt = u.shape[0]
    cps = SSD_CHUNKS_PER_STEP
    q = cps * CHUNK
    assert seq % q == 0 and n_ctx % q == 0
    ncl = seq // q
    ncc = n_ctx // q
    steps = ncc + ncl
    ctx0 = (batch * seq) // q

    def fmap(b, t):
        return (jnp.where(t < ncc, ctx0 + b * ncc + t, b * ncl + (t - ncc)), 0)

    def rmap(b, t):
        return (jnp.where(t < ncc, ctx0 + b * ncc + (ncc - 1 - t), b * ncl + (ncl - 1 - (t - ncc))), 0)

    const = lambda b, t: (0, 0)
    state = pltpu.VMEM((SSM_STATE, SSM_INNER), F32)
    y0 = pl.pallas_call(
        _ssd_fwd_kernel,
        out_shape=jax.ShapeDtypeStruct((nt, SSM_INNER), F32),
        grid=(batch, steps),
        in_specs=[pl.BlockSpec((q, CONV_CH), fmap), pl.BlockSpec((q, LANES), fmap),
                  pl.BlockSpec((1, LANES), const), pl.BlockSpec((1, LANES), const),
                  pl.BlockSpec((1, SSM_INNER), const)],
        out_specs=pl.BlockSpec((q, SSM_INNER), fmap),
        scratch_shapes=[state],
        compiler_params=_params(2),
        name="ssd_forward",
    )(u, dt_raw, dtb, alog, skip)
    return pl.pallas_call(
        _ssd_bwd_kernel,
        out_shape=jax.ShapeDtypeStruct((nt, SSM_INNER), BF16),
        grid=(batch, steps),
        in_specs=[pl.BlockSpec((q, CONV_CH), rmap), pl.BlockSpec((q, LANES), rmap),
                  pl.BlockSpec((1, LANES), const), pl.BlockSpec((1, LANES), const),
                  pl.BlockSpec((q, SSM_INNER), rmap), pl.BlockSpec((q, SSM_INNER), rmap),
                  pl.BlockSpec((1, SSM_INNER), const)],
        out_specs=pl.BlockSpec((q, SSM_INNER), rmap),
        scratch_shapes=[state],
        compiler_params=_params(2),
        name="ssd_backward",
    )(u, dt_raw, dtb, alog, y0, z, ssm_nw)


MOE_TILE = 256


def _out_proj_kernel(xl_ref, xc_ref, a_ref, s_ref, wa_ref, ws_ref, mod_ref, nw_ref, wr_ref, br_ref,
                     xo_ref, h_ref, idxt_ref, gate_ref, cnt_ref, *, n_lat_tiles):
    mix = (jnp.dot(a_ref[...], wa_ref[...], preferred_element_type=F32)
           + jnp.dot(s_ref[...], ws_ref[...], preferred_element_type=F32))
    x_in = jnp.where(pl.program_id(0) < n_lat_tiles, xl_ref[...], xc_ref[...])
    x = x_in + mod_ref[0, 2:3, :] * mix
    xo_ref[...] = x
    h = _rms_mod(x, nw_ref[...], mod_ref[0, 3:4, :], mod_ref[0, 4:5, :])
    h_hi = h.astype(BF16)
    h_ref[...] = h_hi
    h_lo = (h - h_hi.astype(F32)).astype(BF16)
    hh = jnp.dot(h_hi, wr_ref[...], preferred_element_type=F32)
    lh = jnp.dot(h_lo, wr_ref[:, :LANES], preferred_element_type=F32)
    logits = hh[:, :LANES] + hh[:, LANES:] + lh + br_ref[...]
    tm = logits.shape[0]
    lane = lax.broadcasted_iota(I32, (tm, LANES), 1)
    lane_f = lane.astype(F32)
    work = jnp.where(lane < N_EXPERTS, logits, -jnp.inf)
    idx_out = jnp.zeros((tm, LANES), F32)
    val_out = jnp.full((tm, LANES), -jnp.inf, F32)
    onehot = jnp.zeros((tm, LANES), F32)
    for kk in range(TOP_K):
        m = jnp.max(work, axis=-1, keepdims=True)
        sel = jnp.min(jnp.where(work == m, lane_f, float(LANES)), axis=-1, keepdims=True)
        idx_out = jnp.where(lane == kk, sel, idx_out)
        val_out = jnp.where(lane == kk, m, val_out)
        picked = lane_f == sel
        onehot = onehot + jnp.where(picked, 1.0, 0.0)
        work = jnp.where(picked, -jnp.inf, work)
    top = jnp.max(val_out, axis=-1, keepdims=True)
    e = jnp.exp(val_out - top)
    gate_ref[...] = e / jnp.sum(e, axis=-1, keepdims=True)
    idxt_ref[...] = idx_out.T[0:SUBLANES, :].astype(I32)
    ones = jnp.ones((SUBLANES, MOE_TILE), BF16)
    oh_b = onehot.astype(BF16)
    for j in range(tm // MOE_TILE):
        cnt_ref[j] = jnp.dot(ones, oh_b[j * MOE_TILE:(j + 1) * MOE_TILE, :],
                             preferred_element_type=F32).astype(I32)


def _out_proj(x_lat, x_ctx, ctx_row0, attn, ssm, w_out_a, w_out_s, mod_l, nfw, wr_hl, br_p,
              *, n_rows, n_lat, seq, batch):
    d = x_lat.shape[1]
    tm = _pick(math.gcd(seq, n_rows - n_lat) if n_rows > n_lat else seq, (512, 256))
    sub = tm // MOE_TILE
    row = lambda i: (i, 0)
    const = lambda i: (0, 0)
    kern = functools.partial(_out_proj_kernel, n_lat_tiles=n_lat // tm)
    return pl.pallas_call(
        kern,
        out_shape=(jax.ShapeDtypeStruct((n_rows, d), F32), jax.ShapeDtypeStruct((n_rows, d), BF16),
                   jax.ShapeDtypeStruct((SUBLANES, n_rows), I32), jax.ShapeDtypeStruct((n_rows, LANES), F32),
                   jax.ShapeDtypeStruct((n_rows // MOE_TILE, SUBLANES, LANES), I32)),
        grid=(n_rows // tm,),
        in_specs=_token_specs(tm, d, n_lat // tm, ctx_row0 // tm) + [
                  pl.BlockSpec((tm, ATTN_WIDTH), row), pl.BlockSpec((tm, SSM_INNER), row),
                  pl.BlockSpec((ATTN_WIDTH, d), const), pl.BlockSpec((SSM_INNER, d), const),
                  pl.BlockSpec((1, N_MOD, d), lambda i: (jnp.minimum(i * tm // seq, batch), 0, 0)),
                  pl.BlockSpec((1, d), const), pl.BlockSpec((d, 2 * LANES), const),
                  pl.BlockSpec((1, LANES), const)],
        out_specs=(pl.BlockSpec((tm, d), row), pl.BlockSpec((tm, d), row),
                   pl.BlockSpec((SUBLANES, tm), lambda i: (0, i)), pl.BlockSpec((tm, LANES), row),
                   pl.BlockSpec((sub, SUBLANES, LANES), lambda i: (i, 0, 0))),
        compiler_params=_params(1),
        name="out_proj_router",
    )(x_lat, x_ctx, attn, ssm, w_out_a, w_out_s, mod_l, nfw, wr_hl, br_p)


TAB_CNT, TAB_OFF, TAB_BASE = 0, N_EXPERTS, 2 * N_EXPERTS
ROW_TILE = D_MODEL // LANES
assert ROW_TILE == SUBLANES


def _to_row_tiles(ref, base, val):
    n = val.shape[0]
    for s in range(ROW_TILE):
        ref[pl.ds(base + s, n, stride=ROW_TILE), :] = val[:, s * LANES:(s + 1) * LANES]


def _from_row_tiles(ref, base, n):
    return jnp.concatenate([ref[pl.ds(base + s, n, stride=ROW_TILE), :] for s in range(ROW_TILE)], axis=1)


def _tile_rows(start, size):
    return pl.ds(pl.multiple_of(start * ROW_TILE, ROW_TILE), size * ROW_TILE)


def _for_each_run(tab_ref, lanes, make_copy, fn, *, enabled=None, unrolled=False):
    def per_expert(e, carry=0):
        cnt = tab_ref[0, lanes[0] + e]
        if enabled is not None:
            cnt = jnp.where(enabled, cnt, 0)

        @pl.when(cnt > 0)
        def _():
            fn(make_copy(tab_ref[0, lanes[1] + e], tab_ref[0, lanes[2] + e], cnt))
        return carry

    if unrolled:
        for e in range(N_EXPERTS):
            per_expert(e)
    else:
        lax.fori_loop(0, N_EXPERTS, per_expert, 0)


def _start(copy):
    copy.start()


def _wait(copy):
    copy.wait()


def _dispatch_kernel(tab_ref, tabp_ref, pad_ref, idxt_ref, h_ref, buf_ref, pos_ref,
                     scr_ref, zero_ref, sem, zsem, *, n_steps, group):
    i = pl.program_id(0)
    slot = i % 2
    t = h_ref.shape[0] // group
    rows = TOP_K * t
    run_lanes = (TAB_CNT, TAB_OFF, TAB_BASE)

    def copy_out(s):
        return lambda local, glob, size: pltpu.make_async_copy(
            scr_ref.at[_tile_rows(s * rows + local, size), :], buf_ref.at[_tile_rows(glob, size), :], sem.at[s])

    @pl.when(i == 0)
    def _():
        zero_ref[...] = jnp.zeros_like(zero_ref)
        zero_copy = lambda local, glob, size: pltpu.make_async_copy(
            zero_ref.at[_tile_rows(0, size), :], buf_ref.at[_tile_rows(glob, size), :], zsem)
        pad_lanes = (0, 0, N_EXPERTS)
        _for_each_run(pad_ref, pad_lanes, zero_copy, _start)
        _for_each_run(pad_ref, pad_lanes, zero_copy, _wait)
        tail_start = pad_ref[0, 2 * N_EXPERTS]
        n_tail = pad_ref[0, 2 * N_EXPERTS + 1]
        zrows = zero_ref.shape[0] // ROW_TILE

        def tail(fn):
            def body(j, carry):
                fn(zero_copy(0, tail_start + j * zrows, zrows))
                return carry
            return body

        lax.fori_loop(0, n_tail, tail(_start), 0)
        lax.fori_loop(0, n_tail, tail(_wait), 0)

    e_iota = lax.broadcasted_iota(I32, (N_EXPERTS, t), 0)
    upper = (lax.broadcasted_iota(I32, (t, t), 0) < lax.broadcasted_iota(I32, (t, t), 1)).astype(BF16)
    below = (lax.broadcasted_iota(I32, (N_EXPERTS, N_EXPERTS), 1)
             < lax.broadcasted_iota(I32, (N_EXPERTS, N_EXPERTS), 0)).astype(BF16)
    r_iota = lax.broadcasted_iota(I32, (rows, t), 0)
    for g in range(group):
        cols = slice(g * t, (g + 1) * t)
        onehots, counts, before = [], [], []
        for kk in range(TOP_K):
            oh = jnp.where(e_iota == idxt_ref[kk:kk + 1, cols], 1.0, 0.0)
            onehots.append(oh)
            counts.append(jnp.sum(oh, axis=1, keepdims=True))
            before.append(jnp.dot(oh.astype(BF16), upper, preferred_element_type=F32))
        total = counts[0] + counts[1] + counts[2] + counts[3]
        start = _dot3(jnp.broadcast_to(total, (N_EXPERTS, LANES)), below, left=True)[:, 0:1]
        pos_rows = []
        for kk in range(TOP_K):
            pos_rows.append(jnp.sum(onehots[kk] * (before[kk] + start), axis=0, keepdims=True))
            start = start + counts[kk]
        perm = jnp.zeros((rows, t), F32)
        for kk in range(TOP_K):
            perm = jnp.where(r_iota == pos_rows[kk].astype(I32), 1.0, perm)
        _to_row_tiles(scr_ref, (slot * group + g) * rows * ROW_TILE,
                      jnp.dot(perm.astype(BF16), h_ref[cols, :], preferred_element_type=F32))
        pos_t = jnp.concatenate(pos_rows + [jnp.zeros((LANES - TOP_K, t), F32)], axis=0)
        pos_ref[cols, :] = pos_t.T.astype(I32)
        _for_each_run(tab_ref.at[g], run_lanes, copy_out(slot * group + g), _start, unrolled=True)

    for g in range(group):
        _for_each_run(tabp_ref.at[g], run_lanes, copy_out((1 - slot) * group + g), _wait,
                      enabled=i > 0, unrolled=True)

    @pl.when(i == n_steps - 1)
    def _():
        for g in range(group):
            _for_each_run(tab_ref.at[g], run_lanes, copy_out(slot * group + g), _wait)


def _moe_group(n_tiles, largest=2):
    return next(g for g in (4, 2, 1) if g <= largest and n_tiles % g == 0)


def _dispatch(h2, idxt, table, padtab, *, n_rows, tme):
    nt, d = h2.shape
    t = MOE_TILE
    group = _moe_group(nt // t, largest=4)
    n_steps = nt // (t * group)
    smem = lambda shape, imap: pl.BlockSpec(shape, imap, memory_space=pltpu.SMEM)
    kern = functools.partial(_dispatch_kernel, n_steps=n_steps, group=group)
    return pl.pallas_call(
        kern,
        out_shape=(jax.ShapeDtypeStruct((n_rows * ROW_TILE, LANES), F32),
                   jax.ShapeDtypeStruct((nt, LANES), I32)),
        grid=(n_steps,),
        in_specs=[smem((group, 1, LANES), lambda i: (i, 0, 0)),
                  smem((group, 1, LANES), lambda i: (jnp.maximum(i - 1, 0), 0, 0)),
                  smem((None, 1, LANES), lambda i: (0, 0, 0)),
                  pl.BlockSpec((SUBLANES, group * t), lambda i: (0, i)),
                  pl.BlockSpec((group * t, d), lambda i: (i, 0))],
        out_specs=(pl.BlockSpec(memory_space=pl.ANY), pl.BlockSpec((group * t, LANES), lambda i: (i, 0))),
        scratch_shapes=[pltpu.VMEM((2 * group * TOP_K * t * ROW_TILE, LANES), F32),
                        pltpu.VMEM((tme * ROW_TILE, LANES), F32),
                        pltpu.SemaphoreType.DMA((2 * group,)), pltpu.SemaphoreType.DMA],
        compiler_params=_params(1),
        name="moe_dispatch",
    )(table, table, padtab, idxt, h2)


def _expert_kernel(be_ref, nu_ref, nx_ref, x_ref, wgu_hbm, bgu_ref, wd_hbm, bd_ref, o_ref,
                   wgu_f, wd_f, wgu_b, wd_b, sem, *, layer):
    i = pl.program_id(0)
    tme = x_ref.shape[0] // ROW_TILE

    def fetch(e):
        return (pltpu.make_async_copy(wgu_hbm.at[layer, e], wgu_f, sem.at[0]),
                pltpu.make_async_copy(wd_hbm.at[layer, e], wd_f, sem.at[1]))

    @pl.when(i < nu_ref[0])
    def _():
        e = be_ref[i]

        @pl.when(i == 0)
        def _():
            for c in fetch(e):
                c.start()

        @pl.when((i == 0) | (e != be_ref[jnp.maximum(i - 1, 0)]))
        def _():
            for c in fetch(e):
                c.wait()
            wgu_b[...] = wgu_f[...].astype(BF16)
            wd_b[...] = wd_f[...].astype(BF16)
            nxt = nx_ref[i]

            @pl.when(nxt >= 0)
            def _():
                for c in fetch(nxt):
                    c.start()

        x = _from_row_tiles(x_ref, 0, tme)
        gu = jnp.dot(x.astype(BF16), wgu_b[...], preferred_element_type=F32) + bgu_ref[...]
        glu = jnp.minimum(gu[:, :D_FF], SWIGLU_LIMIT)
        lin = jnp.clip(gu[:, D_FF:], -SWIGLU_LIMIT, SWIGLU_LIMIT)
        act = glu * _sigmoid(SWIGLU_ALPHA * glu) * (lin + 1.0)
        _to_row_tiles(o_ref, 0, jnp.dot(act.astype(BF16), wd_b[...], preferred_element_type=F32) + bd_ref[...])

    @pl.when(i >= nu_ref[0])
    def _():
        o_ref[...] = jnp.zeros_like(o_ref)


def _experts(buf, block_e, n_used, next_e, wgu, bgu, wd, bd, *, layer, tme):
    d = D_MODEL
    nblk = buf.shape[0] // (tme * ROW_TILE)
    xmap = lambda i, be, nu, nx: (jnp.maximum(jnp.minimum(i, nu[0] - 1), 0), 0)
    emap = lambda i, be, nu, nx: (layer, be[i], 0, 0)
    kern = functools.partial(_expert_kernel, layer=layer)
    return pl.pallas_call(
        kern,
        out_shape=jax.ShapeDtypeStruct(buf.shape, F32),
        grid_spec=pltpu.PrefetchScalarGridSpec(
            num_scalar_prefetch=3,
            grid=(nblk,),
            in_specs=[pl.BlockSpec((tme * ROW_TILE, LANES), xmap),
                      pl.BlockSpec(memory_space=pl.ANY),
                      pl.BlockSpec((None, None, 1, 2 * D_FF), emap),
                      pl.BlockSpec(memory_space=pl.ANY),
                      pl.BlockSpec((None, None, 1, d), emap)],
            out_specs=pl.BlockSpec((tme * ROW_TILE, LANES), lambda i, be, nu, nx: (i, 0)),
            scratch_shapes=[pltpu.VMEM((d, 2 * D_FF), F32), pltpu.VMEM((D_FF, d), F32),
                            pltpu.VMEM((d, 2 * D_FF), BF16), pltpu.VMEM((D_FF, d), BF16),
                            pltpu.SemaphoreType.DMA((2,))]),
        compiler_params=_params(1),
        name="expert_ffn",
    )(block_e, n_used, next_e, buf, wgu, bgu, wd, bd)


def _combine_kernel(tab_ref, tabn_ref, pos_ref, gate_ref, x_ref, mod_ref, fw_ref, ob_ref, o_ref,
                    scr_ref, sem, *, n_steps, group, final_norm):
    i = pl.program_id(0)
    slot = i % 2
    t = x_ref.shape[0] // group
    rows = TOP_K * t
    run_lanes = (TAB_CNT, TAB_OFF, TAB_BASE)

    def copy_in(s):
        return lambda local, glob, size: pltpu.make_async_copy(
            ob_ref.at[_tile_rows(glob, size), :], scr_ref.at[_tile_rows(s * rows + local, size), :], sem.at[s])

    @pl.when(i == 0)
    def _():
        for g in range(group):
            _for_each_run(tab_ref.at[g], run_lanes, copy_in(slot * group + g), _start)

    for g in range(group):
        _for_each_run(tabn_ref.at[g], run_lanes, copy_in((1 - slot) * group + g), _start,
                      enabled=i + 1 < n_steps, unrolled=True)

    lane = lax.broadcasted_iota(I32, (t, rows), 1)
    for g in range(group):
        tok = slice(g * t, (g + 1) * t)
        pw = jnp.zeros((t, rows), F32)
        for kk in range(TOP_K):
            pw = jnp.where(lane == pos_ref[tok, kk:kk + 1], gate_ref[tok, kk:kk + 1], pw)
        pw_hi = pw.astype(BF16)
        pw_lo = (pw - pw_hi.astype(F32)).astype(BF16)

        _for_each_run(tab_ref.at[g], run_lanes, copy_in(slot * group + g), _wait, unrolled=True)
        y = _from_row_tiles(scr_ref, (slot * group + g) * rows * ROW_TILE, rows).astype(BF16)
        f2 = jnp.dot(jnp.concatenate([pw_hi, pw_lo], axis=0), y, preferred_element_type=F32)
        f = f2[:t] + f2[t:]
        x = x_ref[tok, :] + mod_ref[0, 5:6, :] * f
        if final_norm:
            ms = jnp.mean(x * x, axis=-1, keepdims=True)
            x = x * lax.rsqrt(ms + EPS) * fw_ref[...]
        o_ref[tok, :] = x


def _combine(xt, out_buf, pos, gates, table, mod_l, fw, *, seq, batch, final_norm):
    nt, d = xt.shape
    t = MOE_TILE
    group = _moe_group(nt // t)
    n_steps = nt // (t * group)
    gt = group * t
    row = lambda i: (i, 0)
    smem = lambda imap: pl.BlockSpec((group, 1, LANES), imap, memory_space=pltpu.SMEM)
    kern = functools.partial(_combine_kernel, n_steps=n_steps, group=group, final_norm=final_norm)
    return pl.pallas_call(
        kern,
        out_shape=jax.ShapeDtypeStruct((nt, d), F32),
        grid=(n_steps,),
        in_specs=[smem(lambda i: (i, 0, 0)), smem(lambda i: (jnp.minimum(i + 1, n_steps - 1), 0, 0)),
                  pl.BlockSpec((gt, LANES), row), pl.BlockSpec((gt, LANES), row), pl.BlockSpec((gt, d), row),
                  pl.BlockSpec((1, N_MOD, d), lambda i: (jnp.minimum(i * gt // seq, batch), 0, 0)),
                  pl.BlockSpec((1, d), lambda i: (0, 0)),
                  pl.BlockSpec(memory_space=pl.ANY)],
        out_specs=pl.BlockSpec((gt, d), row),
        scratch_shapes=[pltpu.VMEM((2 * group * TOP_K * t * ROW_TILE, LANES), F32),
                        pltpu.SemaphoreType.DMA((2 * group,))],
        compiler_params=_params(1),
        name="moe_combine",
    )(table, table, pos, gates, xt, mod_l, fw, out_buf)


def _route_tables(counts, *, tme, n_blocks):
    cnt = counts[:, 0, :N_EXPERTS]
    total = jnp.sum(cnt, axis=0)
    padded = (total + tme - 1) // tme * tme
    pad_end = jnp.cumsum(padded)
    pad_start = pad_end - padded
    base = pad_start[None, :] + jnp.cumsum(cnt, axis=0) - cnt
    off = jnp.cumsum(cnt, axis=1) - cnt
    table = jnp.concatenate([cnt, off, base, jnp.zeros_like(cnt)], axis=1).astype(I32)[:, None, :]
    tail = jnp.stack([pad_end[-1], n_blocks - pad_end[-1] // tme])
    padtab = jnp.concatenate([padded - total, pad_start + total, tail,
                              jnp.zeros((LANES - 2 * N_EXPERTS - 2,), I32)]).astype(I32)[None, None, :]
    n_used = (pad_end[-1] // tme).astype(I32).reshape(1)
    block_row0 = jnp.arange(n_blocks, dtype=I32) * tme
    block_e = jnp.minimum(jnp.sum((pad_end[None, :] <= block_row0[:, None]).astype(I32), axis=1),
                          N_EXPERTS - 1).astype(I32)
    ids = jnp.arange(N_EXPERTS, dtype=I32)
    later = (ids[None, :] > ids[:, None]) & (total[None, :] > 0)
    next_used = jnp.min(jnp.where(later, ids[None, :], N_EXPERTS), axis=1)
    next_used = jnp.where(next_used == N_EXPERTS, -1, next_used).astype(I32)
    return table, padtab, block_e, n_used, next_used[block_e]


def _rope_tables(seq):
    rows = seq // GRID_W
    row_pos = jnp.repeat(jnp.arange(rows, dtype=I32), GRID_W).astype(F32)
    col_pos = jnp.tile(jnp.arange(GRID_W, dtype=I32), rows).astype(F32)
    n_freq = HEAD_DIM // 4
    inv_freq = ROPE_BASE ** (-jnp.arange(n_freq, dtype=F32) / n_freq)
    lane = jnp.arange(LANES)
    f = lane % n_freq
    use_col = (lane % HEAD_DIM) >= HEAD_DIM // 2
    ang = jnp.where(use_col[None, :], col_pos[:, None], row_pos[:, None]) * inv_freq[f][None, :]
    first_half = (lane % (2 * ROPE_HALF)) < ROPE_HALF
    return jnp.cos(ang), jnp.where(first_half[None, :], -jnp.sin(ang), jnp.sin(ang))


def kernel(x, c, ctx, c_ctx, w_ada, b_ada, norm_mix_w, norm_ffn_w, w_in, conv_w, conv_b, dt_bias, a_log, d_skip, ssm_norm_w, attn_sinks, attn_norm_w, w_out, w_router, b_router, w_gate_up, b_gate_up, w_down, b_down, final_norm_w):
    batch, seq, d = x.shape
    n_ctx = ctx.shape[1]
    depth = w_ada.shape[0]
    n_lat = batch * seq
    nt = n_lat + batch * n_ctx
    assert d == D_MODEL and seq % ATT_BLOCK == 0 and n_ctx % ATT_BLOCK == 0 and n_lat % n_ctx == 0

    r_mod = -(-(batch + 1) // SUBLANES) * SUBLANES
    c_all = jnp.zeros((r_mod, d), F32).at[:batch].set(c).at[batch].set(c_ctx)
    mod = _modulation(c_all, w_ada, b_ada)
    cos_t, sin_t = _rope_tables(seq)

    tme = 512
    assert nt % MOE_TILE == 0 and n_lat % MOE_TILE == 0
    hp = SSM_INNER // SSM_HEADS
    bgu = b_gate_up.reshape(depth, N_EXPERTS, 1, 2 * D_FF)
    bdn = b_down.reshape(depth, N_EXPERTS, 1, d)

    x_lat, x_ctx, ctx_row0 = x.reshape(n_lat, d), ctx.reshape(batch * n_ctx, d), 0
    for l in range(depth):
        last = l == depth - 1
        w_in_p = jnp.pad(w_in[l], ((0, 0), (0, IN_PAD - w_in.shape[2]))).astype(BF16)
        q, k, v, z, xbc, dt_raw = _in_proj(x_lat, x_ctx, ctx_row0, norm_mix_w[l].reshape(1, d), mod[l], w_in_p,
                                           cos_t, sin_t, nt=nt, n_lat=n_lat, seq=seq, batch=batch)
        attn = _attention(q, k, v, attn_sinks[l], attn_norm_w[l].reshape(1, ATTN_WIDTH),
                          batch=batch, seq=seq, n_ctx=n_ctx, with_ctx_queries=not last)
        u = _conv_silu(xbc, conv_w[l], conv_b[l], seq=seq, n_lat=n_lat, n_ctx=n_ctx)
        pad16 = lambda t: jnp.pad(t.reshape(1, N_DIRS * SSM_HEADS), ((0, 0), (0, LANES - N_DIRS * SSM_HEADS)))
        ssm = _ssd(u, dt_raw, z, pad16(dt_bias[l]), pad16(a_log[l]),
                   jnp.repeat(d_skip[l], hp).reshape(1, SSM_INNER), ssm_norm_w[l].reshape(1, SSM_INNER),
                   batch=batch, seq=seq, n_ctx=n_ctx)
        n_rows = n_lat if last else nt
        n_blocks = -(-(n_rows * TOP_K) // tme) + N_EXPERTS
        w_o = w_out[l].astype(BF16)
        wr_p = jnp.pad(w_router[l], ((0, 0), (0, LANES - N_EXPERTS)))
        wr_hi = wr_p.astype(BF16)
        wr_hl = jnp.concatenate([wr_hi, (wr_p - wr_hi.astype(F32)).astype(BF16)], axis=1)
        br_p = jnp.pad(b_router[l].reshape(1, N_EXPERTS), ((0, 0), (0, LANES - N_EXPERTS)))
        xt, h2, idxt, gates, counts = _out_proj(x_lat, x_ctx, ctx_row0, attn, ssm, w_o[:ATTN_WIDTH],
                                                w_o[ATTN_WIDTH:], mod[l], norm_ffn_w[l].reshape(1, d),
                                                wr_hl, br_p,
                                                n_rows=n_rows, n_lat=n_lat, seq=seq, batch=batch)
        table, padtab, block_e, n_used, next_e = _route_tables(counts, tme=tme, n_blocks=n_blocks)
        buf, pos = _dispatch(h2, idxt, table, padtab, n_rows=n_blocks * tme, tme=tme)
        out_buf = _experts(buf, block_e, n_used, next_e, w_gate_up, bgu, w_down, bdn, layer=l, tme=tme)
        xt = _combine(xt, out_buf, pos, gates, table, mod[l], final_norm_w.reshape(1, d),
                      seq=seq, batch=batch, final_norm=last)
        x_lat, x_ctx, ctx_row0 = xt, xt, n_lat
    return xt.reshape(batch, seq, d)
```

```python
import functools
import math

import jax
import jax.numpy as jnp
from jax import lax
from jax.experimental import pallas as pl
from jax.experimental.pallas import tpu as pltpu

F32 = jnp.float32
BF16 = jnp.bfloat16
I32 = jnp.int32

D_MODEL = 1024
GRID_W = 64
N_MOD = 6
EPS = 1e-6
NEG_INF = -1e30

HEAD_DIM = 64
ATTN_WIDTH = 512
N_HEADS = 8
N_KV_HEADS = 2
KV_WIDTH = 128
ATT_BLOCK = 128
ROPE_BASE = 10000.0

SSM_INNER = 512
SSM_HEADS = 8
SSM_GROUPS = 2
SSM_STATE = 64
CONV_W = 5
CONV_CH = 768
CHUNK = 128
N_DIRS = 2

N_EXPERTS = 32
TOP_K = 4
D_FF = 1024
SWIGLU_LIMIT = 7.0
SWIGLU_ALPHA = 1.702

LANES = 128
SUBLANES = 8
IN_Q = ATTN_WIDTH
IN_K = IN_Q + KV_WIDTH
IN_V = IN_K + KV_WIDTH
IN_Z = IN_V + SSM_INNER
IN_XBC = IN_Z + CONV_CH
IN_PAD = IN_XBC + LANES
ROPE_HALF = HEAD_DIM // 4
VMEM_LIMIT = 56 * 1024 * 1024
HIGHEST = lax.Precision.HIGHEST


def _params(n_axes, vmem=VMEM_LIMIT):
    return pltpu.CompilerParams(dimension_semantics=("arbitrary",) * n_axes, vmem_limit_bytes=vmem)


def _pick(n, prefs):
    for t in prefs:
        if n % t == 0:
            return t
    raise ValueError(f"no tile in {prefs} divides {n}")


def _sigmoid(x):
    return 1.0 / (1.0 + jnp.exp(-x))


def _dot3(x, m_bf16, left=False):
    hi = x.astype(BF16)
    r1 = x - hi.astype(F32)
    mid = r1.astype(BF16)
    lo = (r1 - mid.astype(F32)).astype(BF16)
    mm = (lambda p: jnp.dot(m_bf16, p, preferred_element_type=F32)) if left else (
        lambda p: jnp.dot(p, m_bf16, preferred_element_type=F32))
    return mm(hi) + mm(mid) + mm(lo)


def _mod_kernel(c_ref, w_ref, b_ref, o_ref):
    c = c_ref[...]
    s = c * _sigmoid(c)
    o_ref[...] = jnp.dot(s, w_ref[...], preferred_element_type=F32, precision=HIGHEST) + b_ref[...]


def _modulation(c_all, w_ada, b_ada):
    depth, d, n = w_ada.shape
    r = c_all.shape[0]
    tn = _pick(n, (1536, 1024, 512, 128))
    out = pl.pallas_call(
        _mod_kernel,
        out_shape=jax.ShapeDtypeStruct((depth, r, n), F32),
        grid=(depth, n // tn),
        in_specs=[pl.BlockSpec((r, d), lambda l, j: (0, 0)),
                  pl.BlockSpec((None, d, tn), lambda l, j: (l, 0, j)),
                  pl.BlockSpec((None, 1, tn), lambda l, j: (l, 0, j))],
        out_specs=pl.BlockSpec((None, r, tn), lambda l, j: (l, 0, j)),
        compiler_params=_params(2),
        name="adaln_mod",
    )(c_all, w_ada, b_ada.reshape(depth, 1, n))
    return out.reshape(depth, r, N_MOD, d)


def _rms_mod(x, nw, shift, scale):
    ms = jnp.mean(x * x, axis=-1, keepdims=True)
    y = x * lax.rsqrt(ms + EPS) * nw
    return y * (1.0 + scale) + shift


def _in_proj_kernel(xl_ref, xc_ref, nw_ref, mod_ref, w_ref, cos_ref, sin_ref,
                    q_ref, k_ref, v_ref, z_ref, xbc_ref, dt_ref, *, n_lat_tiles):
    i = pl.program_id(0)
    is_lat = i < n_lat_tiles
    x = jnp.where(is_lat, xl_ref[...], xc_ref[...])
    h = _rms_mod(x, nw_ref[...], mod_ref[0, 0:1, :], mod_ref[0, 1:2, :])
    p = jnp.dot(h.astype(BF16), w_ref[...], preferred_element_type=F32)
    tm = p.shape[0]
    cos = jnp.where(is_lat, cos_ref[...], 1.0)
    sin = jnp.where(is_lat, sin_ref[...], 0.0)
    lane = lax.broadcasted_iota(I32, (tm, LANES), 1)
    first_half = (lane & (2 * ROPE_HALF - 1)) < ROPE_HALF

    def rope(t):
        partner = jnp.where(first_half, pltpu.roll(t, LANES - ROPE_HALF, 1), pltpu.roll(t, ROPE_HALF, 1))
        return t * cos + partner * sin

    for j in range(ATTN_WIDTH // LANES):
        q_ref[:, j * LANES:(j + 1) * LANES] = rope(p[:, j * LANES:(j + 1) * LANES]).astype(BF16)
    k_ref[...] = rope(p[:, IN_Q:IN_K]).astype(BF16)
    v_ref[...] = p[:, IN_K:IN_V].T.astype(BF16)
    z_ref[...] = p[:, IN_V:IN_Z]
    xbc_ref[...] = p[:, IN_Z:IN_XBC]
    dt_ref[...] = p[:, IN_XBC:IN_PAD]


def _token_specs(tm, d, n_lat_tiles, ctx_tile0):
    return [pl.BlockSpec((tm, d), lambda i: (jnp.minimum(i, n_lat_tiles - 1), 0)),
            pl.BlockSpec((tm, d), lambda i: (jnp.maximum(i - n_lat_tiles, 0) + ctx_tile0, 0))]


def _in_proj(x_lat, x_ctx, ctx_row0, nw, mod_l, w_in_p, cos_t, sin_t, *, nt, n_lat, seq, batch):
    d = x_lat.shape[1]
    tm = _pick(math.gcd(seq, nt - n_lat), (512, 256, 128))
    n_pos_tiles = seq // tm
    kern = functools.partial(_in_proj_kernel, n_lat_tiles=n_lat // tm)
    row = lambda i: (i, 0)
    return pl.pallas_call(
        kern,
        out_shape=(jax.ShapeDtypeStruct((nt, ATTN_WIDTH), BF16),
                   jax.ShapeDtypeStruct((nt, KV_WIDTH), BF16),
                   jax.ShapeDtypeStruct((KV_WIDTH, nt), BF16),
                   jax.ShapeDtypeStruct((nt, SSM_INNER), F32),
                   jax.ShapeDtypeStruct((nt, CONV_CH), F32),
                   jax.ShapeDtypeStruct((nt, LANES), F32)),
        grid=(nt // tm,),
        in_specs=_token_specs(tm, d, n_lat // tm, ctx_row0 // tm) + [
                  pl.BlockSpec((1, d), lambda i: (0, 0)),
                  pl.BlockSpec((1, N_MOD, d), lambda i: (jnp.minimum(i * tm // seq, batch), 0, 0)),
                  pl.BlockSpec((d, IN_PAD), lambda i: (0, 0)),
                  pl.BlockSpec((tm, LANES), lambda i: (i % n_pos_tiles, 0)),
                  pl.BlockSpec((tm, LANES), lambda i: (i % n_pos_tiles, 0))],
        out_specs=(pl.BlockSpec((tm, ATTN_WIDTH), row), pl.BlockSpec((tm, KV_WIDTH), row),
                   pl.BlockSpec((KV_WIDTH, tm), lambda i: (0, i)), pl.BlockSpec((tm, SSM_INNER), row),
                   pl.BlockSpec((tm, CONV_CH), row), pl.BlockSpec((tm, LANES), row)),
        compiler_params=_params(1),
        name="in_proj",
    )(x_lat, x_ctx, nw, mod_l, w_in_p, cos_t, sin_t)


def _attn_kernel(sink_ref, q_ref, kp_ref, kc_ref, kn_ref, vp_ref, vc_ref, vn_ref, kx_ref, vx_ref,
                 ba_ref, bb_ref, nw_ref, o_ref):
    blk = ATT_BLOCK
    rep = N_HEADS // N_KV_HEADS
    n_ctx = kx_ref.shape[0]
    scale = HEAD_DIM ** -0.5
    windows = (((kp_ref[...], kc_ref[0:blk, :], kc_ref[blk:, :]),
                (vp_ref[...], vc_ref[:, 0:blk], vc_ref[:, blk:]), ba_ref),
               ((kc_ref[0:blk, :], kc_ref[blk:, :], kn_ref[...]),
                (vc_ref[:, 0:blk], vc_ref[:, blk:], vn_ref[...]), bb_ref))
    for w, (kwin, vwin, bias_ref) in enumerate(windows):
        q = q_ref[w * blk:(w + 1) * blk, :] * jnp.asarray(scale, BF16)
        bias = jnp.concatenate([bias_ref[...]] * rep, axis=1)
        heads = []
        for g in range(N_KV_HEADS):
            sl = slice(g * HEAD_DIM, (g + 1) * HEAD_DIM)
            qg = jnp.concatenate([q[:, (g * rep + j) * HEAD_DIM:(g * rep + j + 1) * HEAD_DIM]
                                  for j in range(rep)], axis=0)
            kg = jnp.concatenate([kx_ref[:, sl]] + [kb[:, sl] for kb in kwin], axis=0)
            vg = jnp.concatenate([vx_ref[sl, :]] + [vb[sl, :] for vb in vwin], axis=1)
            s = lax.dot_general(kg, qg, (((1,), (1,)), ((), ())), preferred_element_type=F32)
            s = jnp.concatenate([s[:n_ctx], s[n_ctx:] + bias], axis=0)
            sink = jnp.concatenate([jnp.full((1, blk), sink_ref[g * rep + j], F32) for j in range(rep)], axis=1)
            m = jnp.maximum(jnp.max(s, axis=0, keepdims=True), sink)
            e = jnp.exp(s - m)
            denom = jnp.sum(e, axis=0, keepdims=True) + jnp.exp(sink - m)
            og = jnp.dot(vg, e.astype(BF16), preferred_element_type=F32) / denom
            heads += [og[:, j * blk:(j + 1) * blk] for j in range(rep)]
        ssq = heads[0] * heads[0]
        for h in heads[1:]:
            ssq = ssq + h * h
        inv = lax.rsqrt(jnp.sum(ssq, axis=0, keepdims=True) * (1.0 / ATTN_WIDTH) + EPS)
        out_t = jnp.concatenate(heads, axis=0) * inv * nw_ref[...]
        o_ref[w * blk:(w + 1) * blk, :] = out_t.T.astype(BF16)


def _window_bias():
    blk = ATT_BLOCK
    j = jnp.arange(blk)[:, None]
    i = jnp.arange(blk)[None, :]
    zero = jnp.zeros((blk, blk), F32)
    hidden = jnp.full((blk, blk), NEG_INF, F32)
    prev = jnp.where(j >= i, 0.0, NEG_INF).astype(F32)
    nxt = jnp.where(j <= i, 0.0, NEG_INF).astype(F32)
    variants = []
    for v in range(4):
        variants.append(jnp.concatenate([hidden if v & 1 else prev, zero, hidden if v & 2 else nxt], axis=0))
    variants.append(jnp.concatenate([hidden, hidden, hidden], axis=0))
    return jnp.stack(variants)


def _attention(q, k, vt, sinks, nw, *, batch, seq, n_ctx, with_ctx_queries):
    nt = q.shape[0]
    blk = ATT_BLOCK
    assert seq % (2 * blk) == 0 and n_ctx % (2 * blk) == 0
    nb = seq // blk
    np_ = nb // 2
    ncp = n_ctx // (2 * blk)
    nq = np_ + (ncp if with_ctx_queries else 0)
    ctx_blk0 = (batch * seq) // n_ctx

    def qmap(b, n, s):
        return (jnp.where(n < np_, b * np_ + n, batch * np_ + b * ncp + (n - np_)), 0)

    own = lambda b, n: b * np_ + jnp.clip(n, 0, np_ - 1)
    edge = lambda off: (lambda b, n: b * nb + jnp.clip(2 * n + off, 0, nb - 1))

    def bias_map(which):
        def index(b, n, s):
            hidden_edge = (n == 0).astype(I32) if which == 0 else 2 * (n == np_ - 1).astype(I32)
            return (jnp.where(n < np_, hidden_edge, 4), 0, 0)
        return index

    nw_b = jnp.broadcast_to(nw.reshape(ATTN_WIDTH, 1), (ATTN_WIDTH, blk))
    bias = _window_bias()
    return pl.pallas_call(
        _attn_kernel,
        out_shape=jax.ShapeDtypeStruct((nt if with_ctx_queries else batch * seq, ATTN_WIDTH), BF16),
        grid_spec=pltpu.PrefetchScalarGridSpec(
            num_scalar_prefetch=1,
            grid=(batch, nq),
            in_specs=[pl.BlockSpec((2 * blk, ATTN_WIDTH), qmap),
                      pl.BlockSpec((blk, KV_WIDTH), lambda b, n, s: (edge(-1)(b, n), 0)),
                      pl.BlockSpec((2 * blk, KV_WIDTH), lambda b, n, s: (own(b, n), 0)),
                      pl.BlockSpec((blk, KV_WIDTH), lambda b, n, s: (edge(2)(b, n), 0)),
                      pl.BlockSpec((KV_WIDTH, blk), lambda b, n, s: (0, edge(-1)(b, n))),
                      pl.BlockSpec((KV_WIDTH, 2 * blk), lambda b, n, s: (0, own(b, n))),
                      pl.BlockSpec((KV_WIDTH, blk), lambda b, n, s: (0, edge(2)(b, n))),
                      pl.BlockSpec((n_ctx, KV_WIDTH), lambda b, n, s: (ctx_blk0 + b, 0)),
                      pl.BlockSpec((KV_WIDTH, n_ctx), lambda b, n, s: (0, ctx_blk0 + b)),
                      pl.BlockSpec((None, 3 * blk, blk), bias_map(0)),
                      pl.BlockSpec((None, 3 * blk, blk), bias_map(1)),
                      pl.BlockSpec((ATTN_WIDTH, blk), lambda b, n, s: (0, 0))],
            out_specs=pl.BlockSpec((2 * blk, ATTN_WIDTH), qmap)),
        compiler_params=_params(2),
        name="attention",
    )(sinks, q, k, k, k, vt, vt, vt, k, vt, bias, bias, nw_b)


def _conv_kernel(xp_ref, xc_ref, xn_ref, w_ref, b_ref, o_ref, *, seq, n_lat, n_ctx):
    i = pl.program_id(0)
    tb = xc_ref.shape[0]
    row0 = i * tb
    in_lat = row0 < n_lat
    local = jnp.where(in_lat, row0 % seq, (row0 - n_lat) % n_ctx)
    length = jnp.where(in_lat, seq, n_ctx)
    first = local == 0
    last = local + tb == length
    h = SUBLANES
    n_ext = tb + 2 * h
    ext = jnp.concatenate([jnp.where(first, 0.0, xp_ref[...]), xc_ref[...],
                           jnp.where(last, 0.0, xn_ref[...])], axis=0)
    acc = jnp.zeros((tb, CONV_CH), F32) + b_ref[...]
    for kk in range(CONV_W):
        shift = (CONV_W // 2 - kk) % n_ext
        src = ext if shift == 0 else pltpu.roll(ext, shift, 0)
        acc = acc + src[h:h + tb, :] * w_ref[kk:kk + 1, :]
    o_ref[...] = acc * _sigmoid(acc)


def _conv_silu(xbc, conv_w, conv_b, *, seq, n_lat, n_ctx):
    nt = xbc.shape[0]
    tb = _pick(math.gcd(seq, n_ctx), (256, 128))
    h = SUBLANES
    per = tb // h
    n_h = nt // h
    kern = functools.partial(_conv_kernel, seq=seq, n_lat=n_lat, n_ctx=n_ctx)
    return pl.pallas_call(
        kern,
        out_shape=jax.ShapeDtypeStruct((nt, CONV_CH), F32),
        grid=(nt // tb,),
        in_specs=[pl.BlockSpec((h, CONV_CH), lambda i: (jnp.maximum(i * per - 1, 0), 0)),
                  pl.BlockSpec((tb, CONV_CH), lambda i: (i, 0)),
                  pl.BlockSpec((h, CONV_CH), lambda i: (jnp.minimum((i + 1) * per, n_h - 1), 0)),
                  pl.BlockSpec((CONV_W, CONV_CH), lambda i: (0, 0)),
                  pl.BlockSpec((1, CONV_CH), lambda i: (0, 0))],
        out_specs=pl.BlockSpec((tb, CONV_CH), lambda i: (i, 0)),
        compiler_params=_params(1),
        name="conv_silu",
    )(xbc, xbc, xbc, conv_w, conv_b.reshape(1, CONV_CH))


def _ssd_chunk(u, dtraw, dtb, alog, state_ref, *, direction):
    q = CHUNK
    hp = SSM_INNER // SSM_HEADS
    per_g = SSM_HEADS // SSM_GROUPS
    gw = per_g * hp
    xs = u[:, :SSM_INNER]
    bm = u[:, SSM_INNER:SSM_INNER + SSM_GROUPS * SSM_STATE]
    cm = u[:, SSM_INNER + SSM_GROUPS * SSM_STATE:]

    xv = dtraw + dtb
    dt = jnp.maximum(xv, 0.0) + jnp.log1p(jnp.exp(-jnp.abs(xv)))
    dta = dt * (-jnp.exp(alog))

    ri = lax.broadcasted_iota(I32, (q, q), 0)
    ci = lax.broadcasted_iota(I32, (q, q), 1)
    tri = (ci <= ri) if direction == 0 else (ci >= ri)
    cs = _dot3(dta, tri.astype(BF16), left=True)
    cs_t = cs.T

    er = lax.broadcasted_iota(I32, (LANES, SSM_INNER), 0)
    ec = lax.broadcasted_iota(I32, (LANES, SSM_INNER), 1)
    expand = (er == direction * SSM_HEADS + jnp.right_shift(ec, hp.bit_length() - 1)).astype(BF16)
    cs_e = _dot3(cs, expand)
    dt_e = _dot3(dt, expand)
    last = q - 1 if direction == 0 else 0
    cs_last = cs_e[last:last + 1, :]

    xdt = (xs * dt_e).astype(BF16)
    xw = (xs * (jnp.exp(cs_last - cs_e) * dt_e)).astype(BF16)
    bm_t = bm.T.astype(BF16)
    cmb = cm.astype(BF16)
    bmb = bm.astype(BF16)
    state = state_ref[...]
    state_b = state.astype(BF16)

    y_diag = []
    y_off = []
    new_states = []
    for g in range(SSM_GROUPS):
        gs = slice(g * SSM_STATE, (g + 1) * SSM_STATE)
        cb = lax.dot_general(cmb[:, gs], bmb[:, gs], (((1,), (1,)), ((), ())), preferred_element_type=F32)
        y_off.append(jnp.dot(cmb[:, gs], state_b[:, g * gw:(g + 1) * gw], preferred_element_type=F32))
        new_states.append(jnp.dot(bm_t[gs, :], xw[:, g * gw:(g + 1) * gw], preferred_element_type=F32))
        for j in range(per_g):
            hh = g * per_g + j
            col = direction * SSM_HEADS + hh
            seg = cs[:, col:col + 1] - cs_t[col:col + 1, :]
            decay = jnp.exp(jnp.where(tri, seg, NEG_INF))
            scores = (cb * decay).astype(BF16)
            y_diag.append(jnp.dot(scores, xdt[:, hh * hp:(hh + 1) * hp], preferred_element_type=F32))
    y = jnp.concatenate(y_diag, axis=1) + jnp.exp(cs_e) * jnp.concatenate(y_off, axis=1)
    state_ref[...] = jnp.exp(cs_last) * state + jnp.concatenate(new_states, axis=1)
    return y


SSD_CHUNKS_PER_STEP = 2


def _ssd_fwd_kernel(u_ref, dt_ref, dtb_ref, alog_ref, skip_ref, y_ref, state_ref):
    @pl.when(pl.program_id(1) == 0)
    def _():
        state_ref[...] = jnp.zeros_like(state_ref)

    for j in range(u_ref.shape[0] // CHUNK):
        rows = slice(j * CHUNK, (j + 1) * CHUNK)
        u = u_ref[rows, :]
        y = _ssd_chunk(u, dt_ref[rows, :], dtb_ref[...], alog_ref[...], state_ref, direction=0)
        y_ref[rows, :] = y + skip_ref[...] * u[:, :SSM_INNER]


def _ssd_bwd_kernel(u_ref, dt_ref, dtb_ref, alog_ref, y0_ref, z_ref, nw_ref, o_ref, state_ref):
    @pl.when(pl.program_id(1) == 0)
    def _():
        state_ref[...] = jnp.zeros_like(state_ref)

    n = u_ref.shape[0] // CHUNK
    for j in range(n - 1, -1, -1):
        rows = slice(j * CHUNK, (j + 1) * CHUNK)
        y = y0_ref[rows, :] + _ssd_chunk(u_ref[rows, :], dt_ref[rows, :], dtb_ref[...], alog_ref[...],
                                         state_ref, direction=1)
        z = z_ref[rows, :]
        gt = y * (z * _sigmoid(z))
        gw = SSM_INNER // SSM_GROUPS
        outs = []
        for g in range(SSM_GROUPS):
            gg = gt[:, g * gw:(g + 1) * gw]
            ms = jnp.mean(gg * gg, axis=-1, keepdims=True)
            outs.append(gg * lax.rsqrt(ms + EPS))
        o_ref[rows, :] = (jnp.concatenate(outs, axis=1) * nw_ref[...]).astype(BF16)


def _ssd(u, dt_raw, z, dtb, alog, skip, ssm_nw, *, batch, seq, n_ctx):
    nt = u.shape[0]
    cps = SSD_CHUNKS_PER_STEP
    q = cps * CHUNK
    assert seq % q == 0 and n_ctx % q == 0
    ncl = seq // q
    ncc = n_ctx // q
    steps = ncc + ncl
    ctx0 = (batch * seq) // q

    def fmap(b, t):
        return (jnp.where(t < ncc, ctx0 + b * ncc + t, b * ncl + (t - ncc)), 0)

    def rmap(b, t):
        return (jnp.where(t < ncc, ctx0 + b * ncc + (ncc - 1 - t), b * ncl + (ncl - 1 - (t - ncc))), 0)

    const = lambda b, t: (0, 0)
    state = pltpu.VMEM((SSM_STATE, SSM_INNER), F32)
    y0 = pl.pallas_call(
        _ssd_fwd_kernel,
        out_shape=jax.ShapeDtypeStruct((nt, SSM_INNER), F32),
        grid=(batch, steps),
        in_specs=[pl.BlockSpec((q, CONV_CH), fmap), pl.BlockSpec((q, LANES), fmap),
                  pl.BlockSpec((1, LANES), const), pl.BlockSpec((1, LANES), const),
                  pl.BlockSpec((1, SSM_INNER), const)],
        out_specs=pl.BlockSpec((q, SSM_INNER), fmap),
        scratch_shapes=[state],
        compiler_params=_params(2),
        name="ssd_forward",
    )(u, dt_raw, dtb, alog, skip)
    return pl.pallas_call(
        _ssd_bwd_kernel,
        out_shape=jax.ShapeDtypeStruct((nt, SSM_INNER), BF16),
        grid=(batch, steps),
        in_specs=[pl.BlockSpec((q, CONV_CH), rmap), pl.BlockSpec((q, LANES), rmap),
                  pl.BlockSpec((1, LANES), const), pl.BlockSpec((1, LANES), const),
                  pl.BlockSpec((q, SSM_INNER), rmap), pl.BlockSpec((q, SSM_INNER), rmap),
                  pl.BlockSpec((1, SSM_INNER), const)],
        out_specs=pl.BlockSpec((q, SSM_INNER), rmap),
        scratch_shapes=[state],
        compiler_params=_params(2),
        name="ssd_backward",
    )(u, dt_raw, dtb, alog, y0, z, ssm_nw)


MOE_TILE = 256


def _out_proj_kernel(xl_ref, xc_ref, a_ref, s_ref, wa_ref, ws_ref, mod_ref, nw_ref, wr_ref, br_ref,
                     xo_ref, h_ref, idxt_ref, gate_ref, cnt_ref, *, n_lat_tiles):
    mix = (jnp.dot(a_ref[...], wa_ref[...], preferred_element_type=F32)
           + jnp.dot(s_ref[...], ws_ref[...], preferred_element_type=F32))
    x_in = jnp.where(pl.program_id(0) < n_lat_tiles, xl_ref[...], xc_ref[...])
    x = x_in + mod_ref[0, 2:3, :] * mix
    xo_ref[...] = x
    h = _rms_mod(x, nw_ref[...], mod_ref[0, 3:4, :], mod_ref[0, 4:5, :])
    h_hi = h.astype(BF16)
    h_ref[...] = h_hi
    h_lo = (h - h_hi.astype(F32)).astype(BF16)
    hh = jnp.dot(h_hi, wr_ref[...], preferred_element_type=F32)
    lh = jnp.dot(h_lo, wr_ref[:, :LANES], preferred_element_type=F32)
    logits = hh[:, :LANES] + hh[:, LANES:] + lh + br_ref[...]
    tm = logits.shape[0]
    work = logits.T[0:N_EXPERTS, :]
    e_iota = lax.broadcasted_iota(I32, (N_EXPERTS, tm), 0).astype(F32)
    idx_rows, val_rows = [], []
    onehot_t = jnp.zeros((N_EXPERTS, tm), F32)
    for kk in range(TOP_K):
        m = jnp.max(work, axis=0, keepdims=True)
        sel = jnp.min(jnp.where(work == m, e_iota, float(N_EXPERTS)), axis=0, keepdims=True)
        picked = e_iota == sel
        onehot_t = onehot_t + jnp.where(picked, 1.0, 0.0)
        work = jnp.where(picked, -jnp.inf, work)
        idx_rows.append(sel)
        val_rows.append(m)
    exps = [jnp.exp(v - val_rows[0]) for v in val_rows]
    total = exps[0] + exps[1] + exps[2] + exps[3]
    gates_t = jnp.concatenate([e / total for e in exps] + [jnp.zeros((LANES - TOP_K, tm), F32)], axis=0)
    gate_ref[...] = gates_t.T
    idxt_ref[...] = jnp.concatenate(idx_rows + [jnp.zeros((SUBLANES - TOP_K, tm), F32)],
                                    axis=0).astype(I32)
    ones = jnp.ones((SUBLANES, MOE_TILE), BF16)
    oh_b = jnp.concatenate([onehot_t, jnp.zeros((LANES - N_EXPERTS, tm), F32)], axis=0).astype(BF16)
    for j in range(tm // MOE_TILE):
        cnt_ref[j] = lax.dot_general(ones, oh_b[:, j * MOE_TILE:(j + 1) * MOE_TILE], (((1,), (1,)), ((), ())),
                                     preferred_element_type=F32).astype(I32)


def _out_proj(x_lat, x_ctx, ctx_row0, attn, ssm, w_out_a, w_out_s, mod_l, nfw, wr_hl, br_p,
              *, n_rows, n_lat, seq, batch):
    d = x_lat.shape[1]
    tm = _pick(math.gcd(seq, n_rows - n_lat) if n_rows > n_lat else seq, (512, 256))
    sub = tm // MOE_TILE
    row = lambda i: (i, 0)
    const = lambda i: (0, 0)
    kern = functools.partial(_out_proj_kernel, n_lat_tiles=n_lat // tm)
    return pl.pallas_call(
        kern,
        out_shape=(jax.ShapeDtypeStruct((n_rows, d), F32), jax.ShapeDtypeStruct((n_rows, d), BF16),
                   jax.ShapeDtypeStruct((SUBLANES, n_rows), I32), jax.ShapeDtypeStruct((n_rows, LANES), F32),
                   jax.ShapeDtypeStruct((n_rows // MOE_TILE, SUBLANES, LANES), I32)),
        grid=(n_rows // tm,),
        in_specs=_token_specs(tm, d, n_lat // tm, ctx_row0 // tm) + [
                  pl.BlockSpec((tm, ATTN_WIDTH), row), pl.BlockSpec((tm, SSM_INNER), row),
                  pl.BlockSpec((ATTN_WIDTH, d), const), pl.BlockSpec((SSM_INNER, d), const),
                  pl.BlockSpec((1, N_MOD, d), lambda i: (jnp.minimum(i * tm // seq, batch), 0, 0)),
                  pl.BlockSpec((1, d), const), pl.BlockSpec((d, 2 * LANES), const),
                  pl.BlockSpec((1, LANES), const)],
        out_specs=(pl.BlockSpec((tm, d), row), pl.BlockSpec((tm, d), row),
                   pl.BlockSpec((SUBLANES, tm), lambda i: (0, i)), pl.BlockSpec((tm, LANES), row),
                   pl.BlockSpec((sub, SUBLANES, LANES), lambda i: (i, 0, 0))),
        compiler_params=_params(1),
        name="out_proj_router",
    )(x_lat, x_ctx, attn, ssm, w_out_a, w_out_s, mod_l, nfw, wr_hl, br_p)


TAB_CNT, TAB_OFF, TAB_BASE = 0, N_EXPERTS, 2 * N_EXPERTS
ROW_TILE = D_MODEL // LANES
assert ROW_TILE == SUBLANES


def _to_row_tiles(ref, base, val):
    n = val.shape[0]
    for s in range(ROW_TILE):
        ref[pl.ds(base + s, n, stride=ROW_TILE), :] = val[:, s * LANES:(s + 1) * LANES]


def _from_row_tiles(ref, base, n):
    return jnp.concatenate([ref[pl.ds(base + s, n, stride=ROW_TILE), :] for s in range(ROW_TILE)], axis=1)


def _tile_rows(start, size):
    return pl.ds(pl.multiple_of(start * ROW_TILE, ROW_TILE), size * ROW_TILE)


def _for_each_run(tab_ref, lanes, make_copy, fn, *, enabled=None, unrolled=False):
    def per_expert(e, carry=0):
        cnt = tab_ref[0, lanes[0] + e]
        if enabled is not None:
            cnt = jnp.where(enabled, cnt, 0)

        @pl.when(cnt > 0)
        def _():
            fn(make_copy(tab_ref[0, lanes[1] + e], tab_ref[0, lanes[2] + e], cnt))
        return carry

    if unrolled:
        for e in range(N_EXPERTS):
            per_expert(e)
    else:
        lax.fori_loop(0, N_EXPERTS, per_expert, 0)


def _start(copy):
    copy.start()


def _wait(copy):
    copy.wait()


def _dispatch_kernel(tab_ref, tabp_ref, pad_ref, idxt_ref, h_ref, buf_ref, pos_ref,
                     scr_ref, zero_ref, sem, zsem, *, n_steps, group):
    i = pl.program_id(0)
    slot = i % 2
    t = h_ref.shape[0] // group
    rows = TOP_K * t
    run_lanes = (TAB_CNT, TAB_OFF, TAB_BASE)

    def copy_out(s):
        return lambda local, glob, size: pltpu.make_async_copy(
            scr_ref.at[_tile_rows(s * rows + local, size), :], buf_ref.at[_tile_rows(glob, size), :], sem.at[s])

    @pl.when(i == 0)
    def _():
        zero_ref[...] = jnp.zeros_like(zero_ref)
        zero_copy = lambda local, glob, size: pltpu.make_async_copy(
            zero_ref.at[_tile_rows(0, size), :], buf_ref.at[_tile_rows(glob, size), :], zsem)
        pad_lanes = (0, 0, N_EXPERTS)
        _for_each_run(pad_ref, pad_lanes, zero_copy, _start)
        _for_each_run(pad_ref, pad_lanes, zero_copy, _wait)
        tail_start = pad_ref[0, 2 * N_EXPERTS]
        n_tail = pad_ref[0, 2 * N_EXPERTS + 1]
        zrows = zero_ref.shape[0] // ROW_TILE

        def tail(fn):
            def body(j, carry):
                fn(zero_copy(0, tail_start + j * zrows, zrows))
                return carry
            return body

        lax.fori_loop(0, n_tail, tail(_start), 0)
        lax.fori_loop(0, n_tail, tail(_wait), 0)

    e_iota = lax.broadcasted_iota(I32, (N_EXPERTS, t), 0)
    upper = (lax.broadcasted_iota(I32, (t, t), 0) < lax.broadcasted_iota(I32, (t, t), 1)).astype(BF16)
    below = (lax.broadcasted_iota(I32, (N_EXPERTS, N_EXPERTS), 1)
             < lax.broadcasted_iota(I32, (N_EXPERTS, N_EXPERTS), 0)).astype(BF16)
    r_iota = lax.broadcasted_iota(I32, (rows, t), 0)
    for g in range(group):
        cols = slice(g * t, (g + 1) * t)
        onehots, counts, before = [], [], []
        for kk in range(TOP_K):
            oh = jnp.where(e_iota == idxt_ref[kk:kk + 1, cols], 1.0, 0.0)
            onehots.append(oh)
            counts.append(jnp.sum(oh, axis=1, keepdims=True))
            before.append(jnp.dot(oh.astype(BF16), upper, preferred_element_type=F32))
        total = counts[0] + counts[1] + counts[2] + counts[3]
        start = _dot3(jnp.broadcast_to(total, (N_EXPERTS, LANES)), below, left=True)[:, 0:1]
        pos_rows = []
        for kk in range(TOP_K):
            pos_rows.append(jnp.sum(onehots[kk] * (before[kk] + start), axis=0, keepdims=True))
            start = start + counts[kk]
        perm = jnp.zeros((rows, t), F32)
        for kk in range(TOP_K):
            perm = jnp.where(r_iota == pos_rows[kk].astype(I32), 1.0, perm)
        _to_row_tiles(scr_ref, (slot * group + g) * rows * ROW_TILE,
                      jnp.dot(perm.astype(BF16), h_ref[cols, :], preferred_element_type=F32))
        pos_t = jnp.concatenate(pos_rows + [jnp.zeros((LANES - TOP_K, t), F32)], axis=0)
        pos_ref[cols, :] = pos_t.T.astype(I32)
        _for_each_run(tab_ref.at[g], run_lanes, copy_out(slot * group + g), _start, unrolled=True)

    for g in range(group):
        _for_each_run(tabp_ref.at[g], run_lanes, copy_out((1 - slot) * group + g), _wait,
                      enabled=i > 0, unrolled=True)

    @pl.when(i == n_steps - 1)
    def _():
        for g in range(group):
            _for_each_run(tab_ref.at[g], run_lanes, copy_out(slot * group + g), _wait)


def _moe_group(n_tiles, largest=2):
    return next(g for g in (4, 2, 1) if g <= largest and n_tiles % g == 0)


def _dispatch(h2, idxt, table, padtab, *, n_rows, tme):
    nt, d = h2.shape
    t = MOE_TILE
    group = _moe_group(nt // t, largest=4)
    n_steps = nt // (t * group)
    smem = lambda shape, imap: pl.BlockSpec(shape, imap, memory_space=pltpu.SMEM)
    kern = functools.partial(_dispatch_kernel, n_steps=n_steps, group=group)
    return pl.pallas_call(
        kern,
        out_shape=(jax.ShapeDtypeStruct((n_rows * ROW_TILE, LANES), F32),
                   jax.ShapeDtypeStruct((nt, LANES), I32)),
        grid=(n_steps,),
        in_specs=[smem((group, 1, LANES), lambda i: (i, 0, 0)),
                  smem((group, 1, LANES), lambda i: (jnp.maximum(i - 1, 0), 0, 0)),
                  smem((None, 1, LANES), lambda i: (0, 0, 0)),
                  pl.BlockSpec((SUBLANES, group * t), lambda i: (0, i)),
                  pl.BlockSpec((group * t, d), lambda i: (i, 0))],
        out_specs=(pl.BlockSpec(memory_space=pl.ANY), pl.BlockSpec((group * t, LANES), lambda i: (i, 0))),
        scratch_shapes=[pltpu.VMEM((2 * group * TOP_K * t * ROW_TILE, LANES), F32),
                        pltpu.VMEM((tme * ROW_TILE, LANES), F32),
                        pltpu.SemaphoreType.DMA((2 * group,)), pltpu.SemaphoreType.DMA],
        compiler_params=_params(1),
        name="moe_dispatch",
    )(table, table, padtab, idxt, h2)


def _expert_kernel(be_ref, nu_ref, nx_ref, x_ref, wgu_hbm, bgu_ref, wd_hbm, bd_ref, o_ref,
                   wgu_f, wd_f, wgu_b, wd_b, sem, *, layer):
    i = pl.program_id(0)
    tme = x_ref.shape[0] // ROW_TILE

    def fetch(e):
        return (pltpu.make_async_copy(wgu_hbm.at[layer, e], wgu_f, sem.at[0]),
                pltpu.make_async_copy(wd_hbm.at[layer, e], wd_f, sem.at[1]))

    @pl.when(i < nu_ref[0])
    def _():
        e = be_ref[i]

        @pl.when(i == 0)
        def _():
            for c in fetch(e):
                c.start()

        @pl.when((i == 0) | (e != be_ref[jnp.maximum(i - 1, 0)]))
        def _():
            for c in fetch(e):
                c.wait()
            wgu_b[...] = wgu_f[...].astype(BF16)
            wd_b[...] = wd_f[...].astype(BF16)
            nxt = nx_ref[i]

            @pl.when(nxt >= 0)
            def _():
                for c in fetch(nxt):
                    c.start()

        x = _from_row_tiles(x_ref, 0, tme)
        gu = jnp.dot(x.astype(BF16), wgu_b[...], preferred_element_type=F32) + bgu_ref[...]
        glu = jnp.minimum(gu[:, :D_FF], SWIGLU_LIMIT)
        lin = jnp.clip(gu[:, D_FF:], -SWIGLU_LIMIT, SWIGLU_LIMIT)
        act = glu * _sigmoid(SWIGLU_ALPHA * glu) * (lin + 1.0)
        _to_row_tiles(o_ref, 0, jnp.dot(act.astype(BF16), wd_b[...], preferred_element_type=F32) + bd_ref[...])

    @pl.when(i >= nu_ref[0])
    def _():
        o_ref[...] = jnp.zeros_like(o_ref)


def _experts(buf, block_e, n_used, next_e, wgu, bgu, wd, bd, *, layer, tme):
    d = D_MODEL
    nblk = buf.shape[0] // (tme * ROW_TILE)
    xmap = lambda i, be, nu, nx: (jnp.maximum(jnp.minimum(i, nu[0] - 1), 0), 0)
    emap = lambda i, be, nu, nx: (layer, be[i], 0, 0)
    kern = functools.partial(_expert_kernel, layer=layer)
    return pl.pallas_call(
        kern,
        out_shape=jax.ShapeDtypeStruct(buf.shape, F32),
        grid_spec=pltpu.PrefetchScalarGridSpec(
            num_scalar_prefetch=3,
            grid=(nblk,),
            in_specs=[pl.BlockSpec((tme * ROW_TILE, LANES), xmap),
                      pl.BlockSpec(memory_space=pl.ANY),
                      pl.BlockSpec((None, None, 1, 2 * D_FF), emap),
                      pl.BlockSpec(memory_space=pl.ANY),
                      pl.BlockSpec((None, None, 1, d), emap)],
            out_specs=pl.BlockSpec((tme * ROW_TILE, LANES), lambda i, be, nu, nx: (i, 0)),
            scratch_shapes=[pltpu.VMEM((d, 2 * D_FF), F32), pltpu.VMEM((D_FF, d), F32),
                            pltpu.VMEM((d, 2 * D_FF), BF16), pltpu.VMEM((D_FF, d), BF16),
                            pltpu.SemaphoreType.DMA((2,))]),
        compiler_params=_params(1),
        name="expert_ffn",
    )(block_e, n_used, next_e, buf, wgu, bgu, wd, bd)


def _combine_kernel(tab_ref, tabn_ref, pos_ref, gate_ref, x_ref, mod_ref, fw_ref, ob_ref, o_ref,
                    scr_ref, sem, *, n_steps, group, final_norm):
    i = pl.program_id(0)
    slot = i % 2
    t = x_ref.shape[0] // group
    rows = TOP_K * t
    run_lanes = (TAB_CNT, TAB_OFF, TAB_BASE)

    def copy_in(s):
        return lambda local, glob, size: pltpu.make_async_copy(
            ob_ref.at[_tile_rows(glob, size), :], scr_ref.at[_tile_rows(s * rows + local, size), :], sem.at[s])

    @pl.when(i == 0)
    def _():
        for g in range(group):
            _for_each_run(tab_ref.at[g], run_lanes, copy_in(slot * group + g), _start)

    for g in range(group):
        _for_each_run(tabn_ref.at[g], run_lanes, copy_in((1 - slot) * group + g), _start,
                      enabled=i + 1 < n_steps, unrolled=True)

    lane = lax.broadcasted_iota(I32, (t, rows), 1)
    for g in range(group):
        tok = slice(g * t, (g + 1) * t)
        pw = jnp.zeros((t, rows), F32)
        for kk in range(TOP_K):
            pw = jnp.where(lane == pos_ref[tok, kk:kk + 1], gate_ref[tok, kk:kk + 1], pw)
        pw_hi = pw.astype(BF16)
        pw_lo = (pw - pw_hi.astype(F32)).astype(BF16)

        _for_each_run(tab_ref.at[g], run_lanes, copy_in(slot * group + g), _wait, unrolled=True)
        y = _from_row_tiles(scr_ref, (slot * group + g) * rows * ROW_TILE, rows).astype(BF16)
        f2 = jnp.dot(jnp.concatenate([pw_hi, pw_lo], axis=0), y, preferred_element_type=F32)
        f = f2[:t] + f2[t:]
        x = x_ref[tok, :] + mod_ref[0, 5:6, :] * f
        if final_norm:
            ms = jnp.mean(x * x, axis=-1, keepdims=True)
            x = x * lax.rsqrt(ms + EPS) * fw_ref[...]
        o_ref[tok, :] = x


def _combine(xt, out_buf, pos, gates, table, mod_l, fw, *, seq, batch, final_norm):
    nt, d = xt.shape
    t = MOE_TILE
    group = _moe_group(nt // t)
    n_steps = nt // (t * group)
    gt = group * t
    row = lambda i: (i, 0)
    smem = lambda imap: pl.BlockSpec((group, 1, LANES), imap, memory_space=pltpu.SMEM)
    kern = functools.partial(_combine_kernel, n_steps=n_steps, group=group, final_norm=final_norm)
    return pl.pallas_call(
        kern,
        out_shape=jax.ShapeDtypeStruct((nt, d), F32),
        grid=(n_steps,),
        in_specs=[smem(lambda i: (i, 0, 0)), smem(lambda i: (jnp.minimum(i + 1, n_steps - 1), 0, 0)),
                  pl.BlockSpec((gt, LANES), row), pl.BlockSpec((gt, LANES), row), pl.BlockSpec((gt, d), row),
                  pl.BlockSpec((1, N_MOD, d), lambda i: (jnp.minimum(i * gt // seq, batch), 0, 0)),
                  pl.BlockSpec((1, d), lambda i: (0, 0)),
                  pl.BlockSpec(memory_space=pl.ANY)],
        out_specs=pl.BlockSpec((gt, d), row),
        scratch_shapes=[pltpu.VMEM((2 * group * TOP_K * t * ROW_TILE, LANES), F32),
                        pltpu.SemaphoreType.DMA((2 * group,))],
        compiler_params=_params(1),
        name="moe_combine",
    )(table, table, pos, gates, xt, mod_l, fw, out_buf)


def _route_tables(counts, *, tme, n_blocks):
    cnt = counts[:, 0, :N_EXPERTS]
    total = jnp.sum(cnt, axis=0)
    padded = (total + tme - 1) // tme * tme
    pad_end = jnp.cumsum(padded)
    pad_start = pad_end - padded
    base = pad_start[None, :] + jnp.cumsum(cnt, axis=0) - cnt
    off = jnp.cumsum(cnt, axis=1) - cnt
    table = jnp.concatenate([cnt, off, base, jnp.zeros_like(cnt)], axis=1).astype(I32)[:, None, :]
    tail = jnp.stack([pad_end[-1], n_blocks - pad_end[-1] // tme])
    padtab = jnp.concatenate([padded - total, pad_start + total, tail,
                              jnp.zeros((LANES - 2 * N_EXPERTS - 2,), I32)]).astype(I32)[None, None, :]
    n_used = (pad_end[-1] // tme).astype(I32).reshape(1)
    block_row0 = jnp.arange(n_blocks, dtype=I32) * tme
    block_e = jnp.minimum(jnp.sum((pad_end[None, :] <= block_row0[:, None]).astype(I32), axis=1),
                          N_EXPERTS - 1).astype(I32)
    ids = jnp.arange(N_EXPERTS, dtype=I32)
    later = (ids[None, :] > ids[:, None]) & (total[None, :] > 0)
    next_used = jnp.min(jnp.where(later, ids[None, :], N_EXPERTS), axis=1)
    next_used = jnp.where(next_used == N_EXPERTS, -1, next_used).astype(I32)
    return table, padtab, block_e, n_used, next_used[block_e]


def _rope_tables(seq):
    rows = seq // GRID_W
    row_pos = jnp.repeat(jnp.arange(rows, dtype=I32), GRID_W).astype(F32)
    col_pos = jnp.tile(jnp.arange(GRID_W, dtype=I32), rows).astype(F32)
    n_freq = HEAD_DIM // 4
    inv_freq = ROPE_BASE ** (-jnp.arange(n_freq, dtype=F32) / n_freq)
    lane = jnp.arange(LANES)
    f = lane % n_freq
    use_col = (lane % HEAD_DIM) >= HEAD_DIM // 2
    ang = jnp.where(use_col[None, :], col_pos[:, None], row_pos[:, None]) * inv_freq[f][None, :]
    first_half = (lane % (2 * ROPE_HALF)) < ROPE_HALF
    return jnp.cos(ang), jnp.where(first_half[None, :], -jnp.sin(ang), jnp.sin(ang))


def kernel(x, c, ctx, c_ctx, w_ada, b_ada, norm_mix_w, norm_ffn_w, w_in, conv_w, conv_b, dt_bias, a_log, d_skip, ssm_norm_w, attn_sinks, attn_norm_w, w_out, w_router, b_router, w_gate_up, b_gate_up, w_down, b_down, final_norm_w):
    batch, seq, d = x.shape
    n_ctx = ctx.shape[1]
    depth = w_ada.shape[0]
    n_lat = batch * seq
    nt = n_lat + batch * n_ctx
    assert d == D_MODEL and seq % ATT_BLOCK == 0 and n_ctx % ATT_BLOCK == 0 and n_lat % n_ctx == 0

    r_mod = -(-(batch + 1) // SUBLANES) * SUBLANES
    c_all = jnp.zeros((r_mod, d), F32).at[:batch].set(c).at[batch].set(c_ctx)
    mod = _modulation(c_all, w_ada, b_ada)
    cos_t, sin_t = _rope_tables(seq)

    tme = 512
    assert nt % MOE_TILE == 0 and n_lat % MOE_TILE == 0
    hp = SSM_INNER // SSM_HEADS
    bgu = b_gate_up.reshape(depth, N_EXPERTS, 1, 2 * D_FF)
    bdn = b_down.reshape(depth, N_EXPERTS, 1, d)

    x_lat, x_ctx, ctx_row0 = x.reshape(n_lat, d), ctx.reshape(batch * n_ctx, d), 0
    for l in range(depth):
        last = l == depth - 1
        w_in_p = jnp.pad(w_in[l], ((0, 0), (0, IN_PAD - w_in.shape[2]))).astype(BF16)
        q, k, v, z, xbc, dt_raw = _in_proj(x_lat, x_ctx, ctx_row0, norm_mix_w[l].reshape(1, d), mod[l], w_in_p,
                                           cos_t, sin_t, nt=nt, n_lat=n_lat, seq=seq, batch=batch)
        attn = _attention(q, k, v, attn_sinks[l], attn_norm_w[l].reshape(1, ATTN_WIDTH),
                          batch=batch, seq=seq, n_ctx=n_ctx, with_ctx_queries=not last)
        u = _conv_silu(xbc, conv_w[l], conv_b[l], seq=seq, n_lat=n_lat, n_ctx=n_ctx)
        pad16 = lambda t: jnp.pad(t.reshape(1, N_DIRS * SSM_HEADS), ((0, 0), (0, LANES - N_DIRS * SSM_HEADS)))
        ssm = _ssd(u, dt_raw, z, pad16(dt_bias[l]), pad16(a_log[l]),
                   jnp.repeat(d_skip[l], hp).reshape(1, SSM_INNER), ssm_norm_w[l].reshape(1, SSM_INNER),
                   batch=batch, seq=seq, n_ctx=n_ctx)
        n_rows = n_lat if last else nt
        n_blocks = -(-(n_rows * TOP_K) // tme) + N_EXPERTS
        w_o = w_out[l].astype(BF16)
        wr_p = jnp.pad(w_router[l], ((0, 0), (0, LANES - N_EXPERTS)))
        wr_hi = wr_p.astype(BF16)
        wr_hl = jnp.concatenate([wr_hi, (wr_p - wr_hi.astype(F32)).astype(BF16)], axis=1)
        br_p = jnp.pad(b_router[l].reshape(1, N_EXPERTS), ((0, 0), (0, LANES - N_EXPERTS)))
        xt, h2, idxt, gates, counts = _out_proj(x_lat, x_ctx, ctx_row0, attn, ssm, w_o[:ATTN_WIDTH],
                                                w_o[ATTN_WIDTH:], mod[l], norm_ffn_w[l].reshape(1, d),
                                                wr_hl, br_p,
                                                n_rows=n_rows, n_lat=n_lat, seq=seq, batch=batch)
        table, padtab, block_e, n_used, next_e = _route_tables(counts, tme=tme, n_blocks=n_blocks)
        buf, pos = _dispatch(h2, idxt, table, padtab, n_rows=n_blocks * tme, tme=tme)
        out_buf = _experts(buf, block_e, n_used, next_e, w_gate_up, bgu, w_down, bdn, layer=l, tme=tme)
        xt = _combine(xt, out_buf, pos, gates, table, mod[l], final_norm_w.reshape(1, d),
                      seq=seq, batch=batch, final_norm=last)
        x_lat, x_ctx, ctx_row0 = xt, xt, n_lat
    return xt.reshape(batch, seq, d)
```
